```python
import math
import jax, jax.numpy as jnp
from jax import lax
import numpy as np


D_MODEL = 2048
BATCH = 2
SEQ = 4096
DEPTH = 2
DEC_BATCH = 32
DEC_SEQ = 8
PAST_LEN = 16384
PAGE_SIZE = 128

N_MIXERS = 2
N_A_LAYERS = (DEPTH + 1) // 2
N_B_LAYERS = DEPTH // 2
WINDOW = 128
HEAD_DIM = 64
N_HEADS = D_MODEL // HEAD_DIM
N_KV_HEADS = N_HEADS // 8
GROUP = N_HEADS // N_KV_HEADS
ROPE_THETA = 10000.0
M_HEADS = 4
M_QK_DIM = D_MODEL // (2 * M_HEADS)
M_V_DIM = D_MODEL // M_HEADS
M_CHUNK = 64
F_BIAS = 3.0
D_FF = 4 * D_MODEL
CONV_W = 3
EPS = 1e-6
NEG_INF = -1e30

kernel_name = 'swa_sink_mlstm_convffn_step'


def _rmsnorm(x, g):
    xf = x.astype(jnp.float32)
    r = xf * lax.rsqrt(jnp.mean(xf * xf, axis=-1, keepdims=True) + EPS)
    return (r * g.astype(jnp.float32)).astype(x.dtype)


def _rope(x, pos):
    half = HEAD_DIM // 2
    inv = ROPE_THETA ** (-jnp.arange(half, dtype=jnp.float32) / half)
    ang = pos.astype(jnp.float32)[:, None] * inv[None, :]
    cos = jnp.cos(ang)[:, None, :]
    sin = jnp.sin(ang)[:, None, :]
    xf = x.astype(jnp.float32)
    x1, x2 = xf[..., :half], xf[..., half:]
    return jnp.concatenate([x1 * cos - x2 * sin, x2 * cos + x1 * sin], axis=-1).astype(x.dtype)


def _window_mask(q_pos, k_pos):
    d = q_pos[..., :, None] - k_pos[..., None, :]
    return (d >= 0) & (d < WINDOW) & (k_pos[..., None, :] >= 0)


def _swa_qkv(x, pos, w_qkv, b_qkv):
    B, T, _ = x.shape
    qkv = x @ w_qkv + b_qkv
    q, k, v = jnp.split(qkv, [N_HEADS * HEAD_DIM, (N_HEADS + N_KV_HEADS) * HEAD_DIM], axis=-1)
    q = _rope(q.reshape(B, T, N_HEADS, HEAD_DIM), pos)
    k = _rope(k.reshape(B, T, N_KV_HEADS, HEAD_DIM), pos)
    v = v.reshape(B, T, N_KV_HEADS, HEAD_DIM)
    return q, k, v


def _swa_attend(q, k, v, mask, sinks):
    s = jnp.einsum('...qhgd,...khd->...hgqk', q, k).astype(jnp.float32) * (HEAD_DIM ** -0.5)
    s = jnp.where(mask[..., None, None, :, :], s, NEG_INF)
    sink = jnp.broadcast_to(sinks.astype(jnp.float32).reshape(N_KV_HEADS, GROUP, 1, 1), s.shape[:-1] + (1,))
    p = jax.nn.softmax(jnp.concatenate([s, sink], axis=-1), axis=-1)[..., :-1]
    return jnp.einsum('...hgqk,...khd->...qhgd', p.astype(v.dtype), v)


def _swa_prompt(x, w_qkv, b_qkv, w_o, b_o, sinks):
    B, T, _ = x.shape
    nb = T // WINDOW
    pos = jnp.arange(T)
    q, k, v = _swa_qkv(x, pos, w_qkv, b_qkv)
    qb = q.reshape(B, nb, WINDOW, N_KV_HEADS, GROUP, HEAD_DIM)
    kb = k.reshape(B, nb, WINDOW, N_KV_HEADS, HEAD_DIM)
    vb = v.reshape(B, nb, WINDOW, N_KV_HEADS, HEAD_DIM)
    pad = jnp.zeros_like(kb[:, :1])
    kk = jnp.concatenate([jnp.concatenate([pad, kb[:, :-1]], axis=1), kb], axis=2)
    vv = jnp.concatenate([jnp.concatenate([pad, vb[:, :-1]], axis=1), vb], axis=2)
    q_pos = pos.reshape(nb, WINDOW)
    k_pos = (jnp.arange(nb)[:, None] - 1) * WINDOW + jnp.arange(2 * WINDOW)[None, :]
    mask = _window_mask(q_pos, k_pos)
    o = _swa_attend(qb, kk, vv, mask, sinks).reshape(B, T, N_HEADS * HEAD_DIM)
    return o @ w_o + b_o, k[:, T - WINDOW:], v[:, T - WINDOW:]


def _swa_sample(x, ck, cv, w_qkv, b_qkv, w_o, b_o, sinks):
    DB, T, _ = x.shape
    wc = ck.shape[1]
    q_pos = PAST_LEN + jnp.arange(T)
    q, k, v = _swa_qkv(x, q_pos, w_qkv, b_qkv)
    keys = jnp.concatenate([ck.astype(k.dtype), k], axis=1)
    vals = jnp.concatenate([cv.astype(v.dtype), v], axis=1)
    k_pos = jnp.concatenate([PAST_LEN - wc + jnp.arange(wc), q_pos])
    mask = _window_mask(q_pos, k_pos)
    o = _swa_attend(q.reshape(DB, T, N_KV_HEADS, GROUP, HEAD_DIM), keys, vals, mask, sinks)
    y = o.reshape(DB, T, N_HEADS * HEAD_DIM) @ w_o + b_o
    return y, keys[:, -WINDOW:], vals[:, -WINDOW:]


def _mlstm_chunkwise(q, k, v, log_i, log_f, C0, n0, m0):
    Bsz, H, T, _ = q.shape
    L = math.gcd(T, M_CHUNK)
    nc = T // L

    def chunks(t):
        return jnp.moveaxis(t.reshape(t.shape[:2] + (nc, L) + t.shape[3:]), 2, 0)

    qc, kc, vc, lic = chunks(q), chunks(k), chunks(v), chunks(log_i)
    bc = jnp.cumsum(chunks(log_f), axis=-1)
    causal = jnp.tril(jnp.ones((L, L), dtype=bool))

    def step(carry, inp):
        C, n, m = carry
        qx, kx, vx, li, b = inp
        dmat = jnp.where(causal, b[..., :, None] - b[..., None, :] + li[..., None, :], -jnp.inf)
        inter = b + m[..., None]
        m_row = jnp.maximum(inter, jnp.max(dmat, axis=-1))
        sm = jnp.einsum('bhld,bhsd->bhls', qx, kx) * jnp.exp(dmat - m_row[..., None])
        w_inter = jnp.exp(inter - m_row)
        num = jnp.einsum('bhls,bhsv->bhlv', sm, vx) + w_inter[..., None] * jnp.einsum('bhld,bhdv->bhlv', qx, C)
        den = jnp.sum(sm, axis=-1) + w_inter * jnp.einsum('bhld,bhd->bhl', qx, n)
        h = num / jnp.maximum(jnp.abs(den), jnp.exp(-m_row))[..., None]
        b_last = b[..., -1]
        g = b_last[..., None] - b + li
        m_new = jnp.maximum(b_last + m, jnp.max(g, axis=-1))
        wk = jnp.exp(g - m_new[..., None])
        decay = jnp.exp(b_last + m - m_new)
        C_new = decay[..., None, None] * C + jnp.einsum('bhs,bhsd,bhsv->bhdv', wk, kx, vx)
        n_new = decay[..., None] * n + jnp.einsum('bhs,bhsd->bhd', wk, kx)
        return (C_new, n_new, m_new), h

    (C, n, m), hs = lax.scan(step, (C0, n0, m0), (qc, kc, vc, lic, bc))
    h = jnp.moveaxis(hs, 0, 2).reshape(Bsz, H, T, -1)
    return h, C, n, m


def _mlstm(x, C0, n0, m0, w_in, b_gates, norm_g, w_out):
    Bsz, T, _ = x.shape
    qk_w = M_HEADS * M_QK_DIM
    v_w = M_HEADS * M_V_DIM
    proj = x @ w_in
    q, k, v, o, gates = jnp.split(proj, [qk_w, 2 * qk_w, 2 * qk_w + v_w, 2 * qk_w + 2 * v_w], axis=-1)
    gates = (gates + b_gates).astype(jnp.float32)
    log_i = gates[..., :M_HEADS].transpose(0, 2, 1)
    log_f = jax.nn.log_sigmoid(gates[..., M_HEADS:]).transpose(0, 2, 1)

    def heads(t, d):
        return t.reshape(Bsz, T, M_HEADS, d).transpose(0, 2, 1, 3).astype(jnp.float32)

    qh = heads(q, M_QK_DIM) * (M_QK_DIM ** -0.5)
    kh = heads(k, M_QK_DIM)
    vh = heads(v, M_V_DIM)
    h, C, n, m = _mlstm_chunkwise(qh, kh, vh, log_i, log_f, C0.astype(jnp.float32),
                                  n0.astype(jnp.float32), m0.astype(jnp.float32))
    h = _rmsnorm(h.transpose(0, 2, 1, 3), norm_g.reshape(M_HEADS, M_V_DIM))
    h = h.reshape(Bsz, T, v_w) * jax.nn.sigmoid(o.astype(jnp.float32))
    return h.astype(x.dtype) @ w_out, C.astype(C0.dtype), n.astype(n0.dtype), m.astype(m0.dtype)


def _conv_ffn(x, prev, w_up, conv_w, conv_b, w_down):
    T = x.shape[1]
    u = x @ w_up
    full = jnp.concatenate([prev.astype(u.dtype), u], axis=1)
    c = conv_b
    for j in range(CONV_W):
        c = c + conv_w[j] * full[:, j:j + T]
    gate, val = jnp.split(c, 2, axis=-1)
    h = jax.nn.gelu(gate, approximate=True) * val
    return h @ w_down, full[:, T:]


def _trunk(x, cache_k, cache_v, state_C, state_n, state_m, state_conv,
           norm_mix_pre, norm_mix_post, norm_ffn_pre, norm_ffn_post,
           attn_w_qkv, attn_b_qkv, attn_w_o, attn_b_o, attn_sinks,
           mlstm_w_in, mlstm_b_gates, mlstm_norm, mlstm_w_out,
           ffn_w_up, ffn_conv_w, ffn_conv_b, ffn_w_down):
    bsz = x.shape[0]
    ks, vs, Cs, ns, ms, convs = [], [], [], [], [], []
    for i in range(DEPTH):
        j = i // N_MIXERS
        h = _rmsnorm(x, norm_mix_pre[i])
        if i % N_MIXERS == 0:
            if cache_k is None:
                h, kt, vt = _swa_prompt(h, attn_w_qkv[j], attn_b_qkv[j], attn_w_o[j], attn_b_o[j], attn_sinks[j])
            else:
                h, kt, vt = _swa_sample(h, cache_k[j], cache_v[j], attn_w_qkv[j], attn_b_qkv[j],
                                        attn_w_o[j], attn_b_o[j], attn_sinks[j])
            ks.append(kt)
            vs.append(vt)
        else:
            if state_C is None:
                C0 = jnp.zeros((bsz, M_HEADS, M_QK_DIM, M_V_DIM), jnp.float32)
                n0 = jnp.zeros((bsz, M_HEADS, M_QK_DIM), jnp.float32)
                m0 = jnp.zeros((bsz, M_HEADS), jnp.float32)
            else:
                C0, n0, m0 = state_C[j], state_n[j], state_m[j]
            h, C, n, m = _mlstm(h, C0, n0, m0, mlstm_w_in[j], mlstm_b_gates[j], mlstm_norm[j], mlstm_w_out[j])
            Cs.append(C)
            ns.append(n)
            ms.append(m)
        x = x + _rmsnorm(h, norm_mix_post[i])
        h = _rmsnorm(x, norm_ffn_pre[i])
        prev = jnp.zeros((bsz, CONV_W - 1, 2 * D_FF), x.dtype) if state_conv is None else state_conv[i]
        h, cbuf = _conv_ffn(h, prev, ffn_w_up[i], ffn_conv_w[i], ffn_conv_b[i], ffn_w_down[i])
        convs.append(cbuf)
        x = x + _rmsnorm(h, norm_ffn_post[i])
    return (x, jnp.stack(ks), jnp.stack(vs), jnp.stack(Cs), jnp.stack(ns), jnp.stack(ms), jnp.stack(convs))


def setup_inputs(seed: int = 0) -> dict:
    key = jax.random.key(seed)
    ks = jax.random.split(key, 28)

    def nrm(k, shape, scale):
        return jax.random.normal(k, shape, jnp.float32) * scale

    qkv_w = (N_HEADS + 2 * N_KV_HEADS) * HEAD_DIM
    m_in = 2 * M_HEADS * M_QK_DIM + 2 * M_HEADS * M_V_DIM + 2 * M_HEADS
    b_gates = jnp.concatenate([nrm(ks[20], (N_B_LAYERS, M_HEADS), 0.1),
                               F_BIAS + nrm(ks[21], (N_B_LAYERS, M_HEADS), 0.1)], axis=-1)
    return {
        'x_prompt': nrm(ks[0], (BATCH, SEQ, D_MODEL), 1.0),
        'x_sample': nrm(ks[1], (DEC_BATCH, DEC_SEQ, D_MODEL), 1.0),
        'cache_k': nrm(ks[2], (N_A_LAYERS, DEC_BATCH, WINDOW, N_KV_HEADS, HEAD_DIM), 1.0),
        'cache_v': nrm(ks[3], (N_A_LAYERS, DEC_BATCH, WINDOW, N_KV_HEADS, HEAD_DIM), 1.0),
        'state_C': nrm(ks[4], (N_B_LAYERS, DEC_BATCH, M_HEADS, M_QK_DIM, M_V_DIM), 0.1),
        'state_n': nrm(ks[5], (N_B_LAYERS, DEC_BATCH, M_HEADS, M_QK_DIM), 1.0),
        'state_m': nrm(ks[6], (N_B_LAYERS, DEC_BATCH, M_HEADS), 1.0),
        'state_conv': nrm(ks[7], (DEPTH, DEC_BATCH, CONV_W - 1, 2 * D_FF), 1.0),
        'norm_mix_pre': 1.0 + nrm(ks[8], (DEPTH, D_MODEL), 0.05),
        'norm_mix_post': 1.0 + nrm(ks[9], (DEPTH, D_MODEL), 0.05),
        'norm_ffn_pre': 1.0 + nrm(ks[10], (DEPTH, D_MODEL), 0.05),
        'norm_ffn_post': 1.0 + nrm(ks[11], (DEPTH, D_MODEL), 0.05),
        'attn_w_qkv': nrm(ks[12], (N_A_LAYERS, D_MODEL, qkv_w), D_MODEL ** -0.5),
        'attn_b_qkv': nrm(ks[13], (N_A_LAYERS, qkv_w), 0.02),
        'attn_w_o': nrm(ks[14], (N_A_LAYERS, N_HEADS * HEAD_DIM, D_MODEL), (N_HEADS * HEAD_DIM) ** -0.5),
        'attn_b_o': nrm(ks[15], (N_A_LAYERS, D_MODEL), 0.02),
        'attn_sinks': nrm(ks[16], (N_A_LAYERS, N_HEADS), 1.0),
        'mlstm_w_in': nrm(ks[17], (N_B_LAYERS, D_MODEL, m_in), D_MODEL ** -0.5),
        'mlstm_b_gates': b_gates,
        'mlstm_norm': 1.0 + nrm(ks[18], (N_B_LAYERS, M_HEADS * M_V_DIM), 0.05),
        'mlstm_w_out': nrm(ks[19], (N_B_LAYERS, M_HEADS * M_V_DIM, D_MODEL), (M_HEADS * M_V_DIM) ** -0.5),
        'ffn_w_up': nrm(ks[22], (DEPTH, D_MODEL, 2 * D_FF), D_MODEL ** -0.5),
        'ffn_conv_w': nrm(ks[23], (DEPTH, CONV_W, 2 * D_FF), CONV_W ** -0.5),
        'ffn_conv_b': nrm(ks[24], (DEPTH, 2 * D_FF), 0.01),
        'ffn_w_down': nrm(ks[25], (DEPTH, D_FF, D_MODEL), D_FF ** -0.5),
    }


def reference(x_prompt, x_sample, cache_k, cache_v, state_C, state_n, state_m, state_conv,
              norm_mix_pre, norm_mix_post, norm_ffn_pre, norm_ffn_post,
              attn_w_qkv, attn_b_qkv, attn_w_o, attn_b_o, attn_sinks,
              mlstm_w_in, mlstm_b_gates, mlstm_norm, mlstm_w_out,
              ffn_w_up, ffn_conv_w, ffn_conv_b, ffn_w_down):
    y_prompt, k_p, v_p, C_p, n_p, m_p, conv_p = _trunk(
        x_prompt, None, None, None, None, None, None,
        norm_mix_pre, norm_mix_post, norm_ffn_pre, norm_ffn_post,
        attn_w_qkv, attn_b_qkv, attn_w_o, attn_b_o, attn_sinks,
        mlstm_w_in, mlstm_b_gates, mlstm_norm, mlstm_w_out,
        ffn_w_up, ffn_conv_w, ffn_conv_b, ffn_w_down)
    y_sample, k_s, v_s, C_s, n_s, m_s, conv_s = _trunk(
        x_sample, cache_k, cache_v, state_C, state_n, state_m, state_conv,
        norm_mix_pre, norm_mix_post, norm_ffn_pre, norm_ffn_post,
        attn_w_qkv, attn_b_qkv, attn_w_o, attn_b_o, attn_sinks,
        mlstm_w_in, mlstm_b_gates, mlstm_norm, mlstm_w_out,
        ffn_w_up, ffn_conv_w, ffn_conv_b, ffn_w_down)
    return (y_prompt, y_sample, k_p, v_p, k_s, v_s, C_p, n_p, m_p, C_s, n_s, m_s, conv_p, conv_s)
```

```python
import functools
import math

import jax
import jax.numpy as jnp
from jax import lax
from jax.experimental import pallas as pl
from jax.experimental.pallas import tpu as pltpu

F32 = jnp.float32
BF16 = jnp.bfloat16

D_MODEL = 2048
WINDOW = 128
HEAD_DIM = 64
N_HEADS = 32
N_KV_HEADS = 4
GROUP = N_HEADS // N_KV_HEADS
Q_W = N_HEADS * HEAD_DIM
KV_W = N_KV_HEADS * HEAD_DIM
ROPE_THETA = 10000.0
M_HEADS = 4
M_QK_DIM = 256
M_V_DIM = 512
QK_W = M_HEADS * M_QK_DIM
V_W = M_HEADS * M_V_DIM
GATE_PAD = 128
PROJ_W = 2 * QK_W + 2 * V_W + GATE_PAD
D_FF = 4 * D_MODEL
CONV_W = 3
EPS = 1e-6
NEG_INF = -1e30
PAST_LEN = 16384

SUBLANES = 8
LANES = 128
VMEM_LIMIT_BYTES = 56 * 1024 * 1024

ROW_TILE = 512
FF_TILE = 512
PROJ_N_TILE = 896
MLSTM_CHUNK = 128


def _params(*sem):
    return pltpu.CompilerParams(dimension_semantics=sem, vmem_limit_bytes=VMEM_LIMIT_BYTES)


def _rmsnorm(xf, g):
    r = xf * lax.rsqrt(jnp.mean(xf * xf, axis=-1, keepdims=True) + EPS)
    return r * g


def _qkv_rope_kernel(x_ref, g_ref, w_ref, b_ref, cos_ref, sin_ref, q_ref, kv_ref):
    xn = _rmsnorm(x_ref[...], g_ref[...]).astype(BF16)
    y = jnp.dot(xn, w_ref[...], preferred_element_type=F32) + b_ref[...]
    cos = cos_ref[...]
    sin = sin_ref[...]
    lane = lax.broadcasted_iota(jnp.int32, cos.shape, 1)
    first_half = (lane & (HEAD_DIM - 1)) < (HEAD_DIM // 2)
    n_rot = (Q_W + KV_W) // LANES
    for c in range(n_rot):
        blk = y[:, c * LANES:(c + 1) * LANES]
        sw = jnp.where(first_half, pltpu.roll(blk, LANES - HEAD_DIM // 2, 1),
                       pltpu.roll(blk, HEAD_DIM // 2, 1))
        r = blk * cos + sw * sin
        if c < Q_W // LANES:
            q_ref[:, c * LANES:(c + 1) * LANES] = r.astype(q_ref.dtype)
        else:
            o = c * LANES - Q_W
            kv_ref[:, o:o + LANES] = r
    kv_ref[:, KV_W:2 * KV_W] = y[:, Q_W + KV_W:Q_W + 2 * KV_W]


def _qkv_rope(x, g, w, b, cos, sin, tm, q_dtype):
    m = x.shape[0]
    n = w.shape[1]
    n_pos_tiles = cos.shape[0] // tm
    return pl.pallas_call(
        _qkv_rope_kernel,
        grid=(m // tm,),
        in_specs=[
            pl.BlockSpec((tm, D_MODEL), lambda i: (i, 0)),
            pl.BlockSpec((1, D_MODEL), lambda i: (0, 0)),
            pl.BlockSpec((D_MODEL, n), lambda i: (0, 0)),
            pl.BlockSpec((1, n), lambda i: (0, 0)),
            pl.BlockSpec((tm, LANES), lambda i: (i % n_pos_tiles, 0)),
            pl.BlockSpec((tm, LANES), lambda i: (i % n_pos_tiles, 0)),
        ],
        out_specs=[
            pl.BlockSpec((tm, Q_W), lambda i: (i, 0)),
            pl.BlockSpec((tm, 2 * KV_W), lambda i: (i, 0)),
        ],
        out_shape=[
            jax.ShapeDtypeStruct((m, Q_W), q_dtype),
            jax.ShapeDtypeStruct((m, 2 * KV_W), F32),
        ],
        compiler_params=_params("arbitrary"),
        name="qkv_rope",
    )(x, g, w, b, cos, sin)


def _attn_prompt_kernel(sinks_ref, q_ref, kvp_ref, kvc_ref, o_ref):
    n = pl.program_id(1)
    w = WINDOW
    ri = lax.broadcasted_iota(jnp.int32, (w, 2 * w), 0)
    ci = lax.broadcasted_iota(jnp.int32, (w, 2 * w), 1)
    has_prev = jnp.full((w, 2 * w), n, jnp.int32) > 0
    allowed = ((ci < w) & (ci > ri) & has_prev) | ((ci >= w) & ((ci - w) <= ri))
    for g in range(N_KV_HEADS):
        ks = slice(g * HEAD_DIM, (g + 1) * HEAD_DIM)
        vs = slice(KV_W + g * HEAD_DIM, KV_W + (g + 1) * HEAD_DIM)
        k = jnp.concatenate([kvp_ref[:, ks], kvc_ref[:, ks]], axis=0).astype(BF16)
        v = jnp.concatenate([kvp_ref[:, vs], kvc_ref[:, vs]], axis=0).astype(BF16)
        q = jnp.concatenate(
            [q_ref[:, (g * GROUP + h) * HEAD_DIM:(g * GROUP + h + 1) * HEAD_DIM] for h in range(GROUP)],
            axis=0)
        s = lax.dot_general(q, k, (((1,), (1,)), ((), ())), preferred_element_type=F32)
        s = s * (HEAD_DIM ** -0.5)
        ps = []
        for h in range(GROUP):
            sh = jnp.where(allowed, s[h * w:(h + 1) * w], NEG_INF)
            sink = sinks_ref[g * GROUP + h]
            m = jnp.maximum(jnp.max(sh, axis=-1, keepdims=True), sink)
            p = jnp.exp(sh - m)
            l = jnp.sum(p, axis=-1, keepdims=True) + jnp.exp(sink - m)
            ps.append((p / l).astype(BF16))
        o = jnp.dot(jnp.concatenate(ps, axis=0), v, preferred_element_type=F32)
        for h in range(GROUP):
            c0 = (g * GROUP + h) * HEAD_DIM
            o_ref[:, c0:c0 + HEAD_DIM] = o[h * w:(h + 1) * w].astype(o_ref.dtype)


def _attn_prompt(sinks, q, kv, batch, seq):
    nb = seq // WINDOW
    return pl.pallas_call(
        _attn_prompt_kernel,
        grid=(batch, nb),
        in_specs=[
            pl.BlockSpec(memory_space=pltpu.SMEM),
            pl.BlockSpec((WINDOW, Q_W), lambda b, n: (b * nb + n, 0)),
            pl.BlockSpec((WINDOW, 2 * KV_W), lambda b, n: (b * nb + jnp.maximum(n - 1, 0), 0)),
            pl.BlockSpec((WINDOW, 2 * KV_W), lambda b, n: (b * nb + n, 0)),
        ],
        out_specs=pl.BlockSpec((WINDOW, Q_W), lambda b, n: (b * nb + n, 0)),
        out_shape=jax.ShapeDtypeStruct((batch * seq, Q_W), BF16),
        compiler_params=_params("arbitrary", "arbitrary"),
        name="attn_prompt",
    )(sinks, q, kv, kv)


def _attn_sample_kernel(sinks_ref, q_ref, kvn_ref, ck_ref, cv_ref, o_ref, ko_ref, vo_ref):
    t = q_ref.shape[0]
    wc = ck_ref.shape[1]
    rows = GROUP * t
    ri = lax.broadcasted_iota(jnp.int32, (rows, wc), 0) & (t - 1)
    ci = lax.broadcasted_iota(jnp.int32, (rows, wc), 1)
    allowed_c = ci > ri
    ri_n = lax.broadcasted_iota(jnp.int32, (rows, t), 0) & (t - 1)
    ci_n = lax.broadcasted_iota(jnp.int32, (rows, t), 1)
    allowed_n = ci_n <= ri_n
    scale = HEAD_DIM ** -0.5
    for g in range(N_KV_HEADS):
        ks = slice(g * HEAD_DIM, (g + 1) * HEAD_DIM)
        vs = slice(KV_W + g * HEAD_DIM, KV_W + (g + 1) * HEAD_DIM)
        kc = ck_ref[0, :, ks].astype(BF16)
        vc = cv_ref[0, :, ks].astype(BF16)
        kn = kvn_ref[:, ks]
        vn = kvn_ref[:, vs]
        q = jnp.concatenate(
            [q_ref[:, (g * GROUP + h) * HEAD_DIM:(g * GROUP + h + 1) * HEAD_DIM] for h in range(GROUP)],
            axis=0)
        s_c = lax.dot_general(q.astype(BF16), kc, (((1,), (1,)), ((), ())), preferred_element_type=F32) * scale
        s_n = lax.dot_general(q, kn, (((1,), (1,)), ((), ())), preferred_element_type=F32) * scale
        s_c = jnp.where(allowed_c, s_c, NEG_INF)
        s_n = jnp.where(allowed_n, s_n, NEG_INF)
        sink = jnp.concatenate([jnp.full((t, 1), sinks_ref[g * GROUP + h], F32) for h in range(GROUP)], axis=0)
        m = jnp.maximum(jnp.maximum(jnp.max(s_c, axis=-1, keepdims=True),
                                    jnp.max(s_n, axis=-1, keepdims=True)), sink)
        p_c = jnp.exp(s_c - m)
        p_n = jnp.exp(s_n - m)
        l = (jnp.sum(p_c, axis=-1, keepdims=True) + jnp.sum(p_n, axis=-1, keepdims=True)
             + jnp.exp(sink - m))
        o = (jnp.dot((p_c / l).astype(BF16), vc, preferred_element_type=F32)
             + jnp.dot(p_n / l, vn, preferred_element_type=F32))
        for h in range(GROUP):
            c0 = (g * GROUP + h) * HEAD_DIM
            o_ref[:, c0:c0 + HEAD_DIM] = o[h * t:(h + 1) * t]
    ko_ref[0, 0:wc - t, :] = ck_ref[0, t:wc, :]
    ko_ref[0, wc - t:wc, :] = kvn_ref[:, 0:KV_W]
    vo_ref[0, 0:wc - t, :] = cv_ref[0, t:wc, :]
    vo_ref[0, wc - t:wc, :] = kvn_ref[:, KV_W:2 * KV_W]


def _attn_sample(sinks, q, kvn, ck, cv, t):
    db = ck.shape[0]
    wc = ck.shape[1]
    return pl.pallas_call(
        _attn_sample_kernel,
        grid=(db,),
        in_specs=[
            pl.BlockSpec(memory_space=pltpu.SMEM),
            pl.BlockSpec((t, Q_W), lambda b: (b, 0)),
            pl.BlockSpec((t, 2 * KV_W), lambda b: (b, 0)),
            pl.BlockSpec((1, wc, KV_W), lambda b: (b, 0, 0)),
            pl.BlockSpec((1, wc, KV_W), lambda b: (b, 0, 0)),
        ],
        out_specs=[
            pl.BlockSpec((t, Q_W), lambda b: (b, 0)),
            pl.BlockSpec((1, wc, KV_W), lambda b: (b, 0, 0)),
            pl.BlockSpec((1, wc, KV_W), lambda b: (b, 0, 0)),
        ],
        out_shape=[
            jax.ShapeDtypeStruct((db * t, Q_W), F32),
            jax.ShapeDtypeStruct((db, wc, KV_W), F32),
            jax.ShapeDtypeStruct((db, wc, KV_W), F32),
        ],
        compiler_params=_params("arbitrary"),
        name="attn_sample",
    )(sinks, q, kvn, ck, cv)


def _proj_post_kernel(has_bias, a_ref, w_ref, b_ref, x_ref, g_ref, o_ref):
    y = jnp.dot(a_ref[...].astype(BF16), w_ref[...], preferred_element_type=F32)
    if has_bias:
        y = y + b_ref[...]
    o_ref[...] = x_ref[...] + _rmsnorm(y, g_ref[...])


def _proj_post(a, w, b, x, g, tm):
    m, k = a.shape
    has_bias = b is not None
    if b is None:
        b = jnp.zeros((1, D_MODEL), F32)
    return pl.pallas_call(
        functools.partial(_proj_post_kernel, has_bias),
        grid=(m // tm,),
        in_specs=[
            pl.BlockSpec((tm, k), lambda i: (i, 0)),
            pl.BlockSpec((k, D_MODEL), lambda i: (0, 0)),
            pl.BlockSpec((1, D_MODEL), lambda i: (0, 0)),
            pl.BlockSpec((tm, D_MODEL), lambda i: (i, 0)),
            pl.BlockSpec((1, D_MODEL), lambda i: (0, 0)),
        ],
        out_specs=pl.BlockSpec((tm, D_MODEL), lambda i: (i, 0)),
        out_shape=jax.ShapeDtypeStruct((m, D_MODEL), F32),
        compiler_params=_params("arbitrary"),
        name="proj_post",
    )(a, w, b, x, g)


def _norm_matmul_kernel(x_ref, g_ref, w_ref, b_ref, o_ref, xn_ref):
    @pl.when(pl.program_id(1) == 0)
    def _():
        xn_ref[...] = _rmsnorm(x_ref[...], g_ref[...]).astype(BF16)

    y = jnp.dot(xn_ref[...], w_ref[...], preferred_element_type=F32) + b_ref[...]
    o_ref[...] = y.astype(o_ref.dtype)


def _norm_matmul(x, g, w, b, tm, tn):
    m = x.shape[0]
    n = w.shape[1]
    return pl.pallas_call(
        _norm_matmul_kernel,
        grid=(m // tm, n // tn),
        in_specs=[
            pl.BlockSpec((tm, D_MODEL), lambda i, j: (i, 0)),
            pl.BlockSpec((1, D_MODEL), lambda i, j: (0, 0)),
            pl.BlockSpec((D_MODEL, tn), lambda i, j: (0, j)),
            pl.BlockSpec((1, tn), lambda i, j: (0, j)),
        ],
        out_specs=pl.BlockSpec((tm, tn), lambda i, j: (i, j)),
        out_shape=jax.ShapeDtypeStruct((m, n), F32),
        scratch_shapes=[pltpu.VMEM((tm, D_MODEL), BF16)],
        compiler_params=_params("arbitrary", "arbitrary"),
        name="norm_matmul",
    )(x, g, w, b)


def _log_sigmoid(x):
    return jnp.minimum(x, 0.0) - jnp.log1p(jnp.exp(-jnp.abs(x)))


def _mlstm_chunk(q, k, v, li, lf, c_state, n_state, m_state, mxu_dtype):
    L = q.shape[0]
    ri = lax.broadcasted_iota(jnp.int32, (L, L), 0)
    ci = lax.broadcasted_iota(jnp.int32, (L, L), 1)
    eye = ri == ci
    tril = ci <= ri
    lf_row = jnp.sum(jnp.where(eye, lf, 0.0), axis=0, keepdims=True)
    li_row = jnp.sum(jnp.where(eye, li, 0.0), axis=0, keepdims=True)
    b_col = jnp.sum(jnp.where(tril, lf_row, 0.0), axis=1, keepdims=True)
    b_row = jnp.sum(jnp.where(ri <= ci, lf, 0.0), axis=0, keepdims=True)
    dmat = jnp.where(tril, b_col - b_row + li_row, -jnp.inf)
    inter = b_col + m_state
    m_row = jnp.maximum(inter, jnp.max(dmat, axis=1, keepdims=True))
    qm = q.astype(mxu_dtype)
    km = k.astype(mxu_dtype)
    vm = v.astype(mxu_dtype)
    s = lax.dot_general(qm, km, (((1,), (1,)), ((), ())), preferred_element_type=F32)
    sm = s * jnp.exp(dmat - m_row)
    w_inter = jnp.exp(inter - m_row)
    num = (jnp.dot(sm.astype(mxu_dtype), vm, preferred_element_type=F32)
           + w_inter * jnp.dot(qm, c_state.astype(mxu_dtype), preferred_element_type=F32))
    den = jnp.sum(sm, axis=1, keepdims=True) + w_inter * jnp.sum(q * n_state, axis=1, keepdims=True)
    h = num / jnp.maximum(jnp.abs(den), jnp.exp(-m_row))
    b_last = b_col[L - 1:L, :]
    gk = b_last - b_col + li
    m_new = jnp.maximum(b_last + m_state, jnp.max(gk, axis=0, keepdims=True))
    wk = jnp.exp(gk - m_new)
    decay = jnp.exp(b_last + m_state - m_new)
    kw = k * wk
    c_new = decay * c_state + lax.dot_general(kw.astype(mxu_dtype), vm, (((0,), (0,)), ((), ())),
                                              preferred_element_type=F32)
    n_new = decay * n_state + jnp.sum(kw, axis=0, keepdims=True)
    return h, c_new, n_new, m_new


def _mlstm_head(cols, hd, ng, c_state, n_state, m_state, mxu_dtype):
    q = cols(hd * M_QK_DIM, (hd + 1) * M_QK_DIM) * (M_QK_DIM ** -0.5)
    k = cols(QK_W + hd * M_QK_DIM, QK_W + (hd + 1) * M_QK_DIM)
    v = cols(2 * QK_W + hd * M_V_DIM, 2 * QK_W + (hd + 1) * M_V_DIM)
    o = cols(2 * QK_W + V_W + hd * M_V_DIM, 2 * QK_W + V_W + (hd + 1) * M_V_DIM)
    g0 = 2 * QK_W + 2 * V_W
    li = cols(g0 + hd, g0 + hd + 1)
    lf = _log_sigmoid(cols(g0 + M_HEADS + hd, g0 + M_HEADS + hd + 1))
    h, c_new, n_new, m_new = _mlstm_chunk(q, k, v, li, lf, c_state, n_state, m_state, mxu_dtype)
    hn = _rmsnorm(h, ng) * jax.nn.sigmoid(o)
    return hn, c_new, n_new, m_new


def _mlstm_prompt_kernel(batch, proj_ref, ng_ref, h_ref, c_ref, n_ref, m_ref):
    @pl.when(pl.program_id(0) == 0)
    def _():
        c_ref[...] = jnp.zeros(c_ref.shape, F32)
        n_ref[...] = jnp.zeros(n_ref.shape, F32)
        m_ref[...] = jnp.zeros(m_ref.shape, F32)

    for b in range(batch):
        for hd in range(M_HEADS):
            r = b * M_HEADS + hd
            hn, c_new, n_new, m_new = _mlstm_head(
                lambda lo, hi, b=b: proj_ref[b, :, lo:hi], hd, ng_ref[:, hd * M_V_DIM:(hd + 1) * M_V_DIM],
                c_ref[r], n_ref[r:r + 1, :], m_ref[r:r + 1, 0:1], BF16)
            h_ref[b, :, hd * M_V_DIM:(hd + 1) * M_V_DIM] = hn.astype(h_ref.dtype)
            c_ref[r] = c_new
            n_ref[r:r + 1, :] = n_new
            m_ref[r:r + 1, :] = jnp.broadcast_to(m_new, (1, LANES))


def _mlstm_prompt(proj, ng, batch, seq):
    chunk = MLSTM_CHUNK
    rows = batch * M_HEADS
    return pl.pallas_call(
        functools.partial(_mlstm_prompt_kernel, batch),
        grid=(seq // chunk,),
        in_specs=[
            pl.BlockSpec((batch, chunk, PROJ_W), lambda c: (0, c, 0)),
            pl.BlockSpec((1, V_W), lambda c: (0, 0)),
        ],
        out_specs=[
            pl.BlockSpec((batch, chunk, V_W), lambda c: (0, c, 0)),
            pl.BlockSpec((rows, M_QK_DIM, M_V_DIM), lambda c: (0, 0, 0)),
            pl.BlockSpec((rows, M_QK_DIM), lambda c: (0, 0)),
            pl.BlockSpec((rows, LANES), lambda c: (0, 0)),
        ],
        out_shape=[
            jax.ShapeDtypeStruct((batch, seq, V_W), BF16),
            jax.ShapeDtypeStruct((rows, M_QK_DIM, M_V_DIM), F32),
            jax.ShapeDtypeStruct((rows, M_QK_DIM), F32),
            jax.ShapeDtypeStruct((rows, LANES), F32),
        ],
        compiler_params=_params("arbitrary"),
        name="mlstm_prompt",
    )(proj.reshape(batch, seq, PROJ_W), ng)


def _mlstm_sample_kernel(proj_ref, ng_ref, c0_ref, n0_ref, m0_ref, h_ref, c_ref, n_ref, m_ref):
    for hd in range(M_HEADS):
        hn, c_new, n_new, m_new = _mlstm_head(
            lambda lo, hi: proj_ref[:, lo:hi], hd, ng_ref[:, hd * M_V_DIM:(hd + 1) * M_V_DIM],
            c0_ref[0, hd], n0_ref[0, hd:hd + 1, :], m0_ref[0, :, hd:hd + 1], F32)
        h_ref[:, hd * M_V_DIM:(hd + 1) * M_V_DIM] = hn
        c_ref[0, hd] = c_new
        n_ref[0, hd:hd + 1, :] = n_new
        m_ref[0, :, hd:hd + 1] = m_new


def _mlstm_sample(proj, ng, c0, n0, m0, t):
    db = c0.shape[0]
    m0 = m0.reshape(db, 1, M_HEADS)
    state_specs = [
        pl.BlockSpec((1, M_HEADS, M_QK_DIM, M_V_DIM), lambda b: (b, 0, 0, 0)),
        pl.BlockSpec((1, M_HEADS, M_QK_DIM), lambda b: (b, 0, 0)),
        pl.BlockSpec((1, 1, M_HEADS), lambda b: (b, 0, 0)),
    ]
    return pl.pallas_call(
        _mlstm_sample_kernel,
        grid=(db,),
        in_specs=[
            pl.BlockSpec((t, PROJ_W), lambda b: (b, 0)),
            pl.BlockSpec((1, V_W), lambda b: (0, 0)),
        ] + state_specs,
        out_specs=[pl.BlockSpec((t, V_W), lambda b: (b, 0))] + state_specs,
        out_shape=[
            jax.ShapeDtypeStruct((db * t, V_W), F32),
            jax.ShapeDtypeStruct(c0.shape, F32),
            jax.ShapeDtypeStruct(n0.shape, F32),
            jax.ShapeDtypeStruct(m0.shape, F32),
        ],
        compiler_params=_params("arbitrary"),
        name="mlstm_sample",
    )(proj, ng, c0, n0, m0)


def _conv3(u, u1, u2, w, b):
    return ((b + w[0:1] * u2) + w[1:2] * u1) + w[2:3] * u


def _conv_stream(u, tail, w, b):
    u1 = pltpu.roll(u, 1, 0)
    u2 = pltpu.roll(u, 2, 0)
    c = _conv3(u, u1, u2, w, b)
    uf = u[0:SUBLANES]
    row = lax.broadcasted_iota(jnp.int32, uf.shape, 0)
    uf1 = jnp.where(row < 1, pltpu.roll(tail, 1, 0), pltpu.roll(uf, 1, 0))
    uf2 = jnp.where(row < 2, pltpu.roll(tail, 2, 0), pltpu.roll(uf, 2, 0))
    cf = _conv3(uf, uf1, uf2, w, b)
    return jnp.concatenate([cf, c[SUBLANES:]], axis=0)


def _conv_seq8(u, prev, w, b):
    nseq = u.shape[0] // SUBLANES
    p0 = jnp.broadcast_to(prev[:, 0:1, :], (nseq, SUBLANES, u.shape[1])).reshape(u.shape)
    p1 = jnp.broadcast_to(prev[:, 1:2, :], (nseq, SUBLANES, u.shape[1])).reshape(u.shape)
    row = lax.broadcasted_iota(jnp.int32, u.shape, 0) & (SUBLANES - 1)
    u1 = jnp.where(row == 0, p1, pltpu.roll(u, 1, 0))
    u2 = jnp.where(row == 0, p0, jnp.where(row == 1, p1, pltpu.roll(u, 2, 0)))
    return _conv3(u, u1, u2, w, b)


def _ffn_kernel(stream, tiles_per_seq, x_ref, gpre_ref, wg_ref, wv_ref, cwg_ref, cwv_ref, cbg_ref, cbv_ref,
                wd_ref, gpost_ref, *rest):
    if stream:
        o_ref, sg_ref, sv_ref, xn_ref, acc_ref, tail_g, tail_v = rest
    else:
        pg_ref, pv_ref, o_ref, sg_ref, sv_ref, xn_ref, acc_ref = rest
    i = pl.program_id(0)
    j = pl.program_id(1)
    tm = x_ref.shape[0]

    @pl.when(j == 0)
    def _():
        xn_ref[...] = _rmsnorm(x_ref[...], gpre_ref[...]).astype(BF16)

    xn = xn_ref[...]
    ug = jnp.dot(xn, wg_ref[...], preferred_element_type=F32)
    uv = jnp.dot(xn, wv_ref[...], preferred_element_type=F32)
    if stream:
        @pl.when(i % tiles_per_seq == 0)
        def _():
            tail_g[j] = jnp.zeros(tail_g.shape[1:], F32)
            tail_v[j] = jnp.zeros(tail_v.shape[1:], F32)

        cg = _conv_stream(ug, tail_g[j], cwg_ref[...], cbg_ref[...])
        cv = _conv_stream(uv, tail_v[j], cwv_ref[...], cbv_ref[...])
        tail_g[j] = ug[tm - SUBLANES:]
        tail_v[j] = uv[tm - SUBLANES:]
        sg_ref[0] = ug[tm - SUBLANES:]
        sv_ref[0] = uv[tm - SUBLANES:]
    else:
        cg = _conv_seq8(ug, pg_ref[...], cwg_ref[...], cbg_ref[...])
        cv = _conv_seq8(uv, pv_ref[...], cwv_ref[...], cbv_ref[...])
        sg_ref[...] = ug.reshape(sg_ref.shape)
        sv_ref[...] = uv.reshape(sv_ref.shape)
    h = (jax.nn.gelu(cg, approximate=True) * cv).astype(BF16)
    part = jnp.dot(h, wd_ref[...], preferred_element_type=F32)

    @pl.when(j == 0)
    def _():
        acc_ref[...] = part

    @pl.when(j > 0)
    def _():
        acc_ref[...] += part

    @pl.when(j == pl.num_programs(1) - 1)
    def _():
        o_ref[...] = x_ref[...] + _rmsnorm(acc_ref[...], gpost_ref[...])


def _ffn(x, prev, gpre, w_up, conv_w, conv_b, w_down, gpost, tm, tf, seq):
    m = x.shape[0]
    nf = D_FF // tf
    stream = prev is None
    in_specs = [
        pl.BlockSpec((tm, D_MODEL), lambda i, j: (i, 0)),
        pl.BlockSpec((1, D_MODEL), lambda i, j: (0, 0)),
        pl.BlockSpec((D_MODEL, tf), lambda i, j: (0, j)),
        pl.BlockSpec((D_MODEL, tf), lambda i, j: (0, j + nf)),
        pl.BlockSpec((CONV_W, tf), lambda i, j: (0, j)),
        pl.BlockSpec((CONV_W, tf), lambda i, j: (0, j + nf)),
        pl.BlockSpec((1, tf), lambda i, j: (0, j)),
        pl.BlockSpec((1, tf), lambda i, j: (0, j + nf)),
        pl.BlockSpec((tf, D_MODEL), lambda i, j: (j, 0)),
        pl.BlockSpec((1, D_MODEL), lambda i, j: (0, 0)),
    ]
    args = [x, gpre, w_up, w_up, conv_w, conv_w, conv_b, conv_b, w_down, gpost]
    scratch = [pltpu.VMEM((tm, D_MODEL), BF16), pltpu.VMEM((tm, D_MODEL), F32)]
    if stream:
        tiles_per_seq = seq // tm
        nseq = m // tm
        state_spec = pl.BlockSpec((1, SUBLANES, tf), lambda i, j: (i, 0, j))
        scratch += [pltpu.VMEM((nf, SUBLANES, tf), F32), pltpu.VMEM((nf, SUBLANES, tf), F32)]
    else:
        assert seq == SUBLANES and m == tm
        tiles_per_seq = 1
        nseq = m // seq
        state_spec = pl.BlockSpec((nseq, SUBLANES, tf), lambda i, j: (0, 0, j))
        in_specs += [
            pl.BlockSpec((nseq, CONV_W - 1, tf), lambda i, j: (0, 0, j)),
            pl.BlockSpec((nseq, CONV_W - 1, tf), lambda i, j: (0, 0, j + nf)),
        ]
        args += [prev, prev]
    return pl.pallas_call(
        functools.partial(_ffn_kernel, stream, tiles_per_seq),
        grid=(m // tm, nf),
        in_specs=in_specs,
        out_specs=[pl.BlockSpec((tm, D_MODEL), lambda i, j: (i, 0)), state_spec, state_spec],
        out_shape=[
            jax.ShapeDtypeStruct((m, D_MODEL), F32),
            jax.ShapeDtypeStruct((nseq, SUBLANES, D_FF), F32),
            jax.ShapeDtypeStruct((nseq, SUBLANES, D_FF), F32),
        ],
        scratch_shapes=scratch,
        compiler_params=_params("arbitrary", "arbitrary"),
        name="conv_ffn",
    )(*args)


def _rope_tables(pos):
    half = HEAD_DIM // 2
    inv = ROPE_THETA ** (-jnp.arange(half, dtype=F32) / half)
    ang = pos.astype(F32)[:, None] * inv[None, :]
    cos = jnp.cos(ang)
    sin = jnp.sin(ang)
    reps = LANES // HEAD_DIM
    return (jnp.tile(cos, (1, 2 * reps)), jnp.tile(jnp.concatenate([-sin, sin], axis=1), (1, reps)))


def _conv_state(sg, sv, blocks_per_seq):
    last = slice(blocks_per_seq - 1, None, blocks_per_seq)
    keep = slice(SUBLANES - (CONV_W - 1), SUBLANES)
    return jnp.concatenate([sg[last, keep], sv[last, keep]], axis=-1)


def _trunk(x, pos, seq, tm, cache, state, state_conv, p):
    m = x.shape[0]
    nseq = m // seq
    sample = cache is not None
    row = lambda v: v.reshape(1, -1)

    cos, sin = _rope_tables(pos)
    q, kv = _qkv_rope(x, row(p["norm_mix_pre"][0]), p["w_qkv"], row(p["attn_b_qkv"][0]), cos, sin, tm,
                      F32 if sample else BF16)
    if sample:
        ck, cv = cache
        o, k_new, v_new = _attn_sample(p["attn_sinks"][0], q, kv, ck.reshape(nseq, WINDOW, KV_W),
                                       cv.reshape(nseq, WINDOW, KV_W), seq)
    else:
        o = _attn_prompt(p["attn_sinks"][0], q, kv, nseq, seq)
        kv3 = kv.reshape(nseq, seq, 2 * KV_W)
        k_new = kv3[:, seq - WINDOW:, :KV_W]
        v_new = kv3[:, seq - WINDOW:, KV_W:]
    k_new = k_new.reshape(1, nseq, WINDOW, N_KV_HEADS, HEAD_DIM)
    v_new = v_new.reshape(1, nseq, WINDOW, N_KV_HEADS, HEAD_DIM)
    x = _proj_post(o, p["w_o"], row(p["attn_b_o"][0]), x, row(p["norm_mix_post"][0]), tm)
    x, sg0, sv0 = _ffn(x, state_conv[0] if sample else None, row(p["norm_ffn_pre"][0]), p["w_up"][0],
                       p["ffn_conv_w"][0], row(p["ffn_conv_b"][0]), p["w_down"][0], row(p["norm_ffn_post"][0]),
                       tm, FF_TILE, seq)

    proj = _norm_matmul(x, row(p["norm_mix_pre"][1]), p["w_in"], p["b_in"], tm, PROJ_N_TILE)
    ng = row(p["mlstm_norm"][0])
    if sample:
        c0, n0, m0 = state
        h, c_new, n_new, m_new = _mlstm_sample(proj, ng, c0, n0, m0, seq)
    else:
        h, c_new, n_new, m_new = _mlstm_prompt(proj, ng, nseq, seq)
        h = h.reshape(m, V_W)
        m_new = m_new[:, 0]
    c_new = c_new.reshape(1, nseq, M_HEADS, M_QK_DIM, M_V_DIM)
    n_new = n_new.reshape(1, nseq, M_HEADS, M_QK_DIM)
    m_new = m_new.reshape(1, nseq, M_HEADS)
    x = _proj_post(h, p["w_out"], None, x, row(p["norm_mix_post"][1]), tm)
    x, sg1, sv1 = _ffn(x, state_conv[1] if sample else None, row(p["norm_ffn_pre"][1]), p["w_up"][1],
                       p["ffn_conv_w"][1], row(p["ffn_conv_b"][1]), p["w_down"][1], row(p["norm_ffn_post"][1]),
                       tm, FF_TILE, seq)
    bps = 1 if sample else seq // tm
    conv = jnp.stack([_conv_state(sg0, sv0, bps), _conv_state(sg1, sv1, bps)])
    return x, k_new, v_new, c_new, n_new, m_new, conv


def kernel(x_prompt, x_sample, cache_k, cache_v, state_C, state_n, state_m, state_conv, norm_mix_pre,
           norm_mix_post, norm_ffn_pre, norm_ffn_post, attn_w_qkv, attn_b_qkv, attn_w_o, attn_b_o, attn_sinks,
           mlstm_w_in, mlstm_b_gates, mlstm_norm, mlstm_w_out, ffn_w_up, ffn_conv_w, ffn_conv_b, ffn_w_down):
    batch, seq, _ = x_prompt.shape
    dec_batch, dec_seq, _ = x_sample.shape
    n_gates = 2 * M_HEADS
    g0 = 2 * QK_W + 2 * V_W
    w_in = mlstm_w_in[0]
    p = dict(
        norm_mix_pre=norm_mix_pre, norm_mix_post=norm_mix_post, norm_ffn_pre=norm_ffn_pre,
        norm_ffn_post=norm_ffn_post, attn_b_qkv=attn_b_qkv, attn_b_o=attn_b_o, attn_sinks=attn_sinks,
        mlstm_norm=mlstm_norm, ffn_conv_w=ffn_conv_w, ffn_conv_b=ffn_conv_b,
        w_qkv=attn_w_qkv[0].astype(BF16),
        w_o=attn_w_o[0].astype(BF16),
        w_in=jnp.pad(w_in, ((0, 0), (0, GATE_PAD - n_gates))).astype(BF16),
        b_in=jnp.pad(mlstm_b_gates[0], (g0, GATE_PAD - n_gates)).reshape(1, PROJ_W),
        w_out=mlstm_w_out[0].astype(BF16),
        w_up=ffn_w_up.astype(BF16),
        w_down=ffn_w_down.astype(BF16),
    )
    yp, k_p, v_p, c_p, n_p, m_p, conv_p = _trunk(
        x_prompt.reshape(batch * seq, D_MODEL), jnp.arange(seq), seq, ROW_TILE, None, None, None, p)
    ys, k_s, v_s, c_s, n_s, m_s, conv_s = _trunk(
        x_sample.reshape(dec_batch * dec_seq, D_MODEL),
        jnp.tile(PAST_LEN + jnp.arange(dec_seq), dec_batch), dec_seq, dec_batch * dec_seq,
        (cache_k[0], cache_v[0]), (state_C[0], state_n[0], state_m[0]), state_conv, p)
    return (yp.reshape(batch, seq, D_MODEL), ys.reshape(dec_batch, dec_seq, D_MODEL),
            k_p, v_p, k_s, v_s, c_p, n_p, m_p, c_s, n_s, m_s, conv_p, conv_s)
```

```python
import functools
import math

import jax
import jax.numpy as jnp
from jax import lax
from jax.experimental import pallas as pl
from jax.experimental.pallas import tpu as pltpu

F32 = jnp.float32
BF16 = jnp.bfloat16

D_MODEL = 2048
WINDOW = 128
HEAD_DIM = 64
N_HEADS = 32
N_KV_HEADS = 4
GROUP = N_HEADS // N_KV_HEADS
Q_W = N_HEADS * HEAD_DIM
KV_W = N_KV_HEADS * HEAD_DIM
ROPE_THETA = 10000.0
M_HEADS = 4
M_QK_DIM = 256
M_V_DIM = 512
QK_W = M_HEADS * M_QK_DIM
V_W = M_HEADS * M_V_DIM
GATE_PAD = 128
PROJ_W = 2 * QK_W + 2 * V_W + GATE_PAD
D_FF = 4 * D_MODEL
CONV_W = 3
EPS = 1e-6
NEG_INF = -1e30
PAST_LEN = 16384

SUBLANES = 8
LANES = 128
VMEM_LIMIT_BYTES = 60 * 1024 * 1024

ROW_TILE = 512
WIDE_ROW_TILE = 1024
FF_TILE = 512
PROJ_N_TILE = 896
MLSTM_CHUNK = 128


def _params(*sem):
    return pltpu.CompilerParams(dimension_semantics=sem, vmem_limit_bytes=VMEM_LIMIT_BYTES)


def _rmsnorm(xf, g):
    r = xf * lax.rsqrt(jnp.mean(xf * xf, axis=-1, keepdims=True) + EPS)
    return r * g


def _qkv_rope_kernel(x_ref, g_ref, w_ref, b_ref, cos_ref, sin_ref, q_ref, kv_ref):
    xn = _rmsnorm(x_ref[...], g_ref[...]).astype(BF16)
    y = jnp.dot(xn, w_ref[...], preferred_element_type=F32) + b_ref[...]
    cos = cos_ref[...]
    sin = sin_ref[...]
    lane = lax.broadcasted_iota(jnp.int32, cos.shape, 1)
    first_half = (lane & (HEAD_DIM - 1)) < (HEAD_DIM // 2)
    n_rot = (Q_W + KV_W) // LANES
    for c in range(n_rot):
        blk = y[:, c * LANES:(c + 1) * LANES]
        sw = jnp.where(first_half, pltpu.roll(blk, LANES - HEAD_DIM // 2, 1),
                       pltpu.roll(blk, HEAD_DIM // 2, 1))
        r = blk * cos + sw * sin
        if c < Q_W // LANES:
            q_ref[:, c * LANES:(c + 1) * LANES] = r.astype(q_ref.dtype)
        else:
            o = c * LANES - Q_W
            kv_ref[:, o:o + LANES] = r
    kv_ref[:, KV_W:2 * KV_W] = y[:, Q_W + KV_W:Q_W + 2 * KV_W]


def _qkv_rope(x, g, w, b, cos, sin, tm, q_dtype):
    m = x.shape[0]
    n = w.shape[1]
    n_pos_tiles = cos.shape[0] // tm
    return pl.pallas_call(
        _qkv_rope_kernel,
        grid=(m // tm,),
        in_specs=[
            pl.BlockSpec((tm, D_MODEL), lambda i: (i, 0)),
            pl.BlockSpec((1, D_MODEL), lambda i: (0, 0)),
            pl.BlockSpec((D_MODEL, n), lambda i: (0, 0)),
            pl.BlockSpec((1, n), lambda i: (0, 0)),
            pl.BlockSpec((tm, LANES), lambda i: (i % n_pos_tiles, 0)),
            pl.BlockSpec((tm, LANES), lambda i: (i % n_pos_tiles, 0)),
        ],
        out_specs=[
            pl.BlockSpec((tm, Q_W), lambda i: (i, 0)),
            pl.BlockSpec((tm, 2 * KV_W), lambda i: (i, 0)),
        ],
        out_shape=[
            jax.ShapeDtypeStruct((m, Q_W), q_dtype),
            jax.ShapeDtypeStruct((m, 2 * KV_W), F32),
        ],
        compiler_params=_params("arbitrary"),
        name="qkv_rope",
    )(x, g, w, b, cos, sin)


def _attn_prompt_kernel(sinks_ref, q_ref, kvp_ref, kvc_ref, o_ref):
    n = pl.program_id(1)
    w = WINDOW
    ri = lax.broadcasted_iota(jnp.int32, (w, 2 * w), 0)
    ci = lax.broadcasted_iota(jnp.int32, (w, 2 * w), 1)
    has_prev = jnp.full((w, 2 * w), n, jnp.int32) > 0
    allowed = ((ci < w) & (ci > ri) & has_prev) | ((ci >= w) & ((ci - w) <= ri))
    for g in range(N_KV_HEADS):
        ks = slice(g * HEAD_DIM, (g + 1) * HEAD_DIM)
        vs = slice(KV_W + g * HEAD_DIM, KV_W + (g + 1) * HEAD_DIM)
        k = jnp.concatenate([kvp_ref[:, ks], kvc_ref[:, ks]], axis=0).astype(BF16)
        v = jnp.concatenate([kvp_ref[:, vs], kvc_ref[:, vs]], axis=0).astype(BF16)
        q = jnp.concatenate(
            [q_ref[:, (g * GROUP + h) * HEAD_DIM:(g * GROUP + h + 1) * HEAD_DIM] for h in range(GROUP)],
            axis=0)
        s = lax.dot_general(q, k, (((1,), (1,)), ((), ())), preferred_element_type=F32)
        s = s * (HEAD_DIM ** -0.5)
        ps = []
        for h in range(GROUP):
            sh = jnp.where(allowed, s[h * w:(h + 1) * w], NEG_INF)
            sink = sinks_ref[g * GROUP + h]
            m = jnp.maximum(jnp.max(sh, axis=-1, keepdims=True), sink)
            p = jnp.exp(sh - m)
            l = jnp.sum(p, axis=-1, keepdims=True) + jnp.exp(sink - m)
            ps.append((p / l).astype(BF16))
        o = jnp.dot(jnp.concatenate(ps, axis=0), v, preferred_element_type=F32)
        for h in range(GROUP):
            c0 = (g * GROUP + h) * HEAD_DIM
            o_ref[:, c0:c0 + HEAD_DIM] = o[h * w:(h + 1) * w].astype(o_ref.dtype)


def _attn_prompt(sinks, q, kv, batch, seq):
    nb = seq // WINDOW
    return pl.pallas_call(
        _attn_prompt_kernel,
        grid=(batch, nb),
        in_specs=[
            pl.BlockSpec(memory_space=pltpu.SMEM),
            pl.BlockSpec((WINDOW, Q_W), lambda b, n: (b * nb + n, 0)),
            pl.BlockSpec((WINDOW, 2 * KV_W), lambda b, n: (b * nb + jnp.maximum(n - 1, 0), 0)),
            pl.BlockSpec((WINDOW, 2 * KV_W), lambda b, n: (b * nb + n, 0)),
        ],
        out_specs=pl.BlockSpec((WINDOW, Q_W), lambda b, n: (b * nb + n, 0)),
        out_shape=jax.ShapeDtypeStruct((batch * seq, Q_W), BF16),
        compiler_params=_params("arbitrary", "arbitrary"),
        name="attn_prompt",
    )(sinks, q, kv, kv)


def _attn_sample_kernel(sinks_ref, q_ref, kvn_ref, ck_ref, cv_ref, o_ref, ko_ref, vo_ref):
    t = q_ref.shape[0]
    wc = ck_ref.shape[1]
    rows = GROUP * t
    ri = lax.broadcasted_iota(jnp.int32, (rows, wc), 0) & (t - 1)
    ci = lax.broadcasted_iota(jnp.int32, (rows, wc), 1)
    allowed_c = ci > ri
    ri_n = lax.broadcasted_iota(jnp.int32, (rows, t), 0) & (t - 1)
    ci_n = lax.broadcasted_iota(jnp.int32, (rows, t), 1)
    allowed_n = ci_n <= ri_n
    scale = HEAD_DIM ** -0.5
    for g in range(N_KV_HEADS):
        ks = slice(g * HEAD_DIM, (g + 1) * HEAD_DIM)
        vs = slice(KV_W + g * HEAD_DIM, KV_W + (g + 1) * HEAD_DIM)
        kc = ck_ref[0, :, ks].astype(BF16)
        vc = cv_ref[0, :, ks].astype(BF16)
        kn = kvn_ref[:, ks]
        vn = kvn_ref[:, vs]
        q = jnp.concatenate(
            [q_ref[:, (g * GROUP + h) * HEAD_DIM:(g * GROUP + h + 1) * HEAD_DIM] for h in range(GROUP)],
            axis=0)
        s_c = lax.dot_general(q.astype(BF16), kc, (((1,), (1,)), ((), ())), preferred_element_type=F32) * scale
        s_n = lax.dot_general(q, kn, (((1,), (1,)), ((), ())), preferred_element_type=F32) * scale
        s_c = jnp.where(allowed_c, s_c, NEG_INF)
        s_n = jnp.where(allowed_n, s_n, NEG_INF)
        sink = jnp.concatenate([jnp.full((t, 1), sinks_ref[g * GROUP + h], F32) for h in range(GROUP)], axis=0)
        m = jnp.maximum(jnp.maximum(jnp.max(s_c, axis=-1, keepdims=True),
                                    jnp.max(s_n, axis=-1, keepdims=True)), sink)
        p_c = jnp.exp(s_c - m)
        p_n = jnp.exp(s_n - m)
        l = (jnp.sum(p_c, axis=-1, keepdims=True) + jnp.sum(p_n, axis=-1, keepdims=True)
             + jnp.exp(sink - m))
        o = (jnp.dot((p_c / l).astype(BF16), vc, preferred_element_type=F32)
             + jnp.dot(p_n / l, vn, preferred_element_type=F32))
        for h in range(GROUP):
            c0 = (g * GROUP + h) * HEAD_DIM
            o_ref[:, c0:c0 + HEAD_DIM] = o[h * t:(h + 1) * t]
    ko_ref[0, 0:wc - t, :] = ck_ref[0, t:wc, :]
    ko_ref[0, wc - t:wc, :] = kvn_ref[:, 0:KV_W]
    vo_ref[0, 0:wc - t, :] = cv_ref[0, t:wc, :]
    vo_ref[0, wc - t:wc, :] = kvn_ref[:, KV_W:2 * KV_W]


def _attn_sample(sinks, q, kvn, ck, cv, t):
    db = ck.shape[0]
    wc = ck.shape[1]
    return pl.pallas_call(
        _attn_sample_kernel,
        grid=(db,),
        in_specs=[
            pl.BlockSpec(memory_space=pltpu.SMEM),
            pl.BlockSpec((t, Q_W), lambda b: (b, 0)),
            pl.BlockSpec((t, 2 * KV_W), lambda b: (b, 0)),
            pl.BlockSpec((1, wc, KV_W), lambda b: (b, 0, 0)),
            pl.BlockSpec((1, wc, KV_W), lambda b: (b, 0, 0)),
        ],
        out_specs=[
            pl.BlockSpec((t, Q_W), lambda b: (b, 0)),
            pl.BlockSpec((1, wc, KV_W), lambda b: (b, 0, 0)),
            pl.BlockSpec((1, wc, KV_W), lambda b: (b, 0, 0)),
        ],
        out_shape=[
            jax.ShapeDtypeStruct((db * t, Q_W), F32),
            jax.ShapeDtypeStruct((db, wc, KV_W), F32),
            jax.ShapeDtypeStruct((db, wc, KV_W), F32),
        ],
        compiler_params=_params("arbitrary"),
        name="attn_sample",
    )(sinks, q, kvn, ck, cv)


def _proj_post_kernel(has_bias, a_ref, w_ref, b_ref, x_ref, g_ref, o_ref):
    y = jnp.dot(a_ref[...].astype(BF16), w_ref[...], preferred_element_type=F32)
    if has_bias:
        y = y + b_ref[...]
    o_ref[...] = x_ref[...] + _rmsnorm(y, g_ref[...])


def _proj_post(a, w, b, x, g, tm):
    m, k = a.shape
    has_bias = b is not None
    if b is None:
        b = jnp.zeros((1, D_MODEL), F32)
    return pl.pallas_call(
        functools.partial(_proj_post_kernel, has_bias),
        grid=(m // tm,),
        in_specs=[
            pl.BlockSpec((tm, k), lambda i: (i, 0)),
            pl.BlockSpec((k, D_MODEL), lambda i: (0, 0)),
            pl.BlockSpec((1, D_MODEL), lambda i: (0, 0)),
            pl.BlockSpec((tm, D_MODEL), lambda i: (i, 0)),
            pl.BlockSpec((1, D_MODEL), lambda i: (0, 0)),
        ],
        out_specs=pl.BlockSpec((tm, D_MODEL), lambda i: (i, 0)),
        out_shape=jax.ShapeDtypeStruct((m, D_MODEL), F32),
        compiler_params=_params("arbitrary"),
        name="proj_post",
    )(a, w, b, x, g)


def _norm_matmul_kernel(x_ref, g_ref, w_ref, b_ref, o_ref, xn_ref):
    @pl.when(pl.program_id(1) == 0)
    def _():
        xn_ref[...] = _rmsnorm(x_ref[...], g_ref[...]).astype(BF16)

    y = jnp.dot(xn_ref[...], w_ref[...], preferred_element_type=F32) + b_ref[...]
    o_ref[...] = y.astype(o_ref.dtype)


def _norm_matmul(x, g, w, b, tm, tn):
    m = x.shape[0]
    n = w.shape[1]
    return pl.pallas_call(
        _norm_matmul_kernel,
        grid=(m // tm, n // tn),
        in_specs=[
            pl.BlockSpec((tm, D_MODEL), lambda i, j: (i, 0)),
            pl.BlockSpec((1, D_MODEL), lambda i, j: (0, 0)),
            pl.BlockSpec((D_MODEL, tn), lambda i, j: (0, j)),
            pl.BlockSpec((1, tn), lambda i, j: (0, j)),
        ],
        out_specs=pl.BlockSpec((tm, tn), lambda i, j: (i, j)),
        out_shape=jax.ShapeDtypeStruct((m, n), F32),
        scratch_shapes=[pltpu.VMEM((tm, D_MODEL), BF16)],
        compiler_params=_params("arbitrary", "arbitrary"),
        name="norm_matmul",
    )(x, g, w, b)


def _log_sigmoid(x):
    return jnp.minimum(x, 0.0) - jnp.log1p(jnp.exp(-jnp.abs(x)))


def _mlstm_chunk(q, k, v, li, lf, c_state, n_state, m_state, mxu_dtype):
    L = q.shape[0]
    ri = lax.broadcasted_iota(jnp.int32, (L, L), 0)
    ci = lax.broadcasted_iota(jnp.int32, (L, L), 1)
    eye = ri == ci
    tril = ci <= ri
    lf_row = jnp.sum(jnp.where(eye, lf, 0.0), axis=0, keepdims=True)
    li_row = jnp.sum(jnp.where(eye, li, 0.0), axis=0, keepdims=True)
    b_col = jnp.sum(jnp.where(tril, lf_row, 0.0), axis=1, keepdims=True)
    b_row = jnp.sum(jnp.where(ri <= ci, lf, 0.0), axis=0, keepdims=True)
    dmat = jnp.where(tril, b_col - b_row + li_row, -jnp.inf)
    inter = b_col + m_state
    m_row = jnp.maximum(inter, jnp.max(dmat, axis=1, keepdims=True))
    qm = q.astype(mxu_dtype)
    km = k.astype(mxu_dtype)
    vm = v.astype(mxu_dtype)
    s = lax.dot_general(qm, km, (((1,), (1,)), ((), ())), preferred_element_type=F32)
    sm = s * jnp.exp(dmat - m_row)
    w_inter = jnp.exp(inter - m_row)
    num = (jnp.dot(sm.astype(mxu_dtype), vm, preferred_element_type=F32)
           + w_inter * jnp.dot(qm, c_state.astype(mxu_dtype), preferred_element_type=F32))
    den = jnp.sum(sm, axis=1, keepdims=True) + w_inter * jnp.sum(q * n_state, axis=1, keepdims=True)
    h = num / jnp.maximum(jnp.abs(den), jnp.exp(-m_row))
    b_last = b_col[L - 1:L, :]
    gk = b_last - b_col + li
    m_new = jnp.maximum(b_last + m_state, jnp.max(gk, axis=0, keepdims=True))
    wk = jnp.exp(gk - m_new)
    decay = jnp.exp(b_last + m_state - m_new)
    kw = k * wk
    c_new = decay * c_state + lax.dot_general(kw.astype(mxu_dtype), vm, (((0,), (0,)), ((), ())),
                                              preferred_element_type=F32)
    n_new = decay * n_state + jnp.sum(kw, axis=0, keepdims=True)
    return h, c_new, n_new, m_new


def _mlstm_head(cols, hd, ng, c_state, n_state, m_state, mxu_dtype):
    q = cols(hd * M_QK_DIM, (hd + 1) * M_QK_DIM) * (M_QK_DIM ** -0.5)
    k = cols(QK_W + hd * M_QK_DIM, QK_W + (hd + 1) * M_QK_DIM)
    v = cols(2 * QK_W + hd * M_V_DIM, 2 * QK_W + (hd + 1) * M_V_DIM)
    o = cols(2 * QK_W + V_W + hd * M_V_DIM, 2 * QK_W + V_W + (hd + 1) * M_V_DIM)
    g0 = 2 * QK_W + 2 * V_W
    li = cols(g0 + hd, g0 + hd + 1)
    lf = _log_sigmoid(cols(g0 + M_HEADS + hd, g0 + M_HEADS + hd + 1))
    h, c_new, n_new, m_new = _mlstm_chunk(q, k, v, li, lf, c_state, n_state, m_state, mxu_dtype)
    hn = _rmsnorm(h, ng) * jax.nn.sigmoid(o)
    return hn, c_new, n_new, m_new


def _mlstm_prompt_kernel(batch, proj_ref, ng_ref, h_ref, c_ref, n_ref, m_ref):
    @pl.when(pl.program_id(0) == 0)
    def _():
        c_ref[...] = jnp.zeros(c_ref.shape, F32)
        n_ref[...] = jnp.zeros(n_ref.shape, F32)
        m_ref[...] = jnp.zeros(m_ref.shape, F32)

    for b in range(batch):
        for hd in range(M_HEADS):
            r = b * M_HEADS + hd
            hn, c_new, n_new, m_new = _mlstm_head(
                lambda lo, hi, b=b: proj_ref[b, :, lo:hi], hd, ng_ref[:, hd * M_V_DIM:(hd + 1) * M_V_DIM],
                c_ref[r], n_ref[r:r + 1, :], m_ref[r:r + 1, 0:1], BF16)
            h_ref[b, :, hd * M_V_DIM:(hd + 1) * M_V_DIM] = hn.astype(h_ref.dtype)
            c_ref[r] = c_new
            n_ref[r:r + 1, :] = n_new
            m_ref[r:r + 1, :] = jnp.broadcast_to(m_new, (1, LANES))


def _mlstm_prompt(proj, ng, batch, seq):
    chunk = MLSTM_CHUNK
    rows = batch * M_HEADS
    return pl.pallas_call(
        functools.partial(_mlstm_prompt_kernel, batch),
        grid=(seq // chunk,),
        in_specs=[
            pl.BlockSpec((batch, chunk, PROJ_W), lambda c: (0, c, 0)),
            pl.BlockSpec((1, V_W), lambda c: (0, 0)),
        ],
        out_specs=[
            pl.BlockSpec((batch, chunk, V_W), lambda c: (0, c, 0)),
            pl.BlockSpec((rows, M_QK_DIM, M_V_DIM), lambda c: (0, 0, 0)),
            pl.BlockSpec((rows, M_QK_DIM), lambda c: (0, 0)),
            pl.BlockSpec((rows, LANES), lambda c: (0, 0)),
        ],
        out_shape=[
            jax.ShapeDtypeStruct((batch, seq, V_W), BF16),
            jax.ShapeDtypeStruct((rows, M_QK_DIM, M_V_DIM), F32),
            jax.ShapeDtypeStruct((rows, M_QK_DIM), F32),
            jax.ShapeDtypeStruct((rows, LANES), F32),
        ],
        compiler_params=_params("arbitrary"),
        name="mlstm_prompt",
    )(proj.reshape(batch, seq, PROJ_W), ng)


def _mlstm_sample_kernel(proj_ref, ng_ref, c0_ref, n0_ref, m0_ref, h_ref, c_ref, n_ref, m_ref):
    for hd in range(M_HEADS):
        hn, c_new, n_new, m_new = _mlstm_head(
            lambda lo, hi: proj_ref[:, lo:hi], hd, ng_ref[:, hd * M_V_DIM:(hd + 1) * M_V_DIM],
            c0_ref[0, hd], n0_ref[0, hd:hd + 1, :], m0_ref[0, :, hd:hd + 1], F32)
        h_ref[:, hd * M_V_DIM:(hd + 1) * M_V_DIM] = hn
        c_ref[0, hd] = c_new
        n_ref[0, hd:hd + 1, :] = n_new
        m_ref[0, :, hd:hd + 1] = m_new


def _mlstm_sample(proj, ng, c0, n0, m0, t):
    db = c0.shape[0]
    m0 = m0.reshape(db, 1, M_HEADS)
    state_specs = [
        pl.BlockSpec((1, M_HEADS, M_QK_DIM, M_V_DIM), lambda b: (b, 0, 0, 0)),
        pl.BlockSpec((1, M_HEADS, M_QK_DIM), lambda b: (b, 0, 0)),
        pl.BlockSpec((1, 1, M_HEADS), lambda b: (b, 0, 0)),
    ]
    return pl.pallas_call(
        _mlstm_sample_kernel,
        grid=(db,),
        in_specs=[
            pl.BlockSpec((t, PROJ_W), lambda b: (b, 0)),
            pl.BlockSpec((1, V_W), lambda b: (0, 0)),
        ] + state_specs,
        out_specs=[pl.BlockSpec((t, V_W), lambda b: (b, 0))] + state_specs,
        out_shape=[
            jax.ShapeDtypeStruct((db * t, V_W), F32),
            jax.ShapeDtypeStruct(c0.shape, F32),
            jax.ShapeDtypeStruct(n0.shape, F32),
            jax.ShapeDtypeStruct(m0.shape, F32),
        ],
        compiler_params=_params("arbitrary"),
        name="mlstm_sample",
    )(proj, ng, c0, n0, m0)


def _conv3(u, u1, u2, w, b):
    return ((b + w[0:1] * u2) + w[1:2] * u1) + w[2:3] * u


def _conv_stream(u, tail, w, b):
    u1 = pltpu.roll(u, 1, 0)
    u2 = pltpu.roll(u, 2, 0)
    c = _conv3(u, u1, u2, w, b)
    uf = u[0:SUBLANES]
    row = lax.broadcasted_iota(jnp.int32, uf.shape, 0)
    uf1 = jnp.where(row < 1, pltpu.roll(tail, 1, 0), pltpu.roll(uf, 1, 0))
    uf2 = jnp.where(row < 2, pltpu.roll(tail, 2, 0), pltpu.roll(uf, 2, 0))
    cf = _conv3(uf, uf1, uf2, w, b)
    return jnp.concatenate([cf, c[SUBLANES:]], axis=0)


def _conv_seq8(u, prev, w, b):
    nseq = u.shape[0] // SUBLANES
    p0 = jnp.broadcast_to(prev[:, 0:1, :], (nseq, SUBLANES, u.shape[1])).reshape(u.shape)
    p1 = jnp.broadcast_to(prev[:, 1:2, :], (nseq, SUBLANES, u.shape[1])).reshape(u.shape)
    row = lax.broadcasted_iota(jnp.int32, u.shape, 0) & (SUBLANES - 1)
    u1 = jnp.where(row == 0, p1, pltpu.roll(u, 1, 0))
    u2 = jnp.where(row == 0, p0, jnp.where(row == 1, p1, pltpu.roll(u, 2, 0)))
    return _conv3(u, u1, u2, w, b)


def _ffn_kernel(stream, tiles_per_seq, x_ref, gpre_ref, wg_ref, wv_ref, cwg_ref, cwv_ref, cbg_ref,
                cbv_ref, wd_ref, gpost_ref, *rest):
    if stream:
        o_ref, sg_ref, sv_ref, xn_ref, tail_g, tail_v = rest
    else:
        pg_ref, pv_ref, o_ref, sg_ref, sv_ref, xn_ref = rest
    i = pl.program_id(0)
    j = pl.program_id(1)
    tm = x_ref.shape[0]

    @pl.when(j == 0)
    def _():
        xn_ref[...] = _rmsnorm(x_ref[...], gpre_ref[...]).astype(BF16)
        o_ref[...] = jnp.zeros(o_ref.shape, F32)

    if stream:
        @pl.when(i % tiles_per_seq == 0)
        def _():
            tail_g[j] = jnp.zeros(tail_g.shape[1:], F32)
            tail_v[j] = jnp.zeros(tail_v.shape[1:], F32)

    xn = xn_ref[...]
    ug = jnp.dot(xn, wg_ref[...], preferred_element_type=F32)
    uv = jnp.dot(xn, wv_ref[...], preferred_element_type=F32)
    if stream:
        cg = _conv_stream(ug, tail_g[j], cwg_ref[...], cbg_ref[...])
        cv = _conv_stream(uv, tail_v[j], cwv_ref[...], cbv_ref[...])
        tail_g[j] = ug[tm - SUBLANES:]
        tail_v[j] = uv[tm - SUBLANES:]
        sg_ref[0] = ug[tm - SUBLANES:]
        sv_ref[0] = uv[tm - SUBLANES:]
    else:
        cg = _conv_seq8(ug, pg_ref[...], cwg_ref[...], cbg_ref[...])
        cv = _conv_seq8(uv, pv_ref[...], cwv_ref[...], cbv_ref[...])
        sg_ref[...] = ug.reshape(sg_ref.shape)
        sv_ref[...] = uv.reshape(sv_ref.shape)
    h = (jax.nn.gelu(cg, approximate=True) * cv).astype(BF16)
    o_ref[...] += jnp.dot(h, wd_ref[...], preferred_element_type=F32)

    @pl.when(j == pl.num_programs(1) - 1)
    def _():
        o_ref[...] = x_ref[...] + _rmsnorm(o_ref[...], gpost_ref[...])


def _ffn(x, prev, layer, gpre, w_up, conv_w, conv_b, w_down, gpost, tm, tf, seq):
    m = x.shape[0]
    nf = D_FF // tf
    stream = prev is None
    in_specs = [
        pl.BlockSpec((tm, D_MODEL), lambda i, j: (i, 0), pipeline_mode=pl.Buffered(1)),
        pl.BlockSpec((1, D_MODEL), lambda i, j: (0, 0)),
        pl.BlockSpec((None, D_MODEL, tf), lambda i, j: (layer, 0, j)),
        pl.BlockSpec((None, D_MODEL, tf), lambda i, j: (layer, 0, j + nf)),
        pl.BlockSpec((None, CONV_W, tf), lambda i, j: (layer, 0, j)),
        pl.BlockSpec((None, CONV_W, tf), lambda i, j: (layer, 0, j + nf)),
        pl.BlockSpec((None, 1, tf), lambda i, j: (layer, 0, j)),
        pl.BlockSpec((None, 1, tf), lambda i, j: (layer, 0, j + nf)),
        pl.BlockSpec((None, tf, D_MODEL), lambda i, j: (layer, j, 0)),
        pl.BlockSpec((1, D_MODEL), lambda i, j: (0, 0)),
    ]
    conv_b = conv_b.reshape(conv_b.shape[0], 1, 2 * D_FF)
    args = [x, gpre, w_up, w_up, conv_w, conv_w, conv_b, conv_b, w_down, gpost]
    scratch = [pltpu.VMEM((tm, D_MODEL), BF16)]
    if stream:
        tiles_per_seq = seq // tm
        nseq = m // tm
        state_spec = pl.BlockSpec((1, SUBLANES, tf), lambda i, j: (i, 0, j))
        scratch += [pltpu.VMEM((nf, SUBLANES, tf), F32), pltpu.VMEM((nf, SUBLANES, tf), F32)]
    else:
        assert seq == SUBLANES and m == tm
        tiles_per_seq = 1
        nseq = m // seq
        state_spec = pl.BlockSpec((nseq, SUBLANES, tf), lambda i, j: (0, 0, j))
        in_specs += [
            pl.BlockSpec((nseq, CONV_W - 1, tf), lambda i, j: (0, 0, j)),
            pl.BlockSpec((nseq, CONV_W - 1, tf), lambda i, j: (0, 0, j + nf)),
        ]
        args += [prev, prev]
    return pl.pallas_call(
        functools.partial(_ffn_kernel, stream, tiles_per_seq),
        grid=(m // tm, nf),
        in_specs=in_specs,
        out_specs=[pl.BlockSpec((tm, D_MODEL), lambda i, j: (i, 0)), state_spec, state_spec],
        out_shape=[
            jax.ShapeDtypeStruct((m, D_MODEL), F32),
            jax.ShapeDtypeStruct((nseq, SUBLANES, D_FF), F32),
            jax.ShapeDtypeStruct((nseq, SUBLANES, D_FF), F32),
        ],
        scratch_shapes=scratch,
        compiler_params=_params("arbitrary", "arbitrary"),
        name="conv_ffn",
    )(*args)


def _rope_tables(pos):
    half = HEAD_DIM // 2
    inv = ROPE_THETA ** (-jnp.arange(half, dtype=F32) / half)
    ang = pos.astype(F32)[:, None] * inv[None, :]
    cos = jnp.cos(ang)
    sin = jnp.sin(ang)
    reps = LANES // HEAD_DIM
    return (jnp.tile(cos, (1, 2 * reps)), jnp.tile(jnp.concatenate([-sin, sin], axis=1), (1, reps)))


def _conv_state(sg, sv, blocks_per_seq):
    last = slice(blocks_per_seq - 1, None, blocks_per_seq)
    keep = slice(SUBLANES - (CONV_W - 1), SUBLANES)
    return jnp.concatenate([sg[last, keep], sv[last, keep]], axis=-1)


def _trunk(x, pos, seq, tm, wide_tm, cache, state, state_conv, p):
    m = x.shape[0]
    nseq = m // seq
    sample = cache is not None
    row = lambda v: v.reshape(1, -1)

    cos, sin = _rope_tables(pos)
    q, kv = _qkv_rope(x, row(p["norm_mix_pre"][0]), p["w_qkv"], row(p["attn_b_qkv"][0]), cos, sin, tm,
                      F32 if sample else BF16)
    if sample:
        ck, cv = cache
        o, k_new, v_new = _attn_sample(p["attn_sinks"][0], q, kv, ck.reshape(nseq, WINDOW, KV_W),
                                       cv.reshape(nseq, WINDOW, KV_W), seq)
    else:
        o = _attn_prompt(p["attn_sinks"][0], q, kv, nseq, seq)
        kv3 = kv.reshape(nseq, seq, 2 * KV_W)
        k_new = kv3[:, seq - WINDOW:, :KV_W]
        v_new = kv3[:, seq - WINDOW:, KV_W:]
    k_new = k_new.reshape(1, nseq, WINDOW, N_KV_HEADS, HEAD_DIM)
    v_new = v_new.reshape(1, nseq, WINDOW, N_KV_HEADS, HEAD_DIM)
    x = _proj_post(o, p["w_o"], row(p["attn_b_o"][0]), x, row(p["norm_mix_post"][0]), tm)
    x, sg0, sv0 = _ffn(x, state_conv[0] if sample else None, 0, row(p["norm_ffn_pre"][0]), p["w_up"],
                       p["ffn_conv_w"], p["ffn_conv_b"], p["w_down"], row(p["norm_ffn_post"][0]),
                       wide_tm, FF_TILE, seq)

    proj = _norm_matmul(x, row(p["norm_mix_pre"][1]), p["w_in"], p["b_in"], wide_tm, PROJ_N_TILE)
    ng = row(p["mlstm_norm"][0])
    if sample:
        c0, n0, m0 = state
        h, c_new, n_new, m_new = _mlstm_sample(proj, ng, c0, n0, m0, seq)
    else:
        h, c_new, n_new, m_new = _mlstm_prompt(proj, ng, nseq, seq)
        h = h.reshape(m, V_W)
        m_new = m_new[:, 0]
    c_new = c_new.reshape(1, nseq, M_HEADS, M_QK_DIM, M_V_DIM)
    n_new = n_new.reshape(1, nseq, M_HEADS, M_QK_DIM)
    m_new = m_new.reshape(1, nseq, M_HEADS)
    x = _proj_post(h, p["w_out"], None, x, row(p["norm_mix_post"][1]), tm)
    x, sg1, sv1 = _ffn(x, state_conv[1] if sample else None, 1, row(p["norm_ffn_pre"][1]), p["w_up"],
                       p["ffn_conv_w"], p["ffn_conv_b"], p["w_down"], row(p["norm_ffn_post"][1]),
                       wide_tm, FF_TILE, seq)
    bps = 1 if sample else seq // wide_tm
    conv = jnp.stack([_conv_state(sg0, sv0, bps), _conv_state(sg1, sv1, bps)])
    return x, k_new, v_new, c_new, n_new, m_new, conv


def kernel(x_prompt, x_sample, cache_k, cache_v, state_C, state_n, state_m, state_conv, norm_mix_pre,
           norm_mix_post, norm_ffn_pre, norm_ffn_post, attn_w_qkv, attn_b_qkv, attn_w_o, attn_b_o, attn_sinks,
           mlstm_w_in, mlstm_b_gates, mlstm_norm, mlstm_w_out, ffn_w_up, ffn_conv_w, ffn_conv_b, ffn_w_down):
    batch, seq, _ = x_prompt.shape
    dec_batch, dec_seq, _ = x_sample.shape
    n_gates = 2 * M_HEADS
    g0 = 2 * QK_W + 2 * V_W
    w_in = mlstm_w_in[0]
    p = dict(
        norm_mix_pre=norm_mix_pre, norm_mix_post=norm_mix_post, norm_ffn_pre=norm_ffn_pre,
        norm_ffn_post=norm_ffn_post, attn_b_qkv=attn_b_qkv, attn_b_o=attn_b_o, attn_sinks=attn_sinks,
        mlstm_norm=mlstm_norm, ffn_conv_w=ffn_conv_w, ffn_conv_b=ffn_conv_b,
        w_qkv=attn_w_qkv[0].astype(BF16),
        w_o=attn_w_o[0].astype(BF16),
        w_in=jnp.pad(w_in, ((0, 0), (0, GATE_PAD - n_gates))).astype(BF16),
        b_in=jnp.pad(mlstm_b_gates[0], (g0, GATE_PAD - n_gates)).reshape(1, PROJ_W),
        w_out=mlstm_w_out[0].astype(BF16),
        w_up=ffn_w_up.astype(BF16),
        w_down=ffn_w_down.astype(BF16),
    )
    yp, k_p, v_p, c_p, n_p, m_p, conv_p = _trunk(
        x_prompt.reshape(batch * seq, D_MODEL), jnp.arange(seq), seq, ROW_TILE, WIDE_ROW_TILE, None, None, None, p)
    ys, k_s, v_s, c_s, n_s, m_s, conv_s = _trunk(
        x_sample.reshape(dec_batch * dec_seq, D_MODEL),
        jnp.tile(PAST_LEN + jnp.arange(dec_seq), dec_batch), dec_seq, dec_batch * dec_seq, dec_batch * dec_seq,
        (cache_k[0], cache_v[0]), (state_C[0], state_n[0], state_m[0]), state_conv, p)
    return (yp.reshape(batch, seq, D_MODEL), ys.reshape(dec_batch, dec_seq, D_MODEL),
            k_p, v_p, k_s, v_s, c_p, n_p, m_p, c_s, n_s, m_s, conv_p, conv_s)
```

```python
import functools
import math

import jax
import jax.numpy as jnp
from jax import lax
from jax.experimental import pallas as pl
from jax.experimental.pallas import tpu as pltpu

F32 = jnp.float32
BF16 = jnp.bfloat16

D_MODEL = 2048
WINDOW = 128
HEAD_DIM = 64
N_HEADS = 32
N_KV_HEADS = 4
GROUP = N_HEADS // N_KV_HEADS
Q_W = N_HEADS * HEAD_DIM
KV_W = N_KV_HEADS * HEAD_DIM
ROPE_THETA = 10000.0
M_HEADS = 4
M_QK_DIM = 256
M_V_DIM = 512
QK_W = M_HEADS * M_QK_DIM
V_W = M_HEADS * M_V_DIM
GATE_PAD = 128
PROJ_W = 2 * QK_W + 2 * V_W + GATE_PAD
D_FF = 4 * D_MODEL
CONV_W = 3
EPS = 1e-6
NEG_INF = -1e30
PAST_LEN = 16384

SUBLANES = 8
LANES = 128
VMEM_LIMIT_BYTES = 60 * 1024 * 1024

ROW_TILE = 512
WIDE_ROW_TILE = 1024
FF_TILE = 512
PROJ_N_TILE = 896
MLSTM_CHUNK = 128


def _params(*sem):
    return pltpu.CompilerParams(dimension_semantics=sem, vmem_limit_bytes=VMEM_LIMIT_BYTES)


def _rmsnorm(xf, g):
    r = xf * lax.rsqrt(jnp.mean(xf * xf, axis=-1, keepdims=True) + EPS)
    return r * g


def _qkv_rope_kernel(x_ref, g_ref, w_ref, b_ref, cos_ref, sin_ref, q_ref, kv_ref):
    xn = _rmsnorm(x_ref[...], g_ref[...]).astype(BF16)
    y = jnp.dot(xn, w_ref[...], preferred_element_type=F32) + b_ref[...]
    cos = cos_ref[...]
    sin = sin_ref[...]
    lane = lax.broadcasted_iota(jnp.int32, cos.shape, 1)
    first_half = (lane & (HEAD_DIM - 1)) < (HEAD_DIM // 2)
    n_rot = (Q_W + KV_W) // LANES
    for c in range(n_rot):
        blk = y[:, c * LANES:(c + 1) * LANES]
        sw = jnp.where(first_half, pltpu.roll(blk, LANES - HEAD_DIM // 2, 1),
                       pltpu.roll(blk, HEAD_DIM // 2, 1))
        r = blk * cos + sw * sin
        if c < Q_W // LANES:
            q_ref[:, c * LANES:(c + 1) * LANES] = (r * (HEAD_DIM ** -0.5)).astype(q_ref.dtype)
        else:
            o = c * LANES - Q_W
            kv_ref[:, o:o + LANES] = r
    kv_ref[:, KV_W:2 * KV_W] = y[:, Q_W + KV_W:Q_W + 2 * KV_W]


def _qkv_rope(x, g, w, b, cos, sin, tm, q_dtype):
    m = x.shape[0]
    n = w.shape[1]
    n_pos_tiles = cos.shape[0] // tm
    return pl.pallas_call(
        _qkv_rope_kernel,
        grid=(m // tm,),
        in_specs=[
            pl.BlockSpec((tm, D_MODEL), lambda i: (i, 0)),
            pl.BlockSpec((1, D_MODEL), lambda i: (0, 0)),
            pl.BlockSpec((D_MODEL, n), lambda i: (0, 0)),
            pl.BlockSpec((1, n), lambda i: (0, 0)),
            pl.BlockSpec((tm, LANES), lambda i: (i % n_pos_tiles, 0)),
            pl.BlockSpec((tm, LANES), lambda i: (i % n_pos_tiles, 0)),
        ],
        out_specs=[
            pl.BlockSpec((tm, Q_W), lambda i: (i, 0)),
            pl.BlockSpec((tm, 2 * KV_W), lambda i: (i, 0)),
        ],
        out_shape=[
            jax.ShapeDtypeStruct((m, Q_W), q_dtype),
            jax.ShapeDtypeStruct((m, 2 * KV_W), F32),
        ],
        compiler_params=_params("arbitrary"),
        name="qkv_rope",
    )(x, g, w, b, cos, sin)


def _attn_prompt_kernel(sinks_ref, q_ref, kvp_ref, kvc_ref, o_ref):
    n = pl.program_id(1)
    w = WINDOW
    ri = lax.broadcasted_iota(jnp.int32, (w, 2 * w), 0)
    ci = lax.broadcasted_iota(jnp.int32, (w, 2 * w), 1)
    has_prev = jnp.full((w, 2 * w), n, jnp.int32) > 0
    allowed = ((ci < w) & (ci > ri) & has_prev) | ((ci >= w) & ((ci - w) <= ri))
    low = lax.broadcasted_iota(jnp.int32, (w, LANES), 1) < HEAD_DIM
    ones = jnp.ones((2 * w, HEAD_DIM), BF16)
    for g in range(N_KV_HEADS):
        ks = slice(g * HEAD_DIM, (g + 1) * HEAD_DIM)
        vs = slice(KV_W + g * HEAD_DIM, KV_W + (g + 1) * HEAD_DIM)
        k = jnp.concatenate([kvp_ref[:, ks], kvc_ref[:, ks]], axis=0).astype(BF16)
        v = jnp.concatenate([kvp_ref[:, vs], kvc_ref[:, vs]], axis=0).astype(BF16)
        v1 = jnp.concatenate([v, ones], axis=1)
        q = jnp.concatenate(
            [q_ref[:, (g * GROUP + h) * HEAD_DIM:(g * GROUP + h + 1) * HEAD_DIM] for h in range(GROUP)],
            axis=0)
        s = lax.dot_general(q, k, (((1,), (1,)), ((), ())), preferred_element_type=F32)
        ps, es = [], []
        for h in range(GROUP):
            sh = jnp.where(allowed, s[h * w:(h + 1) * w], NEG_INF)
            sink = sinks_ref[g * GROUP + h]
            m = jnp.maximum(jnp.max(sh, axis=-1, keepdims=True), sink)
            ps.append(jnp.exp(sh - m).astype(BF16))
            es.append(jnp.exp(sink - m))
        oa = jnp.dot(jnp.concatenate(ps, axis=0), v1, preferred_element_type=F32)
        for h in range(0, GROUP, 2):
            a = oa[h * w:(h + 1) * w]
            b = oa[(h + 1) * w:(h + 2) * w]
            num = jnp.where(low, a, pltpu.roll(b, HEAD_DIM, 1))
            den = jnp.where(low, pltpu.roll(a, HEAD_DIM, 1) + es[h], b + es[h + 1])
            c0 = (g * GROUP + h) * HEAD_DIM
            o_ref[:, c0:c0 + LANES] = (num / den).astype(o_ref.dtype)


def _attn_prompt(sinks, q, kv, batch, seq):
    nb = seq // WINDOW
    return pl.pallas_call(
        _attn_prompt_kernel,
        grid=(batch, nb),
        in_specs=[
            pl.BlockSpec(memory_space=pltpu.SMEM),
            pl.BlockSpec((WINDOW, Q_W), lambda b, n: (b * nb + n, 0)),
            pl.BlockSpec((WINDOW, 2 * KV_W), lambda b, n: (b * nb + jnp.maximum(n - 1, 0), 0)),
            pl.BlockSpec((WINDOW, 2 * KV_W), lambda b, n: (b * nb + n, 0)),
        ],
        out_specs=pl.BlockSpec((WINDOW, Q_W), lambda b, n: (b * nb + n, 0)),
        out_shape=jax.ShapeDtypeStruct((batch * seq, Q_W), BF16),
        compiler_params=_params("arbitrary", "arbitrary"),
        name="attn_prompt",
    )(sinks, q, kv, kv)


def _attn_sample_kernel(sinks_ref, q_ref, kvn_ref, ck_ref, cv_ref, o_ref, ko_ref, vo_ref):
    t = q_ref.shape[0]
    wc = ck_ref.shape[1]
    rows = GROUP * t
    ri = lax.broadcasted_iota(jnp.int32, (rows, wc), 0) & (t - 1)
    ci = lax.broadcasted_iota(jnp.int32, (rows, wc), 1)
    allowed_c = ci > ri
    ri_n = lax.broadcasted_iota(jnp.int32, (rows, t), 0) & (t - 1)
    ci_n = lax.broadcasted_iota(jnp.int32, (rows, t), 1)
    allowed_n = ci_n <= ri_n
    for g in range(N_KV_HEADS):
        ks = slice(g * HEAD_DIM, (g + 1) * HEAD_DIM)
        vs = slice(KV_W + g * HEAD_DIM, KV_W + (g + 1) * HEAD_DIM)
        kc = ck_ref[0, :, ks].astype(BF16)
        vc = cv_ref[0, :, ks].astype(BF16)
        kn = kvn_ref[:, ks]
        vn = kvn_ref[:, vs]
        q = jnp.concatenate(
            [q_ref[:, (g * GROUP + h) * HEAD_DIM:(g * GROUP + h + 1) * HEAD_DIM] for h in range(GROUP)],
            axis=0)
        s_c = lax.dot_general(q.astype(BF16), kc, (((1,), (1,)), ((), ())), preferred_element_type=F32)
        s_n = lax.dot_general(q, kn, (((1,), (1,)), ((), ())), preferred_element_type=F32)
        s_c = jnp.where(allowed_c, s_c, NEG_INF)
        s_n = jnp.where(allowed_n, s_n, NEG_INF)
        sink = jnp.concatenate([jnp.full((t, 1), sinks_ref[g * GROUP + h], F32) for h in range(GROUP)], axis=0)
        m = jnp.maximum(jnp.maximum(jnp.max(s_c, axis=-1, keepdims=True),
                                    jnp.max(s_n, axis=-1, keepdims=True)), sink)
        p_c = jnp.exp(s_c - m)
        p_n = jnp.exp(s_n - m)
        l = (jnp.sum(p_c, axis=-1, keepdims=True) + jnp.sum(p_n, axis=-1, keepdims=True)
             + jnp.exp(sink - m))
        o = (jnp.dot((p_c / l).astype(BF16), vc, preferred_element_type=F32)
             + jnp.dot(p_n / l, vn, preferred_element_type=F32))
        for h in range(GROUP):
            c0 = (g * GROUP + h) * HEAD_DIM
            o_ref[:, c0:c0 + HEAD_DIM] = o[h * t:(h + 1) * t]
    ko_ref[0, 0:wc - t, :] = ck_ref[0, t:wc, :]
    ko_ref[0, wc - t:wc, :] = kvn_ref[:, 0:KV_W]
    vo_ref[0, 0:wc - t, :] = cv_ref[0, t:wc, :]
    vo_ref[0, wc - t:wc, :] = kvn_ref[:, KV_W:2 * KV_W]


def _attn_sample(sinks, q, kvn, ck, cv, t):
    db = ck.shape[0]
    wc = ck.shape[1]
    return pl.pallas_call(
        _attn_sample_kernel,
        grid=(db,),
        in_specs=[
            pl.BlockSpec(memory_space=pltpu.SMEM),
            pl.BlockSpec((t, Q_W), lambda b: (b, 0)),
            pl.BlockSpec((t, 2 * KV_W), lambda b: (b, 0)),
            pl.BlockSpec((1, wc, KV_W), lambda b: (b, 0, 0)),
            pl.BlockSpec((1, wc, KV_W), lambda b: (b, 0, 0)),
        ],
        out_specs=[
            pl.BlockSpec((t, Q_W), lambda b: (b, 0)),
            pl.BlockSpec((1, wc, KV_W), lambda b: (b, 0, 0)),
            pl.BlockSpec((1, wc, KV_W), lambda b: (b, 0, 0)),
        ],
        out_shape=[
            jax.ShapeDtypeStruct((db * t, Q_W), F32),
            jax.ShapeDtypeStruct((db, wc, KV_W), F32),
            jax.ShapeDtypeStruct((db, wc, KV_W), F32),
        ],
        compiler_params=_params("arbitrary"),
        name="attn_sample",
    )(sinks, q, kvn, ck, cv)


def _proj_post_kernel(has_bias, a_ref, w_ref, b_ref, x_ref, g_ref, o_ref):
    y = jnp.dot(a_ref[...].astype(BF16), w_ref[...], preferred_element_type=F32)
    if has_bias:
        y = y + b_ref[...]
    o_ref[...] = x_ref[...] + _rmsnorm(y, g_ref[...])


def _proj_post(a, w, b, x, g, tm):
    m, k = a.shape
    has_bias = b is not None
    if b is None:
        b = jnp.zeros((1, D_MODEL), F32)
    return pl.pallas_call(
        functools.partial(_proj_post_kernel, has_bias),
        grid=(m // tm,),
        in_specs=[
            pl.BlockSpec((tm, k), lambda i: (i, 0)),
            pl.BlockSpec((k, D_MODEL), lambda i: (0, 0)),
            pl.BlockSpec((1, D_MODEL), lambda i: (0, 0)),
            pl.BlockSpec((tm, D_MODEL), lambda i: (i, 0)),
            pl.BlockSpec((1, D_MODEL), lambda i: (0, 0)),
        ],
        out_specs=pl.BlockSpec((tm, D_MODEL), lambda i: (i, 0)),
        out_shape=jax.ShapeDtypeStruct((m, D_MODEL), F32),
        compiler_params=_params("arbitrary"),
        name="proj_post",
    )(a, w, b, x, g)


def _norm_matmul_kernel(x_ref, g_ref, w_ref, b_ref, o_ref, xn_ref):
    @pl.when(pl.program_id(1) == 0)
    def _():
        xn_ref[...] = _rmsnorm(x_ref[...], g_ref[...]).astype(BF16)

    y = jnp.dot(xn_ref[...], w_ref[...], preferred_element_type=F32) + b_ref[...]
    o_ref[...] = y.astype(o_ref.dtype)


def _norm_matmul(x, g, w, b, tm, tn):
    m = x.shape[0]
    n = w.shape[1]
    return pl.pallas_call(
        _norm_matmul_kernel,
        grid=(m // tm, n // tn),
        in_specs=[
            pl.BlockSpec((tm, D_MODEL), lambda i, j: (i, 0)),
            pl.BlockSpec((1, D_MODEL), lambda i, j: (0, 0)),
            pl.BlockSpec((D_MODEL, tn), lambda i, j: (0, j)),
            pl.BlockSpec((1, tn), lambda i, j: (0, j)),
        ],
        out_specs=pl.BlockSpec((tm, tn), lambda i, j: (i, j)),
        out_shape=jax.ShapeDtypeStruct((m, n), F32),
        scratch_shapes=[pltpu.VMEM((tm, D_MODEL), BF16)],
        compiler_params=_params("arbitrary", "arbitrary"),
        name="norm_matmul",
    )(x, g, w, b)


def _log_sigmoid(x):
    return jnp.minimum(x, 0.0) - jnp.log1p(jnp.exp(-jnp.abs(x)))


def _mlstm_chunk(q, k, v, li, lf, c_state, n_state, m_state, mxu_dtype):
    L = q.shape[0]
    ri = lax.broadcasted_iota(jnp.int32, (L, L), 0)
    ci = lax.broadcasted_iota(jnp.int32, (L, L), 1)
    eye = ri == ci
    tril = ci <= ri
    lf_row = jnp.sum(jnp.where(eye, lf, 0.0), axis=0, keepdims=True)
    li_row = jnp.sum(jnp.where(eye, li, 0.0), axis=0, keepdims=True)
    b_col = jnp.sum(jnp.where(tril, lf_row, 0.0), axis=1, keepdims=True)
    b_row = jnp.sum(jnp.where(ri <= ci, lf, 0.0), axis=0, keepdims=True)
    dmat = jnp.where(tril, b_col - b_row + li_row, -jnp.inf)
    inter = b_col + m_state
    m_row = jnp.maximum(inter, jnp.max(dmat, axis=1, keepdims=True))
    qm = q.astype(mxu_dtype)
    km = k.astype(mxu_dtype)
    vm = v.astype(mxu_dtype)
    s = lax.dot_general(qm, km, (((1,), (1,)), ((), ())), preferred_element_type=F32)
    sm = s * jnp.exp(dmat - m_row)
    w_inter = jnp.exp(inter - m_row)
    num = (jnp.dot(sm.astype(mxu_dtype), vm, preferred_element_type=F32)
           + w_inter * jnp.dot(qm, c_state.astype(mxu_dtype), preferred_element_type=F32))
    den = jnp.sum(sm, axis=1, keepdims=True) + w_inter * jnp.sum(q * n_state, axis=1, keepdims=True)
    h = num * (1.0 / jnp.maximum(jnp.abs(den), jnp.exp(-m_row)))
    b_last = b_col[L - 1:L, :]
    gk = b_last - b_col + li
    m_new = jnp.maximum(b_last + m_state, jnp.max(gk, axis=0, keepdims=True))
    wk = jnp.exp(gk - m_new)
    decay = jnp.exp(b_last + m_state - m_new)
    kw = k * wk
    c_new = decay * c_state + lax.dot_general(kw.astype(mxu_dtype), vm, (((0,), (0,)), ((), ())),
                                              preferred_element_type=F32)
    n_new = decay * n_state + jnp.sum(kw, axis=0, keepdims=True)
    return h, c_new, n_new, m_new


def _mlstm_head(cols, hd, ng, c_state, n_state, m_state, mxu_dtype):
    q = cols(hd * M_QK_DIM, (hd + 1) * M_QK_DIM) * (M_QK_DIM ** -0.5)
    k = cols(QK_W + hd * M_QK_DIM, QK_W + (hd + 1) * M_QK_DIM)
    v = cols(2 * QK_W + hd * M_V_DIM, 2 * QK_W + (hd + 1) * M_V_DIM)
    o = cols(2 * QK_W + V_W + hd * M_V_DIM, 2 * QK_W + V_W + (hd + 1) * M_V_DIM)
    g0 = 2 * QK_W + 2 * V_W
    li = cols(g0 + hd, g0 + hd + 1)
    lf = _log_sigmoid(cols(g0 + M_HEADS + hd, g0 + M_HEADS + hd + 1))
    h, c_new, n_new, m_new = _mlstm_chunk(q, k, v, li, lf, c_state, n_state, m_state, mxu_dtype)
    hn = _rmsnorm(h, ng) * (0.5 * (jnp.tanh(0.5 * o) + 1.0))
    return hn, c_new, n_new, m_new


def _mlstm_prompt_kernel(batch, proj_ref, ng_ref, h_ref, c_ref, n_ref, m_ref):
    @pl.when(pl.program_id(0) == 0)
    def _():
        c_ref[...] = jnp.zeros(c_ref.shape, F32)
        n_ref[...] = jnp.zeros(n_ref.shape, F32)
        m_ref[...] = jnp.zeros(m_ref.shape, F32)

    for b in range(batch):
        for hd in range(M_HEADS):
            r = b * M_HEADS + hd
            hn, c_new, n_new, m_new = _mlstm_head(
                lambda lo, hi, b=b: proj_ref[b, :, lo:hi], hd, ng_ref[:, hd * M_V_DIM:(hd + 1) * M_V_DIM],
                c_ref[r], n_ref[r:r + 1, :], m_ref[r:r + 1, 0:1], BF16)
            h_ref[b, :, hd * M_V_DIM:(hd + 1) * M_V_DIM] = hn.astype(h_ref.dtype)
            c_ref[r] = c_new
            n_ref[r:r + 1, :] = n_new
            m_ref[r:r + 1, :] = jnp.broadcast_to(m_new, (1, LANES))


def _mlstm_prompt(proj, ng, batch, seq):
    chunk = MLSTM_CHUNK
    rows = batch * M_HEADS
    return pl.pallas_call(
        functools.partial(_mlstm_prompt_kernel, batch),
        grid=(seq // chunk,),
        in_specs=[
            pl.BlockSpec((batch, chunk, PROJ_W), lambda c: (0, c, 0)),
            pl.BlockSpec((1, V_W), lambda c: (0, 0)),
        ],
        out_specs=[
            pl.BlockSpec((batch, chunk, V_W), lambda c: (0, c, 0)),
            pl.BlockSpec((rows, M_QK_DIM, M_V_DIM), lambda c: (0, 0, 0)),
            pl.BlockSpec((rows, M_QK_DIM), lambda c: (0, 0)),
            pl.BlockSpec((rows, LANES), lambda c: (0, 0)),
        ],
        out_shape=[
            jax.ShapeDtypeStruct((batch, seq, V_W), BF16),
            jax.ShapeDtypeStruct((rows, M_QK_DIM, M_V_DIM), F32),
            jax.ShapeDtypeStruct((rows, M_QK_DIM), F32),
            jax.ShapeDtypeStruct((rows, LANES), F32),
        ],
        compiler_params=_params("arbitrary"),
        name="mlstm_prompt",
    )(proj.reshape(batch, seq, PROJ_W), ng)


def _mlstm_sample_kernel(proj_ref, ng_ref, c0_ref, n0_ref, m0_ref, h_ref, c_ref, n_ref, m_ref):
    for hd in range(M_HEADS):
        hn, c_new, n_new, m_new = _mlstm_head(
            lambda lo, hi: proj_ref[:, lo:hi], hd, ng_ref[:, hd * M_V_DIM:(hd + 1) * M_V_DIM],
            c0_ref[0, hd], n0_ref[0, hd:hd + 1, :], m0_ref[0, :, hd:hd + 1], F32)
        h_ref[:, hd * M_V_DIM:(hd + 1) * M_V_DIM] = hn
        c_ref[0, hd] = c_new
        n_ref[0, hd:hd + 1, :] = n_new
        m_ref[0, :, hd:hd + 1] = m_new


def _mlstm_sample(proj, ng, c0, n0, m0, t):
    db = c0.shape[0]
    m0 = m0.reshape(db, 1, M_HEADS)
    state_specs = [
        pl.BlockSpec((1, M_HEADS, M_QK_DIM, M_V_DIM), lambda b: (b, 0, 0, 0)),
        pl.BlockSpec((1, M_HEADS, M_QK_DIM), lambda b: (b, 0, 0)),
        pl.BlockSpec((1, 1, M_HEADS), lambda b: (b, 0, 0)),
    ]
    return pl.pallas_call(
        _mlstm_sample_kernel,
        grid=(db,),
        in_specs=[
            pl.BlockSpec((t, PROJ_W), lambda b: (b, 0)),
            pl.BlockSpec((1, V_W), lambda b: (0, 0)),
        ] + state_specs,
        out_specs=[pl.BlockSpec((t, V_W), lambda b: (b, 0))] + state_specs,
        out_shape=[
            jax.ShapeDtypeStruct((db * t, V_W), F32),
            jax.ShapeDtypeStruct(c0.shape, F32),
            jax.ShapeDtypeStruct(n0.shape, F32),
            jax.ShapeDtypeStruct(m0.shape, F32),
        ],
        compiler_params=_params("arbitrary"),
        name="mlstm_sample",
    )(proj, ng, c0, n0, m0)


def _conv3(u, u1, u2, w, b):
    return ((b + w[0:1] * u2) + w[1:2] * u1) + w[2:3] * u


def _conv_stream(u, tail, w, b):
    u1 = pltpu.roll(u, 1, 0)
    u2 = pltpu.roll(u, 2, 0)
    c = _conv3(u, u1, u2, w, b)
    uf = u[0:SUBLANES]
    row = lax.broadcasted_iota(jnp.int32, uf.shape, 0)
    uf1 = jnp.where(row < 1, pltpu.roll(tail, 1, 0), pltpu.roll(uf, 1, 0))
    uf2 = jnp.where(row < 2, pltpu.roll(tail, 2, 0), pltpu.roll(uf, 2, 0))
    cf = _conv3(uf, uf1, uf2, w, b)
    return jnp.concatenate([cf, c[SUBLANES:]], axis=0)


def _conv_seq8(u, prev, w, b):
    nseq = u.shape[0] // SUBLANES
    p0 = jnp.broadcast_to(prev[:, 0:1, :], (nseq, SUBLANES, u.shape[1])).reshape(u.shape)
    p1 = jnp.broadcast_to(prev[:, 1:2, :], (nseq, SUBLANES, u.shape[1])).reshape(u.shape)
    row = lax.broadcasted_iota(jnp.int32, u.shape, 0) & (SUBLANES - 1)
    u1 = jnp.where(row == 0, p1, pltpu.roll(u, 1, 0))
    u2 = jnp.where(row == 0, p0, jnp.where(row == 1, p1, pltpu.roll(u, 2, 0)))
    return _conv3(u, u1, u2, w, b)


def _ffn_kernel(stream, tiles_per_seq, x_ref, gpre_ref, wg_ref, wv_ref, cwg_ref, cwv_ref, cbg_ref,
                cbv_ref, wd_ref, gpost_ref, *rest):
    if stream:
        o_ref, sg_ref, sv_ref, xn_ref, tail_g, tail_v = rest
    else:
        pg_ref, pv_ref, o_ref, sg_ref, sv_ref, xn_ref = rest
    i = pl.program_id(0)
    j = pl.program_id(1)
    tm = x_ref.shape[0]

    @pl.when(j == 0)
    def _():
        xn_ref[...] = _rmsnorm(x_ref[...], gpre_ref[...]).astype(BF16)
        o_ref[...] = jnp.zeros(o_ref.shape, F32)

    if stream:
        @pl.when(i % tiles_per_seq == 0)
        def _():
            tail_g[j] = jnp.zeros(tail_g.shape[1:], F32)
            tail_v[j] = jnp.zeros(tail_v.shape[1:], F32)

    xn = xn_ref[...]
    ug = jnp.dot(xn, wg_ref[...], preferred_element_type=F32)
    uv = jnp.dot(xn, wv_ref[...], preferred_element_type=F32)
    if stream:
        cg = _conv_stream(ug, tail_g[j], cwg_ref[...], cbg_ref[...])
        cv = _conv_stream(uv, tail_v[j], cwv_ref[...], cbv_ref[...])
        tail_g[j] = ug[tm - SUBLANES:]
        tail_v[j] = uv[tm - SUBLANES:]
        sg_ref[0] = ug[tm - SUBLANES:]
        sv_ref[0] = uv[tm - SUBLANES:]
    else:
        cg = _conv_seq8(ug, pg_ref[...], cwg_ref[...], cbg_ref[...])
        cv = _conv_seq8(uv, pv_ref[...], cwv_ref[...], cbv_ref[...])
        sg_ref[...] = ug.reshape(sg_ref.shape)
        sv_ref[...] = uv.reshape(sv_ref.shape)
    h = (jax.nn.gelu(cg, approximate=True) * cv).astype(BF16)
    o_ref[...] += jnp.dot(h, wd_ref[...], preferred_element_type=F32)

    @pl.when(j == pl.num_programs(1) - 1)
    def _():
        o_ref[...] = x_ref[...] + _rmsnorm(o_ref[...], gpost_ref[...])


def _ffn(x, prev, layer, gpre, w_up, conv_w, conv_b, w_down, gpost, tm, tf, seq):
    m = x.shape[0]
    nf = D_FF // tf
    stream = prev is None
    in_specs = [
        pl.BlockSpec((tm, D_MODEL), lambda i, j: (i, 0), pipeline_mode=pl.Buffered(1 if tm > ROW_TILE else 2)),
        pl.BlockSpec((1, D_MODEL), lambda i, j: (0, 0)),
        pl.BlockSpec((None, D_MODEL, tf), lambda i, j: (layer, 0, j)),
        pl.BlockSpec((None, D_MODEL, tf), lambda i, j: (layer, 0, j + nf)),
        pl.BlockSpec((None, CONV_W, tf), lambda i, j: (layer, 0, j)),
        pl.BlockSpec((None, CONV_W, tf), lambda i, j: (layer, 0, j + nf)),
        pl.BlockSpec((None, 1, tf), lambda i, j: (layer, 0, j)),
        pl.BlockSpec((None, 1, tf), lambda i, j: (layer, 0, j + nf)),
        pl.BlockSpec((None, tf, D_MODEL), lambda i, j: (layer, j, 0)),
        pl.BlockSpec((1, D_MODEL), lambda i, j: (0, 0)),
    ]
    conv_b = conv_b.reshape(conv_b.shape[0], 1, 2 * D_FF)
    args = [x, gpre, w_up, w_up, conv_w, conv_w, conv_b, conv_b, w_down, gpost]
    scratch = [pltpu.VMEM((tm, D_MODEL), BF16)]
    if stream:
        tiles_per_seq = seq // tm
        nseq = m // tm
        state_spec = pl.BlockSpec((1, SUBLANES, tf), lambda i, j: (i, 0, j))
        scratch += [pltpu.VMEM((nf, SUBLANES, tf), F32), pltpu.VMEM((nf, SUBLANES, tf), F32)]
    else:
        assert seq == SUBLANES and m == tm
        tiles_per_seq = 1
        nseq = m // seq
        state_spec = pl.BlockSpec((nseq, SUBLANES, tf), lambda i, j: (0, 0, j))
        in_specs += [
            pl.BlockSpec((nseq, CONV_W - 1, tf), lambda i, j: (0, 0, j)),
            pl.BlockSpec((nseq, CONV_W - 1, tf), lambda i, j: (0, 0, j + nf)),
        ]
        args += [prev, prev]
    return pl.pallas_call(
        functools.partial(_ffn_kernel, stream, tiles_per_seq),
        grid=(m // tm, nf),
        in_specs=in_specs,
        out_specs=[pl.BlockSpec((tm, D_MODEL), lambda i, j: (i, 0)), state_spec, state_spec],
        out_shape=[
            jax.ShapeDtypeStruct((m, D_MODEL), F32),
            jax.ShapeDtypeStruct((nseq, SUBLANES, D_FF), F32),
            jax.ShapeDtypeStruct((nseq, SUBLANES, D_FF), F32),
        ],
        scratch_shapes=scratch,
        compiler_params=_params("arbitrary", "arbitrary"),
        name="conv_ffn",
    )(*args)


def _rope_tables(pos):
    half = HEAD_DIM // 2
    inv = ROPE_THETA ** (-jnp.arange(half, dtype=F32) / half)
    ang = pos.astype(F32)[:, None] * inv[None, :]
    cos = jnp.cos(ang)
    sin = jnp.sin(ang)
    reps = LANES // HEAD_DIM
    return (jnp.tile(cos, (1, 2 * reps)), jnp.tile(jnp.concatenate([-sin, sin], axis=1), (1, reps)))


def _conv_state(sg, sv, blocks_per_seq):
    last = slice(blocks_per_seq - 1, None, blocks_per_seq)
    keep = slice(SUBLANES - (CONV_W - 1), SUBLANES)
    return jnp.concatenate([sg[last, keep], sv[last, keep]], axis=-1)


def _trunk(x, pos, seq, tm, wide_tm, cache, state, state_conv, p):
    m = x.shape[0]
    nseq = m // seq
    sample = cache is not None
    row = lambda v: v.reshape(1, -1)

    cos, sin = _rope_tables(pos)
    q, kv = _qkv_rope(x, row(p["norm_mix_pre"][0]), p["w_qkv"], row(p["attn_b_qkv"][0]), cos, sin, tm,
                      F32 if sample else BF16)
    if sample:
        ck, cv = cache
        o, k_new, v_new = _attn_sample(p["attn_sinks"][0], q, kv, ck.reshape(nseq, WINDOW, KV_W),
                                       cv.reshape(nseq, WINDOW, KV_W), seq)
    else:
        o = _attn_prompt(p["attn_sinks"][0], q, kv, nseq, seq)
        kv3 = kv.reshape(nseq, seq, 2 * KV_W)
        k_new = kv3[:, seq - WINDOW:, :KV_W]
        v_new = kv3[:, seq - WINDOW:, KV_W:]
    k_new = k_new.reshape(1, nseq, WINDOW, N_KV_HEADS, HEAD_DIM)
    v_new = v_new.reshape(1, nseq, WINDOW, N_KV_HEADS, HEAD_DIM)
    x = _proj_post(o, p["w_o"], row(p["attn_b_o"][0]), x, row(p["norm_mix_post"][0]), tm)
    x, sg0, sv0 = _ffn(x, state_conv[0] if sample else None, 0, row(p["norm_ffn_pre"][0]), p["w_up"],
                       p["ffn_conv_w"], p["ffn_conv_b"], p["w_down"], row(p["norm_ffn_post"][0]),
                       wide_tm, FF_TILE, seq)

    proj = _norm_matmul(x, row(p["norm_mix_pre"][1]), p["w_in"], p["b_in"], wide_tm, PROJ_N_TILE)
    ng = row(p["mlstm_norm"][0])
    if sample:
        c0, n0, m0 = state
        h, c_new, n_new, m_new = _mlstm_sample(proj, ng, c0, n0, m0, seq)
    else:
        h, c_new, n_new, m_new = _mlstm_prompt(proj, ng, nseq, seq)
        h = h.reshape(m, V_W)
        m_new = m_new[:, 0]
    c_new = c_new.reshape(1, nseq, M_HEADS, M_QK_DIM, M_V_DIM)
    n_new = n_new.reshape(1, nseq, M_HEADS, M_QK_DIM)
    m_new = m_new.reshape(1, nseq, M_HEADS)
    x = _proj_post(h, p["w_out"], None, x, row(p["norm_mix_post"][1]), tm)
    tm1, tf1 = (wide_tm, FF_TILE) if sample else (wide_tm // 2, 2 * FF_TILE)
    x, sg1, sv1 = _ffn(x, state_conv[1] if sample else None, 1, row(p["norm_ffn_pre"][1]), p["w_up"],
                       p["ffn_conv_w"], p["ffn_conv_b"], p["w_down"], row(p["norm_ffn_post"][1]),
                       tm1, tf1, seq)
    bps0 = 1 if sample else seq // wide_tm
    bps1 = 1 if sample else seq // tm1
    conv = jnp.stack([_conv_state(sg0, sv0, bps0), _conv_state(sg1, sv1, bps1)])
    return x, k_new, v_new, c_new, n_new, m_new, conv


def kernel(x_prompt, x_sample, cache_k, cache_v, state_C, state_n, state_m, state_conv, norm_mix_pre,
           norm_mix_post, norm_ffn_pre, norm_ffn_post, attn_w_qkv, attn_b_qkv, attn_w_o, attn_b_o, attn_sinks,
           mlstm_w_in, mlstm_b_gates, mlstm_norm, mlstm_w_out, ffn_w_up, ffn_conv_w, ffn_conv_b, ffn_w_down):
    batch, seq, _ = x_prompt.shape
    dec_batch, dec_seq, _ = x_sample.shape
    n_gates = 2 * M_HEADS
    g0 = 2 * QK_W + 2 * V_W
    w_in = mlstm_w_in[0]
    p = dict(
        norm_mix_pre=norm_mix_pre, norm_mix_post=norm_mix_post, norm_ffn_pre=norm_ffn_pre,
        norm_ffn_post=norm_ffn_post, attn_b_qkv=attn_b_qkv, attn_b_o=attn_b_o, attn_sinks=attn_sinks,
        mlstm_norm=mlstm_norm, ffn_conv_w=ffn_conv_w, ffn_conv_b=ffn_conv_b,
        w_qkv=attn_w_qkv[0].astype(BF16),
        w_o=attn_w_o[0].astype(BF16),
        w_in=jnp.pad(w_in, ((0, 0), (0, GATE_PAD - n_gates))).astype(BF16),
        b_in=jnp.pad(mlstm_b_gates[0], (g0, GATE_PAD - n_gates)).reshape(1, PROJ_W),
        w_out=mlstm_w_out[0].astype(BF16),
        w_up=ffn_w_up.astype(BF16),
        w_down=ffn_w_down.astype(BF16),
    )
    yp, k_p, v_p, c_p, n_p, m_p, conv_p = _trunk(
        x_prompt.reshape(batch * seq, D_MODEL), jnp.arange(seq), seq, ROW_TILE, WIDE_ROW_TILE, None, None, None, p)
    ys, k_s, v_s, c_s, n_s, m_s, conv_s = _trunk(
        x_sample.reshape(dec_batch * dec_seq, D_MODEL),
        jnp.tile(PAST_LEN + jnp.arange(dec_seq), dec_batch), dec_seq, dec_batch * dec_seq, dec_batch * dec_seq,
        (cache_k[0], cache_v[0]), (state_C[0], state_n[0], state_m[0]), state_conv, p)
    return (yp.reshape(batch, seq, D_MODEL), ys.reshape(dec_batch, dec_seq, D_MODEL),
            k_p, v_p, k_s, v_s, c_p, n_p, m_p, c_s, n_s, m_s, conv_p, conv_s)
```

```python
import functools

import jax
import jax.numpy as jnp
from jax import lax
from jax.experimental import pallas as pl
from jax.experimental.pallas import tpu as pltpu

F32 = jnp.float32
BF16 = jnp.bfloat16

D_MODEL = 2048
WINDOW = 128
HEAD_DIM = 64
N_HEADS = 32
N_KV_HEADS = 4
GROUP = N_HEADS // N_KV_HEADS
Q_W = N_HEADS * HEAD_DIM
KV_W = N_KV_HEADS * HEAD_DIM
ROPE_THETA = 10000.0
M_HEADS = 4
M_QK_DIM = 256
M_V_DIM = 512
QK_W = M_HEADS * M_QK_DIM
V_W = M_HEADS * M_V_DIM
GATE_PAD = 256
PROJ_W = QK_W + 2 * V_W + GATE_PAD
D_FF = 4 * D_MODEL
CONV_W = 3
EPS = 1e-6
NEG_INF = -1e30
PAST_LEN = 16384

SUBLANES = 8
LANES = 128
VMEM_LIMIT_BYTES = 60 * 1024 * 1024

ROW_TILE = 512
WIDE_ROW_TILE = 1024
FF_TILE = 1024
PROJ_N_TILE = 896
MLSTM_CHUNK = 128


def _params(*sem):
    return pltpu.CompilerParams(dimension_semantics=sem, vmem_limit_bytes=VMEM_LIMIT_BYTES)


def _rmsnorm(xf, g):
    r = xf * lax.rsqrt(jnp.mean(xf * xf, axis=-1, keepdims=True) + EPS)
    return r * g


def _qkv_rope_kernel(x_ref, g_ref, w_ref, b_ref, cos_ref, sin_ref, q_ref, kv_ref):
    xn = _rmsnorm(x_ref[...], g_ref[...]).astype(BF16)
    y = jnp.dot(xn, w_ref[...], preferred_element_type=F32) + b_ref[...]
    cos = cos_ref[...]
    sin = sin_ref[...]
    lane = lax.broadcasted_iota(jnp.int32, cos.shape, 1)
    first_half = (lane & (HEAD_DIM - 1)) < (HEAD_DIM // 2)
    n_rot = (Q_W + KV_W) // LANES
    for c in range(n_rot):
        blk = y[:, c * LANES:(c + 1) * LANES]
        sw = jnp.where(first_half, pltpu.roll(blk, LANES - HEAD_DIM // 2, 1),
                       pltpu.roll(blk, HEAD_DIM // 2, 1))
        r = blk * cos + sw * sin
        if c < Q_W // LANES:
            q_ref[:, c * LANES:(c + 1) * LANES] = (r * (HEAD_DIM ** -0.5)).astype(q_ref.dtype)
        else:
            o = c * LANES - Q_W
            kv_ref[:, o:o + LANES] = r
    kv_ref[:, KV_W:2 * KV_W] = y[:, Q_W + KV_W:Q_W + 2 * KV_W]


def _qkv_rope(x, g, w, b, cos, sin, tm, q_dtype):
    m = x.shape[0]
    n = w.shape[1]
    n_pos_tiles = cos.shape[0] // tm
    return pl.pallas_call(
        _qkv_rope_kernel,
        grid=(m // tm,),
        in_specs=[
            pl.BlockSpec((tm, D_MODEL), lambda i: (i, 0)),
            pl.BlockSpec((1, D_MODEL), lambda i: (0, 0)),
            pl.BlockSpec((D_MODEL, n), lambda i: (0, 0)),
            pl.BlockSpec((1, n), lambda i: (0, 0)),
            pl.BlockSpec((tm, LANES), lambda i: (i % n_pos_tiles, 0)),
            pl.BlockSpec((tm, LANES), lambda i: (i % n_pos_tiles, 0)),
        ],
        out_specs=[
            pl.BlockSpec((tm, Q_W), lambda i: (i, 0)),
            pl.BlockSpec((tm, 2 * KV_W), lambda i: (i, 0)),
        ],
        out_shape=[
            jax.ShapeDtypeStruct((m, Q_W), q_dtype),
            jax.ShapeDtypeStruct((m, 2 * KV_W), F32),
        ],
        compiler_params=_params("arbitrary"),
        name="qkv_rope",
    )(x, g, w, b, cos, sin)


def _attn_prompt_kernel(sinks_ref, q_ref, kvp_ref, kvc_ref, o_ref):
    n = pl.program_id(1)
    w = WINDOW
    ri = lax.broadcasted_iota(jnp.int32, (w, 2 * w), 0)
    ci = lax.broadcasted_iota(jnp.int32, (w, 2 * w), 1)
    has_prev = jnp.full((w, 2 * w), n, jnp.int32) > 0
    allowed = ((ci < w) & (ci > ri) & has_prev) | ((ci >= w) & ((ci - w) <= ri))
    low = lax.broadcasted_iota(jnp.int32, (w, LANES), 1) < HEAD_DIM
    ones = jnp.ones((2 * w, HEAD_DIM), BF16)
    for g in range(N_KV_HEADS):
        ks = slice(g * HEAD_DIM, (g + 1) * HEAD_DIM)
        vs = slice(KV_W + g * HEAD_DIM, KV_W + (g + 1) * HEAD_DIM)
        k = jnp.concatenate([kvp_ref[:, ks], kvc_ref[:, ks]], axis=0).astype(BF16)
        v = jnp.concatenate([kvp_ref[:, vs], kvc_ref[:, vs]], axis=0).astype(BF16)
        v1 = jnp.concatenate([v, ones], axis=1)
        q = jnp.concatenate(
            [q_ref[:, (g * GROUP + h) * HEAD_DIM:(g * GROUP + h + 1) * HEAD_DIM] for h in range(GROUP)],
            axis=0)
        s = lax.dot_general(q, k, (((1,), (1,)), ((), ())), preferred_element_type=F32)
        ps, es = [], []
        for h in range(GROUP):
            sh = jnp.where(allowed, s[h * w:(h + 1) * w], NEG_INF)
            sink = sinks_ref[g * GROUP + h]
            m = jnp.maximum(jnp.max(sh, axis=-1, keepdims=True), sink)
            ps.append(jnp.exp(sh - m).astype(BF16))
            es.append(jnp.exp(sink - m))
        oa = jnp.dot(jnp.concatenate(ps, axis=0), v1, preferred_element_type=F32)
        for h in range(0, GROUP, 2):
            a = oa[h * w:(h + 1) * w]
            b = oa[(h + 1) * w:(h + 2) * w]
            num = jnp.where(low, a, pltpu.roll(b, HEAD_DIM, 1))
            den = jnp.where(low, pltpu.roll(a, HEAD_DIM, 1) + es[h], b + es[h + 1])
            c0 = (g * GROUP + h) * HEAD_DIM
            o_ref[:, c0:c0 + LANES] = (num / den).astype(o_ref.dtype)


def _attn_prompt(sinks, q, kv, batch, seq):
    nb = seq // WINDOW
    return pl.pallas_call(
        _attn_prompt_kernel,
        grid=(batch, nb),
        in_specs=[
            pl.BlockSpec(memory_space=pltpu.SMEM),
            pl.BlockSpec((WINDOW, Q_W), lambda b, n: (b * nb + n, 0)),
            pl.BlockSpec((WINDOW, 2 * KV_W), lambda b, n: (b * nb + jnp.maximum(n - 1, 0), 0)),
            pl.BlockSpec((WINDOW, 2 * KV_W), lambda b, n: (b * nb + n, 0)),
        ],
        out_specs=pl.BlockSpec((WINDOW, Q_W), lambda b, n: (b * nb + n, 0)),
        out_shape=jax.ShapeDtypeStruct((batch * seq, Q_W), BF16),
        compiler_params=_params("arbitrary", "arbitrary"),
        name="attn_prompt",
    )(sinks, q, kv, kv)


def _attn_sample_kernel(sinks_ref, q_ref, kvn_ref, ck_ref, cv_ref, o_ref, ko_ref, vo_ref):
    t = q_ref.shape[0]
    wc = ck_ref.shape[1]
    rows = GROUP * t
    ri = lax.broadcasted_iota(jnp.int32, (rows, wc), 0) & (t - 1)
    ci = lax.broadcasted_iota(jnp.int32, (rows, wc), 1)
    allowed_c = ci > ri
    ri_n = lax.broadcasted_iota(jnp.int32, (rows, t), 0) & (t - 1)
    ci_n = lax.broadcasted_iota(jnp.int32, (rows, t), 1)
    allowed_n = ci_n <= ri_n
    for g in range(N_KV_HEADS):
        ks = slice(g * HEAD_DIM, (g + 1) * HEAD_DIM)
        vs = slice(KV_W + g * HEAD_DIM, KV_W + (g + 1) * HEAD_DIM)
        kc = ck_ref[0, :, ks].astype(BF16)
        vc = cv_ref[0, :, ks].astype(BF16)
        kn = kvn_ref[:, ks]
        vn = kvn_ref[:, vs]
        q = jnp.concatenate(
            [q_ref[:, (g * GROUP + h) * HEAD_DIM:(g * GROUP + h + 1) * HEAD_DIM] for h in range(GROUP)],
            axis=0)
        s_c = lax.dot_general(q.astype(BF16), kc, (((1,), (1,)), ((), ())), preferred_element_type=F32)
        s_n = lax.dot_general(q, kn, (((1,), (1,)), ((), ())), preferred_element_type=F32)
        s_c = jnp.where(allowed_c, s_c, NEG_INF)
        s_n = jnp.where(allowed_n, s_n, NEG_INF)
        sink = jnp.concatenate([jnp.full((t, 1), sinks_ref[g * GROUP + h], F32) for h in range(GROUP)], axis=0)
        m = jnp.maximum(jnp.maximum(jnp.max(s_c, axis=-1, keepdims=True),
                                    jnp.max(s_n, axis=-1, keepdims=True)), sink)
        p_c = jnp.exp(s_c - m)
        p_n = jnp.exp(s_n - m)
        l = (jnp.sum(p_c, axis=-1, keepdims=True) + jnp.sum(p_n, axis=-1, keepdims=True)
             + jnp.exp(sink - m))
        o = (jnp.dot((p_c / l).astype(BF16), vc, preferred_element_type=F32)
             + jnp.dot(p_n / l, vn, preferred_element_type=F32))
        for h in range(GROUP):
            c0 = (g * GROUP + h) * HEAD_DIM
            o_ref[:, c0:c0 + HEAD_DIM] = o[h * t:(h + 1) * t]
    ko_ref[0, 0:wc - t, :] = ck_ref[0, t:wc, :]
    ko_ref[0, wc - t:wc, :] = kvn_ref[:, 0:KV_W]
    vo_ref[0, 0:wc - t, :] = cv_ref[0, t:wc, :]
    vo_ref[0, wc - t:wc, :] = kvn_ref[:, KV_W:2 * KV_W]


def _attn_sample(sinks, q, kvn, ck, cv, t):
    db = ck.shape[0]
    wc = ck.shape[1]
    return pl.pallas_call(
        _attn_sample_kernel,
        grid=(db,),
        in_specs=[
            pl.BlockSpec(memory_space=pltpu.SMEM),
            pl.BlockSpec((t, Q_W), lambda b: (b, 0)),
            pl.BlockSpec((t, 2 * KV_W), lambda b: (b, 0)),
            pl.BlockSpec((1, wc, KV_W), lambda b: (b, 0, 0)),
            pl.BlockSpec((1, wc, KV_W), lambda b: (b, 0, 0)),
        ],
        out_specs=[
            pl.BlockSpec((t, Q_W), lambda b: (b, 0)),
            pl.BlockSpec((1, wc, KV_W), lambda b: (b, 0, 0)),
            pl.BlockSpec((1, wc, KV_W), lambda b: (b, 0, 0)),
        ],
        out_shape=[
            jax.ShapeDtypeStruct((db * t, Q_W), F32),
            jax.ShapeDtypeStruct((db, wc, KV_W), F32),
            jax.ShapeDtypeStruct((db, wc, KV_W), F32),
        ],
        compiler_params=_params("arbitrary"),
        name="attn_sample",
    )(sinks, q, kvn, ck, cv)


def _proj_post_kernel(has_bias, a_ref, w_ref, b_ref, x_ref, g_ref, o_ref):
    y = jnp.dot(a_ref[...].astype(BF16), w_ref[...], preferred_element_type=F32)
    if has_bias:
        y = y + b_ref[...]
    o_ref[...] = x_ref[...] + _rmsnorm(y, g_ref[...])


def _proj_post(a, w, b, x, g, tm):
    m, k = a.shape
    has_bias = b is not None
    if b is None:
        b = jnp.zeros((1, D_MODEL), F32)
    return pl.pallas_call(
        functools.partial(_proj_post_kernel, has_bias),
        grid=(m // tm,),
        in_specs=[
            pl.BlockSpec((tm, k), lambda i: (i, 0)),
            pl.BlockSpec((k, D_MODEL), lambda i: (0, 0)),
            pl.BlockSpec((1, D_MODEL), lambda i: (0, 0)),
            pl.BlockSpec((tm, D_MODEL), lambda i: (i, 0)),
            pl.BlockSpec((1, D_MODEL), lambda i: (0, 0)),
        ],
        out_specs=pl.BlockSpec((tm, D_MODEL), lambda i: (i, 0)),
        out_shape=jax.ShapeDtypeStruct((m, D_MODEL), F32),
        compiler_params=_params("arbitrary"),
        name="proj_post",
    )(a, w, b, x, g)


def _norm_matmul_kernel(has_bias, x_ref, g_ref, w_ref, b_ref, o_ref, xn_ref):
    @pl.when(pl.program_id(1) == 0)
    def _():
        xn_ref[...] = _rmsnorm(x_ref[...], g_ref[...]).astype(BF16)

    y = jnp.dot(xn_ref[...], w_ref[...], preferred_element_type=F32)
    if has_bias:
        y = y + b_ref[...]
    o_ref[...] = y


def _norm_matmul(x, g, w, b, tm, tn):
    m = x.shape[0]
    n = w.shape[1]
    has_bias = b is not None
    if b is None:
        b = jnp.zeros((1, n), F32)
    return pl.pallas_call(
        functools.partial(_norm_matmul_kernel, has_bias),
        grid=(m // tm, n // tn),
        in_specs=[
            pl.BlockSpec((tm, D_MODEL), lambda i, j: (i, 0)),
            pl.BlockSpec((1, D_MODEL), lambda i, j: (0, 0)),
            pl.BlockSpec((D_MODEL, tn), lambda i, j: (0, j)),
            pl.BlockSpec((1, tn), lambda i, j: (0, j)),
        ],
        out_specs=pl.BlockSpec((tm, tn), lambda i, j: (i, j)),
        out_shape=jax.ShapeDtypeStruct((m, n), F32),
        scratch_shapes=[pltpu.VMEM((tm, D_MODEL), BF16)],
        compiler_params=_params("arbitrary", "arbitrary"),
        name="norm_matmul",
    )(x, g, w, b)


def _norm_matmul_t_kernel(x_ref, g_ref, wt_ref, o_ref):
    xn = _rmsnorm(x_ref[...], g_ref[...]).astype(BF16)
    o_ref[...] = lax.dot_general(wt_ref[...], xn, (((1,), (1,)), ((), ())), preferred_element_type=F32)


def _norm_matmul_t(x, g, wt, tm):
    m = x.shape[0]
    n = wt.shape[0]
    return pl.pallas_call(
        _norm_matmul_t_kernel,
        grid=(m // tm,),
        in_specs=[
            pl.BlockSpec((tm, D_MODEL), lambda i: (i, 0)),
            pl.BlockSpec((1, D_MODEL), lambda i: (0, 0)),
            pl.BlockSpec((n, D_MODEL), lambda i: (0, 0)),
        ],
        out_specs=pl.BlockSpec((n, tm), lambda i: (0, i)),
        out_shape=jax.ShapeDtypeStruct((n, m), F32),
        compiler_params=_params("arbitrary"),
        name="norm_matmul_t",
    )(x, g, wt)


def _log_sigmoid(x):
    return jnp.minimum(x, 0.0) - jnp.log1p(jnp.exp(-jnp.abs(x)))


def _chunk_gates(li, lf, m_state):
    L = li.shape[0]
    ri = lax.broadcasted_iota(jnp.int32, (L, L), 0)
    ci = lax.broadcasted_iota(jnp.int32, (L, L), 1)
    eye = ri == ci
    tril = ci <= ri
    lf_row = jnp.sum(jnp.where(eye, lf, 0.0), axis=0, keepdims=True)
    li_row = jnp.sum(jnp.where(eye, li, 0.0), axis=0, keepdims=True)
    b_col = jnp.sum(jnp.where(tril, lf_row, 0.0), axis=1, keepdims=True)
    b_row = jnp.sum(jnp.where(ri <= ci, lf, 0.0), axis=0, keepdims=True)
    dmat = jnp.where(tril, b_col - b_row + li_row, -jnp.inf)
    inter = b_col + m_state
    m_row = jnp.maximum(inter, jnp.max(dmat, axis=1, keepdims=True))
    b_last = b_col[L - 1:L, :]
    gk_col = b_last - b_col + li
    gk_row = b_last - b_row + li_row
    m_new = jnp.maximum(b_last + m_state, jnp.max(gk_col, axis=0, keepdims=True))
    decay = jnp.exp(b_last + m_state - m_new)
    return jnp.exp(dmat - m_row), jnp.exp(inter - m_row), m_row, gk_col, gk_row, m_new, decay


def _mlstm_chunk(q, k, v, li, lf, c_state, n_state, m_state):
    dexp, w_inter, m_row, gk_col, _, m_new, decay = _chunk_gates(li, lf, m_state)
    s = lax.dot_general(q, k, (((1,), (1,)), ((), ())), preferred_element_type=F32)
    sm = s * dexp
    num = (jnp.dot(sm, v, preferred_element_type=F32)
           + w_inter * jnp.dot(q, c_state, preferred_element_type=F32))
    den = jnp.sum(sm, axis=1, keepdims=True) + w_inter * jnp.sum(q * n_state, axis=1, keepdims=True)
    h = num * (1.0 / jnp.maximum(jnp.abs(den), jnp.exp(-m_row)))
    kw = k * jnp.exp(gk_col - m_new)
    c_new = decay * c_state + lax.dot_general(kw, v, (((0,), (0,)), ((), ())), preferred_element_type=F32)
    n_new = decay * n_state + jnp.sum(kw, axis=0, keepdims=True)
    return h, c_new, n_new, m_new


def _mlstm_chunk_t(q, kt, v1, li, lf, c_aug, m_state):
    dk = q.shape[1]
    dv = v1.shape[1] - LANES
    scale = dk ** -0.5
    dexp, w_inter, m_row, _, gk_row, m_new, decay = _chunk_gates(li, lf, m_state)
    qm = q.astype(BF16)
    s = jnp.dot(qm, kt.astype(BF16), preferred_element_type=F32)
    sm = (s * scale) * dexp
    num = (jnp.dot(sm.astype(BF16), v1, preferred_element_type=F32)
           + (w_inter * scale) * jnp.dot(qm, c_aug.astype(BF16), preferred_element_type=F32))
    den = num[:, dv:dv + 1]
    h = num[:, :dv] * (1.0 / jnp.maximum(jnp.abs(den), jnp.exp(-m_row)))
    kwt = (kt * jnp.exp(gk_row - m_new)).astype(BF16)
    c_new = decay * c_aug + jnp.dot(kwt, v1, preferred_element_type=F32)
    return h, c_new, m_new


def _head_cols(hd):
    q0 = hd * M_QK_DIM
    v0 = QK_W + hd * M_V_DIM
    o0 = QK_W + V_W + hd * M_V_DIM
    g0 = QK_W + 2 * V_W
    return (q0, q0 + M_QK_DIM), (v0, v0 + M_V_DIM), (o0, o0 + M_V_DIM), g0 + hd, g0 + M_HEADS + hd


def _head_out(h, o, ng):
    return _rmsnorm(h, ng) * (0.5 * (jnp.tanh(0.5 * o) + 1.0))


def _mlstm_prompt_kernel(batch, proj_ref, *rest):
    kt_refs = rest[:batch]
    ng_ref, h_ref, c_ref, m_ref = rest[batch:]

    @pl.when(pl.program_id(0) == 0)
    def _():
        c_ref[...] = jnp.zeros(c_ref.shape, F32)
        m_ref[...] = jnp.zeros(m_ref.shape, F32)

    L = proj_ref.shape[1]
    one_col = (lax.broadcasted_iota(jnp.int32, (L, LANES), 1) == 0).astype(BF16)
    for b in range(batch):
        for hd in range(M_HEADS):
            r = b * M_HEADS + hd
            (q0, q1), (v0, v1), (o0, o1), gi, gf = _head_cols(hd)
            va = jnp.concatenate([proj_ref[b, :, v0:v1].astype(BF16), one_col], axis=1)
            h, c_new, m_new = _mlstm_chunk_t(
                proj_ref[b, :, q0:q1], kt_refs[b][hd * M_QK_DIM:(hd + 1) * M_QK_DIM, :], va,
                proj_ref[b, :, gi:gi + 1], _log_sigmoid(proj_ref[b, :, gf:gf + 1]),
                c_ref[r], m_ref[r:r + 1, 0:1])
            hn = _head_out(h, proj_ref[b, :, o0:o1], ng_ref[:, hd * M_V_DIM:(hd + 1) * M_V_DIM])
            h_ref[b, :, hd * M_V_DIM:(hd + 1) * M_V_DIM] = hn.astype(h_ref.dtype)
            c_ref[r] = c_new
            m_ref[r:r + 1, :] = jnp.broadcast_to(m_new, (1, LANES))


def _mlstm_prompt(proj, kt, ng, batch, seq):
    chunk = MLSTM_CHUNK
    nc = seq // chunk
    rows = batch * M_HEADS
    return pl.pallas_call(
        functools.partial(_mlstm_prompt_kernel, batch),
        grid=(nc,),
        in_specs=[pl.BlockSpec((batch, chunk, PROJ_W), lambda c: (0, c, 0))]
        + [pl.BlockSpec((QK_W, chunk), lambda c, b=b: (0, b * nc + c)) for b in range(batch)]
        + [pl.BlockSpec((1, V_W), lambda c: (0, 0))],
        out_specs=[
            pl.BlockSpec((batch, chunk, V_W), lambda c: (0, c, 0)),
            pl.BlockSpec((rows, M_QK_DIM, M_V_DIM + LANES), lambda c: (0, 0, 0)),
            pl.BlockSpec((rows, LANES), lambda c: (0, 0)),
        ],
        out_shape=[
            jax.ShapeDtypeStruct((batch, seq, V_W), BF16),
            jax.ShapeDtypeStruct((rows, M_QK_DIM, M_V_DIM + LANES), F32),
            jax.ShapeDtypeStruct((rows, LANES), F32),
        ],
        compiler_params=_params("arbitrary"),
        name="mlstm_prompt",
    )(proj.reshape(batch, seq, PROJ_W), *([kt] * batch), ng)


def _mlstm_sample_kernel(proj_ref, k_ref, ng_ref, c0_ref, n0_ref, m0_ref, h_ref, c_ref, n_ref, m_ref):
    for hd in range(M_HEADS):
        (q0, q1), (v0, v1), (o0, o1), gi, gf = _head_cols(hd)
        h, c_new, n_new, m_new = _mlstm_chunk(
            proj_ref[:, q0:q1] * (M_QK_DIM ** -0.5), k_ref[:, hd * M_QK_DIM:(hd + 1) * M_QK_DIM],
            proj_ref[:, v0:v1], proj_ref[:, gi:gi + 1], _log_sigmoid(proj_ref[:, gf:gf + 1]),
            c0_ref[0, hd], n0_ref[0, hd:hd + 1, :], m0_ref[0, :, hd:hd + 1])
        h_ref[:, hd * M_V_DIM:(hd + 1) * M_V_DIM] = _head_out(
            h, proj_ref[:, o0:o1], ng_ref[:, hd * M_V_DIM:(hd + 1) * M_V_DIM])
        c_ref[0, hd] = c_new
        n_ref[0, hd:hd + 1, :] = n_new
        m_ref[0, :, hd:hd + 1] = m_new


def _mlstm_sample(proj, k, ng, c0, n0, m0, t):
    db = c0.shape[0]
    m0 = m0.reshape(db, 1, M_HEADS)
    state_specs = [
        pl.BlockSpec((1, M_HEADS, M_QK_DIM, M_V_DIM), lambda b: (b, 0, 0, 0)),
        pl.BlockSpec((1, M_HEADS, M_QK_DIM), lambda b: (b, 0, 0)),
        pl.BlockSpec((1, 1, M_HEADS), lambda b: (b, 0, 0)),
    ]
    return pl.pallas_call(
        _mlstm_sample_kernel,
        grid=(db,),
        in_specs=[
            pl.BlockSpec((t, PROJ_W), lambda b: (b, 0)),
            pl.BlockSpec((t, QK_W), lambda b: (b, 0)),
            pl.BlockSpec((1, V_W), lambda b: (0, 0)),
        ] + state_specs,
        out_specs=[pl.BlockSpec((t, V_W), lambda b: (b, 0))] + state_specs,
        out_shape=[
            jax.ShapeDtypeStruct((db * t, V_W), F32),
            jax.ShapeDtypeStruct(c0.shape, F32),
            jax.ShapeDtypeStruct(n0.shape, F32),
            jax.ShapeDtypeStruct(m0.shape, F32),
        ],
        compiler_params=_params("arbitrary"),
        name="mlstm_sample",
    )(proj, k, ng, c0, n0, m0)


def _conv3(u, u1, u2, w, b):
    return ((b + w[0:1] * u2) + w[1:2] * u1) + w[2:3] * u


def _conv_stream(u, tail, w, b):
    u1 = pltpu.roll(u, 1, 0)
    u2 = pltpu.roll(u, 2, 0)
    c = _conv3(u, u1, u2, w, b)
    uf = u[0:SUBLANES]
    row = lax.broadcasted_iota(jnp.int32, uf.shape, 0)
    uf1 = jnp.where(row < 1, pltpu.roll(tail, 1, 0), pltpu.roll(uf, 1, 0))
    uf2 = jnp.where(row < 2, pltpu.roll(tail, 2, 0), pltpu.roll(uf, 2, 0))
    cf = _conv3(uf, uf1, uf2, w, b)
    return jnp.concatenate([cf, c[SUBLANES:]], axis=0)


def _conv_seq8(u, prev, w, b):
    nseq = u.shape[0] // SUBLANES
    p0 = jnp.broadcast_to(prev[:, 0:1, :], (nseq, SUBLANES, u.shape[1])).reshape(u.shape)
    p1 = jnp.broadcast_to(prev[:, 1:2, :], (nseq, SUBLANES, u.shape[1])).reshape(u.shape)
    row = lax.broadcasted_iota(jnp.int32, u.shape, 0) & (SUBLANES - 1)
    u1 = jnp.where(row == 0, p1, pltpu.roll(u, 1, 0))
    u2 = jnp.where(row == 0, p0, jnp.where(row == 1, p1, pltpu.roll(u, 2, 0)))
    return _conv3(u, u1, u2, w, b)


def _ffn_kernel(stream, tiles_per_seq, x_ref, gpre_ref, wg_ref, wv_ref, cwg_ref, cwv_ref, cbg_ref,
                cbv_ref, wd_ref, gpost_ref, *rest):
    if stream:
        o_ref, sg_ref, sv_ref, xn_ref, tail_g, tail_v = rest
    else:
        pg_ref, pv_ref, o_ref, sg_ref, sv_ref, xn_ref = rest
    i = pl.program_id(0)
    j = pl.program_id(1)
    tm = x_ref.shape[0]

    @pl.when(j == 0)
    def _():
        xn_ref[...] = _rmsnorm(x_ref[...], gpre_ref[...]).astype(BF16)
        o_ref[...] = jnp.zeros(o_ref.shape, F32)

    if stream:
        @pl.when(i % tiles_per_seq == 0)
        def _():
            tail_g[j] = jnp.zeros(tail_g.shape[1:], F32)
            tail_v[j] = jnp.zeros(tail_v.shape[1:], F32)

    xn = xn_ref[...]
    ug = jnp.dot(xn, wg_ref[...], preferred_element_type=F32)
    uv = jnp.dot(xn, wv_ref[...], preferred_element_type=F32)
    if stream:
        cg = _conv_stream(ug, tail_g[j], cwg_ref[...], cbg_ref[...])
        cv = _conv_stream(uv, tail_v[j], cwv_ref[...], cbv_ref[...])
        tail_g[j] = ug[tm - SUBLANES:]
        tail_v[j] = uv[tm - SUBLANES:]
        sg_ref[0] = ug[tm - SUBLANES:]
        sv_ref[0] = uv[tm - SUBLANES:]
    else:
        cg = _conv_seq8(ug, pg_ref[...], cwg_ref[...], cbg_ref[...])
        cv = _conv_seq8(uv, pv_ref[...], cwv_ref[...], cbv_ref[...])
        sg_ref[...] = ug.reshape(sg_ref.shape)
        sv_ref[...] = uv.reshape(sv_ref.shape)
    h = (jax.nn.gelu(cg, approximate=True) * cv).astype(BF16)
    o_ref[...] += jnp.dot(h, wd_ref[...], preferred_element_type=F32)

    @pl.when(j == pl.num_programs(1) - 1)
    def _():
        o_ref[...] = x_ref[...] + _rmsnorm(o_ref[...], gpost_ref[...])


def _ffn(x, prev, layer, gpre, w_up, conv_w, conv_b, w_down, gpost, tm, tf, seq):
    m = x.shape[0]
    nf = D_FF // tf
    stream = prev is None
    in_specs = [
        pl.BlockSpec((tm, D_MODEL), lambda i, j: (i, 0)),
        pl.BlockSpec((1, D_MODEL), lambda i, j: (0, 0)),
        pl.BlockSpec((None, D_MODEL, tf), lambda i, j: (layer, 0, j)),
        pl.BlockSpec((None, D_MODEL, tf), lambda i, j: (layer, 0, j + nf)),
        pl.BlockSpec((None, CONV_W, tf), lambda i, j: (layer, 0, j)),
        pl.BlockSpec((None, CONV_W, tf), lambda i, j: (layer, 0, j + nf)),
        pl.BlockSpec((None, 1, tf), lambda i, j: (layer, 0, j)),
        pl.BlockSpec((None, 1, tf), lambda i, j: (layer, 0, j + nf)),
        pl.BlockSpec((None, tf, D_MODEL), lambda i, j: (layer, j, 0)),
        pl.BlockSpec((1, D_MODEL), lambda i, j: (0, 0)),
    ]
    conv_b = conv_b.reshape(conv_b.shape[0], 1, 2 * D_FF)
    args = [x, gpre, w_up, w_up, conv_w, conv_w, conv_b, conv_b, w_down, gpost]
    scratch = [pltpu.VMEM((tm, D_MODEL), BF16)]
    if stream:
        tiles_per_seq = seq // tm
        nseq = m // tm
        state_spec = pl.BlockSpec((1, SUBLANES, tf), lambda i, j: (i, 0, j))
        scratch += [pltpu.VMEM((nf, SUBLANES, tf), F32), pltpu.VMEM((nf, SUBLANES, tf), F32)]
    else:
        assert seq == SUBLANES and m == tm
        tiles_per_seq = 1
        nseq = m // seq
        state_spec = pl.BlockSpec((nseq, SUBLANES, tf), lambda i, j: (0, 0, j))
        in_specs += [
            pl.BlockSpec((nseq, CONV_W - 1, tf), lambda i, j: (0, 0, j)),
            pl.BlockSpec((nseq, CONV_W - 1, tf), lambda i, j: (0, 0, j + nf)),
        ]
        args += [prev, prev]
    return pl.pallas_call(
        functools.partial(_ffn_kernel, stream, tiles_per_seq),
        grid=(m // tm, nf),
        in_specs=in_specs,
        out_specs=[pl.BlockSpec((tm, D_MODEL), lambda i, j: (i, 0)), state_spec, state_spec],
        out_shape=[
            jax.ShapeDtypeStruct((m, D_MODEL), F32),
            jax.ShapeDtypeStruct((nseq, SUBLANES, D_FF), F32),
            jax.ShapeDtypeStruct((nseq, SUBLANES, D_FF), F32),
        ],
        scratch_shapes=scratch,
        compiler_params=_params("arbitrary", "arbitrary"),
        name="conv_ffn",
    )(*args)


def _rope_tables(pos):
    half = HEAD_DIM // 2
    inv = ROPE_THETA ** (-jnp.arange(half, dtype=F32) / half)
    ang = pos.astype(F32)[:, None] * inv[None, :]
    cos = jnp.cos(ang)
    sin = jnp.sin(ang)
    reps = LANES // HEAD_DIM
    return (jnp.tile(cos, (1, 2 * reps)), jnp.tile(jnp.concatenate([-sin, sin], axis=1), (1, reps)))


def _conv_state(sg, sv, blocks_per_seq):
    last = slice(blocks_per_seq - 1, None, blocks_per_seq)
    keep = slice(SUBLANES - (CONV_W - 1), SUBLANES)
    return jnp.concatenate([sg[last, keep], sv[last, keep]], axis=-1)


def _trunk(x, pos, seq, tm, wide_tm, cache, state, state_conv, p):
    m = x.shape[0]
    nseq = m // seq
    sample = cache is not None
    row = lambda v: v.reshape(1, -1)

    cos, sin = _rope_tables(pos)
    q, kv = _qkv_rope(x, row(p["norm_mix_pre"][0]), p["w_qkv"], row(p["attn_b_qkv"][0]), cos, sin, tm,
                      F32 if sample else BF16)
    if sample:
        ck, cv = cache
        o, k_new, v_new = _attn_sample(p["attn_sinks"][0], q, kv, ck.reshape(nseq, WINDOW, KV_W),
                                       cv.reshape(nseq, WINDOW, KV_W), seq)
    else:
        o = _attn_prompt(p["attn_sinks"][0], q, kv, nseq, seq)
        kv3 = kv.reshape(nseq, seq, 2 * KV_W)
        k_new = kv3[:, seq - WINDOW:, :KV_W]
        v_new = kv3[:, seq - WINDOW:, KV_W:]
    k_new = k_new.reshape(1, nseq, WINDOW, N_KV_HEADS, HEAD_DIM)
    v_new = v_new.reshape(1, nseq, WINDOW, N_KV_HEADS, HEAD_DIM)
    x = _proj_post(o, p["w_o"], row(p["attn_b_o"][0]), x, row(p["norm_mix_post"][0]), tm)
    x, sg0, sv0 = _ffn(x, state_conv[0] if sample else None, 0, row(p["norm_ffn_pre"][0]), p["w_up"],
                       p["ffn_conv_w"], p["ffn_conv_b"], p["w_down"], row(p["norm_ffn_post"][0]),
                       tm, FF_TILE, seq)

    g1 = row(p["norm_mix_pre"][1])
    proj = _norm_matmul(x, g1, p["w_in"], p["b_in"], wide_tm, PROJ_N_TILE)
    ng = row(p["mlstm_norm"][0])
    if sample:
        c0, n0, m0 = state
        k = _norm_matmul(x, g1, p["w_k"], None, wide_tm, QK_W)
        h, c_new, n_new, m_new = _mlstm_sample(proj, k, ng, c0, n0, m0, seq)
    else:
        kt = _norm_matmul_t(x, g1, p["w_k"].T, wide_tm)
        h, c_aug, m_new = _mlstm_prompt(proj, kt, ng, nseq, seq)
        h = h.reshape(m, V_W)
        c_new = c_aug[:, :, :M_V_DIM]
        n_new = c_aug[:, :, M_V_DIM]
        m_new = m_new[:, 0]
    c_new = c_new.reshape(1, nseq, M_HEADS, M_QK_DIM, M_V_DIM)
    n_new = n_new.reshape(1, nseq, M_HEADS, M_QK_DIM)
    m_new = m_new.reshape(1, nseq, M_HEADS)
    x = _proj_post(h, p["w_out"], None, x, row(p["norm_mix_post"][1]), tm)
    x, sg1, sv1 = _ffn(x, state_conv[1] if sample else None, 1, row(p["norm_ffn_pre"][1]), p["w_up"],
                       p["ffn_conv_w"], p["ffn_conv_b"], p["w_down"], row(p["norm_ffn_post"][1]),
                       tm, FF_TILE, seq)
    bps = 1 if sample else seq // tm
    conv = jnp.stack([_conv_state(sg0, sv0, bps), _conv_state(sg1, sv1, bps)])
    return x, k_new, v_new, c_new, n_new, m_new, conv


def kernel(x_prompt, x_sample, cache_k, cache_v, state_C, state_n, state_m, state_conv, norm_mix_pre,
           norm_mix_post, norm_ffn_pre, norm_ffn_post, attn_w_qkv, attn_b_qkv, attn_w_o, attn_b_o, attn_sinks,
           mlstm_w_in, mlstm_b_gates, mlstm_norm, mlstm_w_out, ffn_w_up, ffn_conv_w, ffn_conv_b, ffn_w_down):
    batch, seq, _ = x_prompt.shape
    dec_batch, dec_seq, _ = x_sample.shape
    n_gates = 2 * M_HEADS
    w_in = mlstm_w_in[0]
    kv0 = 2 * QK_W
    w_main = jnp.concatenate(
        [w_in[:, :QK_W], w_in[:, kv0:kv0 + 2 * V_W],
         jnp.pad(w_in[:, kv0 + 2 * V_W:], ((0, 0), (0, GATE_PAD - n_gates)))], axis=1)
    p = dict(
        norm_mix_pre=norm_mix_pre, norm_mix_post=norm_mix_post, norm_ffn_pre=norm_ffn_pre,
        norm_ffn_post=norm_ffn_post, attn_b_qkv=attn_b_qkv, attn_b_o=attn_b_o, attn_sinks=attn_sinks,
        mlstm_norm=mlstm_norm, ffn_conv_w=ffn_conv_w, ffn_conv_b=ffn_conv_b,
        w_qkv=attn_w_qkv[0].astype(BF16),
        w_o=attn_w_o[0].astype(BF16),
        w_in=w_main.astype(BF16),
        w_k=w_in[:, QK_W:kv0].astype(BF16),
        b_in=jnp.pad(mlstm_b_gates[0], (QK_W + 2 * V_W, GATE_PAD - n_gates)).reshape(1, PROJ_W),
        w_out=mlstm_w_out[0].astype(BF16),
        w_up=ffn_w_up.astype(BF16),
        w_down=ffn_w_down.astype(BF16),
    )
    yp, k_p, v_p, c_p, n_p, m_p, conv_p = _trunk(
        x_prompt.reshape(batch * seq, D_MODEL), jnp.arange(seq), seq, ROW_TILE, WIDE_ROW_TILE, None, None, None, p)
    ys, k_s, v_s, c_s, n_s, m_s, conv_s = _trunk(
        x_sample.reshape(dec_batch * dec_seq, D_MODEL),
        jnp.tile(PAST_LEN + jnp.arange(dec_seq), dec_batch), dec_seq, dec_batch * dec_seq, dec_batch * dec_seq,
        (cache_k[0], cache_v[0]), (state_C[0], state_n[0], state_m[0]), state_conv, p)
    return (yp.reshape(batch, seq, D_MODEL), ys.reshape(dec_batch, dec_seq, D_MODEL),
            k_p, v_p, k_s, v_s, c_p, n_p, m_p, c_s, n_s, m_s, conv_p, conv_s)
```

```python
import functools

import jax
import jax.numpy as jnp
from jax import lax
from jax.experimental import pallas as pl
from jax.experimental.pallas import tpu as pltpu

F32 = jnp.float32
BF16 = jnp.bfloat16

D_MODEL = 2048
WINDOW = 128
HEAD_DIM = 64
N_HEADS = 32
N_KV_HEADS = 4
GROUP = N_HEADS // N_KV_HEADS
Q_W = N_HEADS * HEAD_DIM
KV_W = N_KV_HEADS * HEAD_DIM
ROPE_THETA = 10000.0
M_HEADS = 4
M_QK_DIM = 256
M_V_DIM = 512
QK_W = M_HEADS * M_QK_DIM
V_W = M_HEADS * M_V_DIM
GATE_PAD = 256
PROJ_W = QK_W + 2 * V_W + GATE_PAD
D_FF = 4 * D_MODEL
CONV_W = 3
EPS = 1e-6
NEG_INF = -1e30
PAST_LEN = 16384

SUBLANES = 8
LANES = 128
VMEM_LIMIT_BYTES = 60 * 1024 * 1024

ROW_TILE = 512
WIDE_ROW_TILE = 1024
FF_TILE = 1024
PROJ_N_TILE = 896
MLSTM_CHUNK = 256


def _params(*sem):
    return pltpu.CompilerParams(dimension_semantics=sem, vmem_limit_bytes=VMEM_LIMIT_BYTES)


def _rmsnorm(xf, g):
    r = xf * lax.rsqrt(jnp.mean(xf * xf, axis=-1, keepdims=True) + EPS)
    return r * g


def _qkv_rope_kernel(x_ref, g_ref, w_ref, b_ref, cos_ref, sin_ref, q_ref, kv_ref):
    xn = _rmsnorm(x_ref[...], g_ref[...]).astype(BF16)
    y = jnp.dot(xn, w_ref[...], preferred_element_type=F32) + b_ref[...]
    cos = cos_ref[...]
    sin = sin_ref[...]
    lane = lax.broadcasted_iota(jnp.int32, cos.shape, 1)
    first_half = (lane & (HEAD_DIM - 1)) < (HEAD_DIM // 2)
    n_rot = (Q_W + KV_W) // LANES
    for c in range(n_rot):
        blk = y[:, c * LANES:(c + 1) * LANES]
        sw = jnp.where(first_half, pltpu.roll(blk, LANES - HEAD_DIM // 2, 1),
                       pltpu.roll(blk, HEAD_DIM // 2, 1))
        r = blk * cos + sw * sin
        if c < Q_W // LANES:
            q_ref[:, c * LANES:(c + 1) * LANES] = (r * (HEAD_DIM ** -0.5)).astype(q_ref.dtype)
        else:
            o = c * LANES - Q_W
            kv_ref[:, o:o + LANES] = r
    kv_ref[:, KV_W:2 * KV_W] = y[:, Q_W + KV_W:Q_W + 2 * KV_W]


def _qkv_rope(x, g, w, b, cos, sin, tm, q_dtype):
    m = x.shape[0]
    n = w.shape[1]
    n_pos_tiles = cos.shape[0] // tm
    return pl.pallas_call(
        _qkv_rope_kernel,
        grid=(m // tm,),
        in_specs=[
            pl.BlockSpec((tm, D_MODEL), lambda i: (i, 0)),
            pl.BlockSpec((1, D_MODEL), lambda i: (0, 0)),
            pl.BlockSpec((D_MODEL, n), lambda i: (0, 0)),
            pl.BlockSpec((1, n), lambda i: (0, 0)),
            pl.BlockSpec((tm, LANES), lambda i: (i % n_pos_tiles, 0)),
            pl.BlockSpec((tm, LANES), lambda i: (i % n_pos_tiles, 0)),
        ],
        out_specs=[
            pl.BlockSpec((tm, Q_W), lambda i: (i, 0)),
            pl.BlockSpec((tm, 2 * KV_W), lambda i: (i, 0)),
        ],
        out_shape=[
            jax.ShapeDtypeStruct((m, Q_W), q_dtype),
            jax.ShapeDtypeStruct((m, 2 * KV_W), F32),
        ],
        compiler_params=_params("arbitrary"),
        name="qkv_rope",
    )(x, g, w, b, cos, sin)


def _attn_prompt_kernel(sinks_ref, q_ref, kvp_ref, kvc_ref, o_ref):
    n = pl.program_id(1)
    w = WINDOW
    ri = lax.broadcasted_iota(jnp.int32, (w, 2 * w), 0)
    ci = lax.broadcasted_iota(jnp.int32, (w, 2 * w), 1)
    has_prev = jnp.full((w, 2 * w), n, jnp.int32) > 0
    allowed = ((ci < w) & (ci > ri) & has_prev) | ((ci >= w) & ((ci - w) <= ri))
    low = lax.broadcasted_iota(jnp.int32, (w, LANES), 1) < HEAD_DIM
    ones = jnp.ones((2 * w, HEAD_DIM), BF16)
    for g in range(N_KV_HEADS):
        ks = slice(g * HEAD_DIM, (g + 1) * HEAD_DIM)
        vs = slice(KV_W + g * HEAD_DIM, KV_W + (g + 1) * HEAD_DIM)
        k = jnp.concatenate([kvp_ref[:, ks], kvc_ref[:, ks]], axis=0).astype(BF16)
        v = jnp.concatenate([kvp_ref[:, vs], kvc_ref[:, vs]], axis=0).astype(BF16)
        v1 = jnp.concatenate([v, ones], axis=1)
        q = jnp.concatenate(
            [q_ref[:, (g * GROUP + h) * HEAD_DIM:(g * GROUP + h + 1) * HEAD_DIM] for h in range(GROUP)],
            axis=0)
        s = lax.dot_general(q, k, (((1,), (1,)), ((), ())), preferred_element_type=F32)
        ps, es = [], []
        for h in range(GROUP):
            sh = jnp.where(allowed, s[h * w:(h + 1) * w], NEG_INF)
            sink = sinks_ref[g * GROUP + h]
            m = jnp.maximum(jnp.max(sh, axis=-1, keepdims=True), sink)
            ps.append(jnp.exp(sh - m).astype(BF16))
            es.append(jnp.exp(sink - m))
        oa = jnp.dot(jnp.concatenate(ps, axis=0), v1, preferred_element_type=F32)
        for h in range(0, GROUP, 2):
            a = oa[h * w:(h + 1) * w]
            b = oa[(h + 1) * w:(h + 2) * w]
            num = jnp.where(low, a, pltpu.roll(b, HEAD_DIM, 1))
            den = jnp.where(low, pltpu.roll(a, HEAD_DIM, 1) + es[h], b + es[h + 1])
            c0 = (g * GROUP + h) * HEAD_DIM
            o_ref[:, c0:c0 + LANES] = (num / den).astype(o_ref.dtype)


def _attn_prompt(sinks, q, kv, batch, seq):
    nb = seq // WINDOW
    return pl.pallas_call(
        _attn_prompt_kernel,
        grid=(batch, nb),
        in_specs=[
            pl.BlockSpec(memory_space=pltpu.SMEM),
            pl.BlockSpec((WINDOW, Q_W), lambda b, n: (b * nb + n, 0)),
            pl.BlockSpec((WINDOW, 2 * KV_W), lambda b, n: (b * nb + jnp.maximum(n - 1, 0), 0)),
            pl.BlockSpec((WINDOW, 2 * KV_W), lambda b, n: (b * nb + n, 0)),
        ],
        out_specs=pl.BlockSpec((WINDOW, Q_W), lambda b, n: (b * nb + n, 0)),
        out_shape=jax.ShapeDtypeStruct((batch * seq, Q_W), BF16),
        compiler_params=_params("arbitrary", "arbitrary"),
        name="attn_prompt",
    )(sinks, q, kv, kv)


def _attn_sample_kernel(sinks_ref, q_ref, kvn_ref, ck_ref, cv_ref, o_ref, ko_ref, vo_ref):
    t = q_ref.shape[0]
    wc = ck_ref.shape[1]
    rows = N_HEADS * t
    ri = lax.broadcasted_iota(jnp.int32, (rows, wc), 0) & (t - 1)
    ci = lax.broadcasted_iota(jnp.int32, (rows, wc), 1)
    allowed_c = ci > ri
    ri_n = lax.broadcasted_iota(jnp.int32, (rows, t), 0) & (t - 1)
    ci_n = lax.broadcasted_iota(jnp.int32, (rows, t), 1)
    allowed_n = ci_n <= ri_n
    nt = (((1,), (1,)), ((), ()))
    s_c, s_n = [], []
    for g in range(N_KV_HEADS):
        ks = slice(g * HEAD_DIM, (g + 1) * HEAD_DIM)
        q = jnp.concatenate(
            [q_ref[:, (g * GROUP + h) * HEAD_DIM:(g * GROUP + h + 1) * HEAD_DIM] for h in range(GROUP)],
            axis=0)
        s_c.append(lax.dot_general(q.astype(BF16), ck_ref[0, :, ks].astype(BF16), nt, preferred_element_type=F32))
        s_n.append(lax.dot_general(q, kvn_ref[:, ks], nt, preferred_element_type=F32))
    s_c = jnp.where(allowed_c, jnp.concatenate(s_c, axis=0), NEG_INF)
    s_n = jnp.where(allowed_n, jnp.concatenate(s_n, axis=0), NEG_INF)
    sink = jnp.concatenate([jnp.full((t, 1), sinks_ref[h], F32) for h in range(N_HEADS)], axis=0)
    m = jnp.maximum(jnp.maximum(jnp.max(s_c, axis=-1, keepdims=True),
                                jnp.max(s_n, axis=-1, keepdims=True)), sink)
    p_c = jnp.exp(s_c - m)
    p_n = jnp.exp(s_n - m)
    inv_l = 1.0 / (jnp.sum(p_c, axis=-1, keepdims=True) + jnp.sum(p_n, axis=-1, keepdims=True)
                   + jnp.exp(sink - m))
    p_c = (p_c * inv_l).astype(BF16)
    p_n = p_n * inv_l
    gr = GROUP * t
    for g in range(N_KV_HEADS):
        ks = slice(g * HEAD_DIM, (g + 1) * HEAD_DIM)
        vs = slice(KV_W + g * HEAD_DIM, KV_W + (g + 1) * HEAD_DIM)
        o = (jnp.dot(p_c[g * gr:(g + 1) * gr], cv_ref[0, :, ks].astype(BF16), preferred_element_type=F32)
             + jnp.dot(p_n[g * gr:(g + 1) * gr], kvn_ref[:, vs], preferred_element_type=F32))
        for h in range(GROUP):
            c0 = (g * GROUP + h) * HEAD_DIM
            o_ref[:, c0:c0 + HEAD_DIM] = o[h * t:(h + 1) * t]
    ko_ref[0, 0:wc - t, :] = ck_ref[0, t:wc, :]
    ko_ref[0, wc - t:wc, :] = kvn_ref[:, 0:KV_W]
    vo_ref[0, 0:wc - t, :] = cv_ref[0, t:wc, :]
    vo_ref[0, wc - t:wc, :] = kvn_ref[:, KV_W:2 * KV_W]


def _attn_sample(sinks, q, kvn, ck, cv, t):
    db = ck.shape[0]
    wc = ck.shape[1]
    return pl.pallas_call(
        _attn_sample_kernel,
        grid=(db,),
        in_specs=[
            pl.BlockSpec(memory_space=pltpu.SMEM),
            pl.BlockSpec((t, Q_W), lambda b: (b, 0)),
            pl.BlockSpec((t, 2 * KV_W), lambda b: (b, 0)),
            pl.BlockSpec((1, wc, KV_W), lambda b: (b, 0, 0)),
            pl.BlockSpec((1, wc, KV_W), lambda b: (b, 0, 0)),
        ],
        out_specs=[
            pl.BlockSpec((t, Q_W), lambda b: (b, 0)),
            pl.BlockSpec((1, wc, KV_W), lambda b: (b, 0, 0)),
            pl.BlockSpec((1, wc, KV_W), lambda b: (b, 0, 0)),
        ],
        out_shape=[
            jax.ShapeDtypeStruct((db * t, Q_W), F32),
            jax.ShapeDtypeStruct((db, wc, KV_W), F32),
            jax.ShapeDtypeStruct((db, wc, KV_W), F32),
        ],
        compiler_params=_params("arbitrary"),
        name="attn_sample",
    )(sinks, q, kvn, ck, cv)


def _proj_post_kernel(has_bias, a_ref, w_ref, b_ref, x_ref, g_ref, gn_ref, o_ref, xn_ref):
    y = jnp.dot(a_ref[...].astype(BF16), w_ref[...], preferred_element_type=F32)
    if has_bias:
        y = y + b_ref[...]
    x_new = x_ref[...] + _rmsnorm(y, g_ref[...])
    o_ref[...] = x_new
    xn_ref[...] = _rmsnorm(x_new, gn_ref[...]).astype(xn_ref.dtype)


def _proj_post(a, w, b, x, g, g_next, tm):
    m, k = a.shape
    has_bias = b is not None
    if b is None:
        b = jnp.zeros((1, D_MODEL), F32)
    return pl.pallas_call(
        functools.partial(_proj_post_kernel, has_bias),
        grid=(m // tm,),
        in_specs=[
            pl.BlockSpec((tm, k), lambda i: (i, 0)),
            pl.BlockSpec((k, D_MODEL), lambda i: (0, 0)),
            pl.BlockSpec((1, D_MODEL), lambda i: (0, 0)),
            pl.BlockSpec((tm, D_MODEL), lambda i: (i, 0)),
            pl.BlockSpec((1, D_MODEL), lambda i: (0, 0)),
            pl.BlockSpec((1, D_MODEL), lambda i: (0, 0)),
        ],
        out_specs=[pl.BlockSpec((tm, D_MODEL), lambda i: (i, 0)), pl.BlockSpec((tm, D_MODEL), lambda i: (i, 0))],
        out_shape=[jax.ShapeDtypeStruct((m, D_MODEL), F32), jax.ShapeDtypeStruct((m, D_MODEL), BF16)],
        compiler_params=_params("arbitrary"),
        name="proj_post",
    )(a, w, b, x, g, g_next)


def _norm_matmul_kernel(has_bias, x_ref, g_ref, w_ref, b_ref, o_ref, xn_ref):
    @pl.when(pl.program_id(1) == 0)
    def _():
        xn_ref[...] = _rmsnorm(x_ref[...], g_ref[...]).astype(BF16)

    y = jnp.dot(xn_ref[...], w_ref[...], preferred_element_type=F32)
    if has_bias:
        y = y + b_ref[...]
    o_ref[...] = y


def _norm_matmul(x, g, w, b, tm, tn):
    m = x.shape[0]
    n = w.shape[1]
    has_bias = b is not None
    if b is None:
        b = jnp.zeros((1, n), F32)
    return pl.pallas_call(
        functools.partial(_norm_matmul_kernel, has_bias),
        grid=(m // tm, n // tn),
        in_specs=[
            pl.BlockSpec((tm, D_MODEL), lambda i, j: (i, 0)),
            pl.BlockSpec((1, D_MODEL), lambda i, j: (0, 0)),
            pl.BlockSpec((D_MODEL, tn), lambda i, j: (0, j)),
            pl.BlockSpec((1, tn), lambda i, j: (0, j)),
        ],
        out_specs=pl.BlockSpec((tm, tn), lambda i, j: (i, j)),
        out_shape=jax.ShapeDtypeStruct((m, n), F32),
        scratch_shapes=[pltpu.VMEM((tm, D_MODEL), BF16)],
        compiler_params=_params("arbitrary", "arbitrary"),
        name="norm_matmul",
    )(x, g, w, b)


def _norm_matmul_t_kernel(x_ref, g_ref, wt_ref, o_ref):
    xn = _rmsnorm(x_ref[...], g_ref[...]).astype(BF16)
    o_ref[...] = lax.dot_general(wt_ref[...], xn, (((1,), (1,)), ((), ())), preferred_element_type=F32)


def _norm_matmul_t(x, g, wt, tm):
    m = x.shape[0]
    n = wt.shape[0]
    return pl.pallas_call(
        _norm_matmul_t_kernel,
        grid=(m // tm,),
        in_specs=[
            pl.BlockSpec((tm, D_MODEL), lambda i: (i, 0)),
            pl.BlockSpec((1, D_MODEL), lambda i: (0, 0)),
            pl.BlockSpec((n, D_MODEL), lambda i: (0, 0)),
        ],
        out_specs=pl.BlockSpec((n, tm), lambda i: (0, i)),
        out_shape=jax.ShapeDtypeStruct((n, m), F32),
        compiler_params=_params("arbitrary"),
        name="norm_matmul_t",
    )(x, g, wt)


def _log_sigmoid(x):
    return jnp.minimum(x, 0.0) - jnp.log1p(jnp.exp(-jnp.abs(x)))


def _chunk_gates(li, lf, m_state):
    L = li.shape[0]
    ri = lax.broadcasted_iota(jnp.int32, (L, L), 0)
    ci = lax.broadcasted_iota(jnp.int32, (L, L), 1)
    eye = ri == ci
    tril = ci <= ri
    lf_row = jnp.sum(jnp.where(eye, lf, 0.0), axis=0, keepdims=True)
    li_row = jnp.sum(jnp.where(eye, li, 0.0), axis=0, keepdims=True)
    b_col = jnp.sum(jnp.where(tril, lf_row, 0.0), axis=1, keepdims=True)
    b_row = jnp.sum(jnp.where(ri <= ci, lf, 0.0), axis=0, keepdims=True)
    dmat = jnp.where(tril, b_col - b_row + li_row, -jnp.inf)
    inter = b_col + m_state
    m_row = jnp.maximum(inter, jnp.max(dmat, axis=1, keepdims=True))
    b_last = b_col[L - 1:L, :]
    gk_col = b_last - b_col + li
    gk_row = b_last - b_row + li_row
    m_new = jnp.maximum(b_last + m_state, jnp.max(gk_col, axis=0, keepdims=True))
    decay = jnp.exp(b_last + m_state - m_new)
    return jnp.exp(dmat - m_row), jnp.exp(inter - m_row), m_row, gk_col, gk_row, m_new, decay


def _mlstm_chunk(q, k, v, li, lf, c_state, n_state, m_state):
    dexp, w_inter, m_row, gk_col, _, m_new, decay = _chunk_gates(li, lf, m_state)
    s = lax.dot_general(q, k, (((1,), (1,)), ((), ())), preferred_element_type=F32)
    sm = s * dexp
    num = (jnp.dot(sm, v, preferred_element_type=F32)
           + w_inter * jnp.dot(q, c_state, preferred_element_type=F32))
    den = jnp.sum(sm, axis=1, keepdims=True) + w_inter * jnp.sum(q * n_state, axis=1, keepdims=True)
    h = num * (1.0 / jnp.maximum(jnp.abs(den), jnp.exp(-m_row)))
    kw = k * jnp.exp(gk_col - m_new)
    c_new = decay * c_state + lax.dot_general(kw, v, (((0,), (0,)), ((), ())), preferred_element_type=F32)
    n_new = decay * n_state + jnp.sum(kw, axis=0, keepdims=True)
    return h, c_new, n_new, m_new


def _mlstm_chunk_t(q, kt, v1, li, lf, c_aug, m_state):
    dk = q.shape[1]
    dv = v1.shape[1] - LANES
    scale = dk ** -0.5
    dexp, w_inter, m_row, _, gk_row, m_new, decay = _chunk_gates(li, lf, m_state)
    qm = q.astype(BF16)
    s = jnp.dot(qm, kt.astype(BF16), preferred_element_type=F32)
    sm = (s * scale) * dexp
    num = (jnp.dot(sm.astype(BF16), v1, preferred_element_type=F32)
           + (w_inter * scale) * jnp.dot(qm, c_aug.astype(BF16), preferred_element_type=F32))
    den = num[:, dv:dv + 1]
    h = num[:, :dv] * (1.0 / jnp.maximum(jnp.abs(den), jnp.exp(-m_row)))
    kwt = (kt * jnp.exp(gk_row - m_new)).astype(BF16)
    c_new = decay * c_aug + jnp.dot(kwt, v1, preferred_element_type=F32)
    return h, c_new, m_new


def _head_cols(hd):
    q0 = hd * M_QK_DIM
    v0 = QK_W + hd * M_V_DIM
    o0 = QK_W + V_W + hd * M_V_DIM
    g0 = QK_W + 2 * V_W
    return (q0, q0 + M_QK_DIM), (v0, v0 + M_V_DIM), (o0, o0 + M_V_DIM), g0 + hd, g0 + M_HEADS + hd


def _head_out(h, o, ng):
    return _rmsnorm(h, ng) * (0.5 * (jnp.tanh(0.5 * o) + 1.0))


def _mlstm_prompt_kernel(batch, proj_ref, *rest):
    kt_refs = rest[:batch]
    ng_ref, h_ref, c_ref, m_ref = rest[batch:]

    @pl.when(pl.program_id(0) == 0)
    def _():
        c_ref[...] = jnp.zeros(c_ref.shape, F32)
        m_ref[...] = jnp.zeros(m_ref.shape, F32)

    L = proj_ref.shape[1]
    one_col = (lax.broadcasted_iota(jnp.int32, (L, LANES), 1) == 0).astype(BF16)
    for b in range(batch):
        for hd in range(M_HEADS):
            r = b * M_HEADS + hd
            (q0, q1), (v0, v1), (o0, o1), gi, gf = _head_cols(hd)
            va = jnp.concatenate([proj_ref[b, :, v0:v1].astype(BF16), one_col], axis=1)
            h, c_new, m_new = _mlstm_chunk_t(
                proj_ref[b, :, q0:q1], kt_refs[b][hd * M_QK_DIM:(hd + 1) * M_QK_DIM, :], va,
                proj_ref[b, :, gi:gi + 1], _log_sigmoid(proj_ref[b, :, gf:gf + 1]),
                c_ref[r], m_ref[r:r + 1, 0:1])
            hn = _head_out(h, proj_ref[b, :, o0:o1], ng_ref[:, hd * M_V_DIM:(hd + 1) * M_V_DIM])
            h_ref[b, :, hd * M_V_DIM:(hd + 1) * M_V_DIM] = hn.astype(h_ref.dtype)
            c_ref[r] = c_new
            m_ref[r:r + 1, :] = jnp.broadcast_to(m_new, (1, LANES))


def _mlstm_prompt(proj, kt, ng, batch, seq):
    chunk = MLSTM_CHUNK
    nc = seq // chunk
    rows = batch * M_HEADS
    return pl.pallas_call(
        functools.partial(_mlstm_prompt_kernel, batch),
        grid=(nc,),
        in_specs=[pl.BlockSpec((batch, chunk, PROJ_W), lambda c: (0, c, 0))]
        + [pl.BlockSpec((QK_W, chunk), lambda c, b=b: (0, b * nc + c)) for b in range(batch)]
        + [pl.BlockSpec((1, V_W), lambda c: (0, 0))],
        out_specs=[
            pl.BlockSpec((batch, chunk, V_W), lambda c: (0, c, 0)),
            pl.BlockSpec((rows, M_QK_DIM, M_V_DIM + LANES), lambda c: (0, 0, 0)),
            pl.BlockSpec((rows, LANES), lambda c: (0, 0)),
        ],
        out_shape=[
            jax.ShapeDtypeStruct((batch, seq, V_W), BF16),
            jax.ShapeDtypeStruct((rows, M_QK_DIM, M_V_DIM + LANES), F32),
            jax.ShapeDtypeStruct((rows, LANES), F32),
        ],
        compiler_params=_params("arbitrary"),
        name="mlstm_prompt",
    )(proj.reshape(batch, seq, PROJ_W), *([kt] * batch), ng)


def _mlstm_sample_kernel(proj_ref, k_ref, ng_ref, c0_ref, n0_ref, m0_ref, h_ref, c_ref, n_ref, m_ref):
    for hd in range(M_HEADS):
        (q0, q1), (v0, v1), (o0, o1), gi, gf = _head_cols(hd)
        h, c_new, n_new, m_new = _mlstm_chunk(
            proj_ref[:, q0:q1] * (M_QK_DIM ** -0.5), k_ref[:, hd * M_QK_DIM:(hd + 1) * M_QK_DIM],
            proj_ref[:, v0:v1], proj_ref[:, gi:gi + 1], _log_sigmoid(proj_ref[:, gf:gf + 1]),
            c0_ref[0, hd], n0_ref[0, hd:hd + 1, :], m0_ref[0, :, hd:hd + 1])
        h_ref[:, hd * M_V_DIM:(hd + 1) * M_V_DIM] = _head_out(
            h, proj_ref[:, o0:o1], ng_ref[:, hd * M_V_DIM:(hd + 1) * M_V_DIM])
        c_ref[0, hd] = c_new
        n_ref[0, hd:hd + 1, :] = n_new
        m_ref[0, :, hd:hd + 1] = m_new


def _mlstm_sample(proj, k, ng, c0, n0, m0, t):
    db = c0.shape[0]
    m0 = m0.reshape(db, 1, M_HEADS)
    state_specs = [
        pl.BlockSpec((1, M_HEADS, M_QK_DIM, M_V_DIM), lambda b: (b, 0, 0, 0)),
        pl.BlockSpec((1, M_HEADS, M_QK_DIM), lambda b: (b, 0, 0)),
        pl.BlockSpec((1, 1, M_HEADS), lambda b: (b, 0, 0)),
    ]
    return pl.pallas_call(
        _mlstm_sample_kernel,
        grid=(db,),
        in_specs=[
            pl.BlockSpec((t, PROJ_W), lambda b: (b, 0)),
            pl.BlockSpec((t, QK_W), lambda b: (b, 0)),
            pl.BlockSpec((1, V_W), lambda b: (0, 0)),
        ] + state_specs,
        out_specs=[pl.BlockSpec((t, V_W), lambda b: (b, 0))] + state_specs,
        out_shape=[
            jax.ShapeDtypeStruct((db * t, V_W), F32),
            jax.ShapeDtypeStruct(c0.shape, F32),
            jax.ShapeDtypeStruct(n0.shape, F32),
            jax.ShapeDtypeStruct(m0.shape, F32),
        ],
        compiler_params=_params("arbitrary"),
        name="mlstm_sample",
    )(proj, k, ng, c0, n0, m0)


def _conv3(u, u1, u2, w, b):
    return ((b + w[0:1] * u2) + w[1:2] * u1) + w[2:3] * u


def _conv_stream(u, tail, w, b):
    u1 = pltpu.roll(u, 1, 0)
    u2 = pltpu.roll(u, 2, 0)
    c = _conv3(u, u1, u2, w, b)
    uf = u[0:SUBLANES]
    row = lax.broadcasted_iota(jnp.int32, uf.shape, 0)
    uf1 = jnp.where(row < 1, pltpu.roll(tail, 1, 0), pltpu.roll(uf, 1, 0))
    uf2 = jnp.where(row < 2, pltpu.roll(tail, 2, 0), pltpu.roll(uf, 2, 0))
    cf = _conv3(uf, uf1, uf2, w, b)
    return jnp.concatenate([cf, c[SUBLANES:]], axis=0)


def _conv_seq8(u, prev, w, b):
    nseq = u.shape[0] // SUBLANES
    p0 = jnp.broadcast_to(prev[:, 0:1, :], (nseq, SUBLANES, u.shape[1])).reshape(u.shape)
    p1 = jnp.broadcast_to(prev[:, 1:2, :], (nseq, SUBLANES, u.shape[1])).reshape(u.shape)
    row = lax.broadcasted_iota(jnp.int32, u.shape, 0) & (SUBLANES - 1)
    u1 = jnp.where(row == 0, p1, pltpu.roll(u, 1, 0))
    u2 = jnp.where(row == 0, p0, jnp.where(row == 1, p1, pltpu.roll(u, 2, 0)))
    return _conv3(u, u1, u2, w, b)


def _ffn_kernel(stream, tiles_per_seq, x_ref, xn_ref, wg_ref, wv_ref, cw_ref, cb_ref, wd_ref, gpost_ref, *rest):
    if stream:
        o_ref, sg_ref, sv_ref, tail_g, tail_v = rest
    else:
        pg_ref, pv_ref, o_ref, sg_ref, sv_ref = rest
    i = pl.program_id(0)
    j = pl.program_id(1)
    nf = pl.num_programs(1)
    tm = x_ref.shape[0]

    if stream:
        @pl.when(i % tiles_per_seq == 0)
        def _():
            tail_g[j] = jnp.zeros(tail_g.shape[1:], F32)
            tail_v[j] = jnp.zeros(tail_v.shape[1:], F32)

    xn = xn_ref[...]
    ug = jnp.dot(xn, wg_ref[...], preferred_element_type=F32)
    uv = jnp.dot(xn, wv_ref[...], preferred_element_type=F32)
    cwg, cwv, cbg, cbv = cw_ref[j], cw_ref[j + nf], cb_ref[j], cb_ref[j + nf]
    if stream:
        cg = _conv_stream(ug, tail_g[j], cwg, cbg)
        cv = _conv_stream(uv, tail_v[j], cwv, cbv)
        tail_g[j] = ug[tm - SUBLANES:]
        tail_v[j] = uv[tm - SUBLANES:]
        sg_ref[0, j] = ug[tm - SUBLANES:]
        sv_ref[0, j] = uv[tm - SUBLANES:]
    else:
        cg = _conv_seq8(ug, pg_ref[...], cwg, cbg)
        cv = _conv_seq8(uv, pv_ref[...], cwv, cbv)
        sg_ref[...] = ug.reshape(sg_ref.shape)
        sv_ref[...] = uv.reshape(sv_ref.shape)
    h = (jax.nn.gelu(cg, approximate=True) * cv).astype(BF16)
    acc = jnp.where(j == 0, 0.0, o_ref[...])
    o_ref[...] = acc + jnp.dot(h, wd_ref[...], preferred_element_type=F32)

    @pl.when(j == nf - 1)
    def _():
        o_ref[...] = x_ref[...] + _rmsnorm(o_ref[...], gpost_ref[...])


def _ffn(x, xn, prev, layer, w_up, conv_w, conv_b, w_down, gpost, tm, tf, seq):
    m = x.shape[0]
    nf = D_FF // tf
    stream = prev is None
    n_layers = conv_w.shape[0]
    conv_w = conv_w.reshape(n_layers, CONV_W, 2 * nf, tf).transpose(0, 2, 1, 3)
    conv_b = conv_b.reshape(n_layers, 2 * nf, 1, tf)
    in_specs = [
        pl.BlockSpec((tm, D_MODEL), lambda i, j: (i, 0)),
        pl.BlockSpec((tm, D_MODEL), lambda i, j: (i, 0)),
        pl.BlockSpec((None, D_MODEL, tf), lambda i, j: (layer, 0, j)),
        pl.BlockSpec((None, D_MODEL, tf), lambda i, j: (layer, 0, j + nf)),
        pl.BlockSpec((None, 2 * nf, CONV_W, tf), lambda i, j: (layer, 0, 0, 0)),
        pl.BlockSpec((None, 2 * nf, 1, tf), lambda i, j: (layer, 0, 0, 0)),
        pl.BlockSpec((None, tf, D_MODEL), lambda i, j: (layer, j, 0)),
        pl.BlockSpec((1, D_MODEL), lambda i, j: (0, 0)),
    ]
    args = [x, xn, w_up, w_up, conv_w, conv_b, w_down, gpost]
    scratch = []
    if stream:
        tiles_per_seq = seq // tm
        state_shape = (m // tm, nf, SUBLANES, tf)
        state_spec = pl.BlockSpec((1, nf, SUBLANES, tf), lambda i, j: (i, 0, 0, 0))
        scratch += [pltpu.VMEM((nf, SUBLANES, tf), F32), pltpu.VMEM((nf, SUBLANES, tf), F32)]
    else:
        assert seq == SUBLANES and m == tm
        tiles_per_seq = 1
        nseq = m // seq
        state_shape = (nseq, SUBLANES, D_FF)
        state_spec = pl.BlockSpec((nseq, SUBLANES, tf), lambda i, j: (0, 0, j))
        in_specs += [
            pl.BlockSpec((nseq, CONV_W - 1, tf), lambda i, j: (0, 0, j)),
            pl.BlockSpec((nseq, CONV_W - 1, tf), lambda i, j: (0, 0, j + nf)),
        ]
        args += [prev, prev]
    return pl.pallas_call(
        functools.partial(_ffn_kernel, stream, tiles_per_seq),
        grid=(m // tm, nf),
        in_specs=in_specs,
        out_specs=[pl.BlockSpec((tm, D_MODEL), lambda i, j: (i, 0)), state_spec, state_spec],
        out_shape=[
            jax.ShapeDtypeStruct((m, D_MODEL), F32),
            jax.ShapeDtypeStruct(state_shape, F32),
            jax.ShapeDtypeStruct(state_shape, F32),
        ],
        scratch_shapes=scratch,
        compiler_params=_params("arbitrary", "arbitrary"),
        name="conv_ffn",
    )(*args)


def _rope_tables(pos):
    half = HEAD_DIM // 2
    inv = ROPE_THETA ** (-jnp.arange(half, dtype=F32) / half)
    ang = pos.astype(F32)[:, None] * inv[None, :]
    cos = jnp.cos(ang)
    sin = jnp.sin(ang)
    reps = LANES // HEAD_DIM
    return (jnp.tile(cos, (1, 2 * reps)), jnp.tile(jnp.concatenate([-sin, sin], axis=1), (1, reps)))


def _conv_state(sg, sv, tiles_per_seq):
    keep = slice(SUBLANES - (CONV_W - 1), SUBLANES)
    if tiles_per_seq is None:
        return jnp.concatenate([sg[:, keep], sv[:, keep]], axis=-1)
    last = slice(tiles_per_seq - 1, None, tiles_per_seq)

    def rows(s):
        s = s[last, :, keep, :]
        return jnp.swapaxes(s, 1, 2).reshape(s.shape[0], CONV_W - 1, -1)

    return jnp.concatenate([rows(sg), rows(sv)], axis=-1)


def _trunk(x, pos, seq, tm, wide_tm, cache, state, state_conv, p):
    m = x.shape[0]
    nseq = m // seq
    sample = cache is not None
    row = lambda v: v.reshape(1, -1)

    cos, sin = _rope_tables(pos)
    q, kv = _qkv_rope(x, row(p["norm_mix_pre"][0]), p["w_qkv"], row(p["attn_b_qkv"][0]), cos, sin, tm,
                      F32 if sample else BF16)
    if sample:
        ck, cv = cache
        o, k_new, v_new = _attn_sample(p["attn_sinks"][0], q, kv, ck.reshape(nseq, WINDOW, KV_W),
                                       cv.reshape(nseq, WINDOW, KV_W), seq)
    else:
        o = _attn_prompt(p["attn_sinks"][0], q, kv, nseq, seq)
        kv3 = kv.reshape(nseq, seq, 2 * KV_W)
        k_new = kv3[:, seq - WINDOW:, :KV_W]
        v_new = kv3[:, seq - WINDOW:, KV_W:]
    k_new = k_new.reshape(1, nseq, WINDOW, N_KV_HEADS, HEAD_DIM)
    v_new = v_new.reshape(1, nseq, WINDOW, N_KV_HEADS, HEAD_DIM)
    x, xn = _proj_post(o, p["w_o"], row(p["attn_b_o"][0]), x, row(p["norm_mix_post"][0]),
                       row(p["norm_ffn_pre"][0]), tm)
    x, sg0, sv0 = _ffn(x, xn, state_conv[0] if sample else None, 0, p["w_up"], p["ffn_conv_w"], p["ffn_conv_b"],
                       p["w_down"], row(p["norm_ffn_post"][0]), tm, FF_TILE, seq)

    g1 = row(p["norm_mix_pre"][1])
    proj = _norm_matmul(x, g1, p["w_in"], p["b_in"], wide_tm, PROJ_N_TILE)
    ng = row(p["mlstm_norm"][0])
    if sample:
        c0, n0, m0 = state
        k = _norm_matmul(x, g1, p["w_k"], None, wide_tm, QK_W)
        h, c_new, n_new, m_new = _mlstm_sample(proj, k, ng, c0, n0, m0, seq)
    else:
        kt = _norm_matmul_t(x, g1, p["w_k"].T, wide_tm)
        h, c_aug, m_new = _mlstm_prompt(proj, kt, ng, nseq, seq)
        h = h.reshape(m, V_W)
        c_new = c_aug[:, :, :M_V_DIM]
        n_new = c_aug[:, :, M_V_DIM]
        m_new = m_new[:, 0]
    c_new = c_new.reshape(1, nseq, M_HEADS, M_QK_DIM, M_V_DIM)
    n_new = n_new.reshape(1, nseq, M_HEADS, M_QK_DIM)
    m_new = m_new.reshape(1, nseq, M_HEADS)
    x, xn = _proj_post(h, p["w_out"], None, x, row(p["norm_mix_post"][1]), row(p["norm_ffn_pre"][1]), tm)
    x, sg1, sv1 = _ffn(x, xn, state_conv[1] if sample else None, 1, p["w_up"], p["ffn_conv_w"], p["ffn_conv_b"],
                       p["w_down"], row(p["norm_ffn_post"][1]), tm, FF_TILE, seq)
    tps = None if sample else seq // tm
    conv = jnp.stack([_conv_state(sg0, sv0, tps), _conv_state(sg1, sv1, tps)])
    return x, k_new, v_new, c_new, n_new, m_new, conv


def kernel(x_prompt, x_sample, cache_k, cache_v, state_C, state_n, state_m, state_conv, norm_mix_pre,
           norm_mix_post, norm_ffn_pre, norm_ffn_post, attn_w_qkv, attn_b_qkv, attn_w_o, attn_b_o, attn_sinks,
           mlstm_w_in, mlstm_b_gates, mlstm_norm, mlstm_w_out, ffn_w_up, ffn_conv_w, ffn_conv_b, ffn_w_down):
    batch, seq, _ = x_prompt.shape
    dec_batch, dec_seq, _ = x_sample.shape
    n_gates = 2 * M_HEADS
    w_in = mlstm_w_in[0]
    kv0 = 2 * QK_W
    w_main = jnp.concatenate(
        [w_in[:, :QK_W], w_in[:, kv0:kv0 + 2 * V_W],
         jnp.pad(w_in[:, kv0 + 2 * V_W:], ((0, 0), (0, GATE_PAD - n_gates)))], axis=1)
    p = dict(
        norm_mix_pre=norm_mix_pre, norm_mix_post=norm_mix_post, norm_ffn_pre=norm_ffn_pre,
        norm_ffn_post=norm_ffn_post, attn_b_qkv=attn_b_qkv, attn_b_o=attn_b_o, attn_sinks=attn_sinks,
        mlstm_norm=mlstm_norm, ffn_conv_w=ffn_conv_w, ffn_conv_b=ffn_conv_b,
        w_qkv=attn_w_qkv[0].astype(BF16),
        w_o=attn_w_o[0].astype(BF16),
        w_in=w_main.astype(BF16),
        w_k=w_in[:, QK_W:kv0].astype(BF16),
        b_in=jnp.pad(mlstm_b_gates[0], (QK_W + 2 * V_W, GATE_PAD - n_gates)).reshape(1, PROJ_W),
        w_out=mlstm_w_out[0].astype(BF16),
        w_up=ffn_w_up.astype(BF16),
        w_down=ffn_w_down.astype(BF16),
    )
    yp, k_p, v_p, c_p, n_p, m_p, conv_p = _trunk(
        x_prompt.reshape(batch * seq, D_MODEL), jnp.arange(seq), seq, ROW_TILE, WIDE_ROW_TILE, None, None, None, p)
    ys, k_s, v_s, c_s, n_s, m_s, conv_s = _trunk(
        x_sample.reshape(dec_batch * dec_seq, D_MODEL),
        jnp.tile(PAST_LEN + jnp.arange(dec_seq), dec_batch), dec_seq, dec_batch * dec_seq, dec_batch * dec_seq,
        (cache_k[0], cache_v[0]), (state_C[0], state_n[0], state_m[0]), state_conv, p)
    return (yp.reshape(batch, seq, D_MODEL), ys.reshape(dec_batch, dec_seq, D_MODEL),
            k_p, v_p, k_s, v_s, c_p, n_p, m_p, c_s, n_s, m_s, conv_p, conv_s)
```

```python
import functools

import jax
import jax.numpy as jnp
from jax import lax
from jax.experimental import pallas as pl
from jax.experimental.pallas import tpu as pltpu

F32 = jnp.float32
BF16 = jnp.bfloat16

D_MODEL = 2048
WINDOW = 128
HEAD_DIM = 64
N_HEADS = 32
N_KV_HEADS = 4
GROUP = N_HEADS // N_KV_HEADS
Q_W = N_HEADS * HEAD_DIM
KV_W = N_KV_HEADS * HEAD_DIM
ROPE_THETA = 10000.0
M_HEADS = 4
M_QK_DIM = 256
M_V_DIM = 512
QK_W = M_HEADS * M_QK_DIM
V_W = M_HEADS * M_V_DIM
GATE_PAD = 256
PROJ_W = QK_W + 2 * V_W + GATE_PAD
D_FF = 4 * D_MODEL
CONV_W = 3
EPS = 1e-6
NEG_INF = -1e30
PAST_LEN = 16384

SUBLANES = 8
LANES = 128
VMEM_LIMIT_BYTES = 60 * 1024 * 1024

ROW_TILE = 512
WIDE_ROW_TILE = 1024
FF_TILE = 1024
PROJ_N_TILE = 896
MLSTM_CHUNK = 256
MLSTM_SAMPLE_SEQS = 2


def _params(*sem):
    return pltpu.CompilerParams(dimension_semantics=sem, vmem_limit_bytes=VMEM_LIMIT_BYTES)


def _rmsnorm(xf, g):
    r = xf * lax.rsqrt(jnp.mean(xf * xf, axis=-1, keepdims=True) + EPS)
    return r * g


def _qkv_rope_kernel(x_ref, g_ref, w_ref, b_ref, cos_ref, sin_ref, q_ref, kv_ref):
    xn = _rmsnorm(x_ref[...], g_ref[...]).astype(BF16)
    y = jnp.dot(xn, w_ref[...], preferred_element_type=F32) + b_ref[...]
    cos = cos_ref[...]
    sin = sin_ref[...]
    lane = lax.broadcasted_iota(jnp.int32, cos.shape, 1)
    first_half = (lane & (HEAD_DIM - 1)) < (HEAD_DIM // 2)
    n_rot = (Q_W + KV_W) // LANES
    for c in range(n_rot):
        blk = y[:, c * LANES:(c + 1) * LANES]
        sw = jnp.where(first_half, pltpu.roll(blk, LANES - HEAD_DIM // 2, 1),
                       pltpu.roll(blk, HEAD_DIM // 2, 1))
        r = blk * cos + sw * sin
        if c < Q_W // LANES:
            q_ref[:, c * LANES:(c + 1) * LANES] = (r * (HEAD_DIM ** -0.5)).astype(q_ref.dtype)
        else:
            o = c * LANES - Q_W
            kv_ref[:, o:o + LANES] = r
    kv_ref[:, KV_W:2 * KV_W] = y[:, Q_W + KV_W:Q_W + 2 * KV_W]


def _qkv_rope(x, g, w, b, cos, sin, tm, q_dtype):
    m = x.shape[0]
    n = w.shape[1]
    n_pos_tiles = cos.shape[0] // tm
    return pl.pallas_call(
        _qkv_rope_kernel,
        grid=(m // tm,),
        in_specs=[
            pl.BlockSpec((tm, D_MODEL), lambda i: (i, 0)),
            pl.BlockSpec((1, D_MODEL), lambda i: (0, 0)),
            pl.BlockSpec((D_MODEL, n), lambda i: (0, 0)),
            pl.BlockSpec((1, n), lambda i: (0, 0)),
            pl.BlockSpec((tm, LANES), lambda i: (i % n_pos_tiles, 0)),
            pl.BlockSpec((tm, LANES), lambda i: (i % n_pos_tiles, 0)),
        ],
        out_specs=[
            pl.BlockSpec((tm, Q_W), lambda i: (i, 0)),
            pl.BlockSpec((tm, 2 * KV_W), lambda i: (i, 0)),
        ],
        out_shape=[
            jax.ShapeDtypeStruct((m, Q_W), q_dtype),
            jax.ShapeDtypeStruct((m, 2 * KV_W), F32),
        ],
        compiler_params=_params("arbitrary"),
        name="qkv_rope",
    )(x, g, w, b, cos, sin)


def _attn_prompt_kernel(sinks_ref, q_ref, kvp_ref, kvc_ref, wu_ref, wd_ref, o_ref, wu_bf_ref, wd_bf_ref):
    wu_bf_ref[...] = wu_ref[...].astype(BF16)
    wd_bf_ref[...] = wd_ref[...].astype(BF16)
    n = pl.program_id(1)
    w = WINDOW
    ri = lax.broadcasted_iota(jnp.int32, (w, 2 * w), 0)
    ci = lax.broadcasted_iota(jnp.int32, (w, 2 * w), 1)
    has_prev = jnp.full((w, 2 * w), n, jnp.int32) > 0
    allowed = ((ci < w) & (ci > ri) & has_prev) | ((ci >= w) & ((ci - w) <= ri))
    low = lax.broadcasted_iota(jnp.int32, (w, LANES), 1) < HEAD_DIM
    ones = jnp.ones((2 * w, HEAD_DIM), BF16)
    for g in range(N_KV_HEADS):
        ks = slice(g * HEAD_DIM, (g + 1) * HEAD_DIM)
        vs = slice(KV_W + g * HEAD_DIM, KV_W + (g + 1) * HEAD_DIM)
        k = jnp.concatenate([kvp_ref[:, ks], kvc_ref[:, ks]], axis=0).astype(BF16)
        v = jnp.concatenate([kvp_ref[:, vs], kvc_ref[:, vs]], axis=0).astype(BF16)
        v1 = jnp.concatenate([v, ones], axis=1)
        q = jnp.concatenate(
            [q_ref[:, (g * GROUP + h) * HEAD_DIM:(g * GROUP + h + 1) * HEAD_DIM] for h in range(GROUP)],
            axis=0)
        s = lax.dot_general(q, k, (((1,), (1,)), ((), ())), preferred_element_type=F32)
        ps, es = [], []
        for h in range(GROUP):
            sh = jnp.where(allowed, s[h * w:(h + 1) * w], NEG_INF)
            sink = sinks_ref[g * GROUP + h]
            m = jnp.maximum(jnp.max(sh, axis=-1, keepdims=True), sink)
            ps.append(jnp.exp(sh - m).astype(BF16))
            es.append(jnp.exp(sink - m))
        oa = jnp.dot(jnp.concatenate(ps, axis=0), v1, preferred_element_type=F32)
        for h in range(0, GROUP, 2):
            a = oa[h * w:(h + 1) * w]
            b = oa[(h + 1) * w:(h + 2) * w]
            num = jnp.where(low, a, pltpu.roll(b, HEAD_DIM, 1))
            den = jnp.where(low, pltpu.roll(a, HEAD_DIM, 1) + es[h], b + es[h + 1])
            c0 = (g * GROUP + h) * HEAD_DIM
            o_ref[:, c0:c0 + LANES] = (num / den).astype(o_ref.dtype)


def _attn_prompt(sinks, q, kv, batch, seq, w_up, w_down):
    nb = seq // WINDOW
    steps = batch * nb
    up_rows = w_up.shape[1] // steps
    down_rows = w_down.shape[1] // steps
    up_spec = pl.BlockSpec((w_up.shape[0], up_rows, w_up.shape[2]), lambda b, n: (0, b * nb + n, 0))
    down_spec = pl.BlockSpec((w_down.shape[0], down_rows, w_down.shape[2]), lambda b, n: (0, b * nb + n, 0))
    return pl.pallas_call(
        _attn_prompt_kernel,
        grid=(batch, nb),
        in_specs=[
            pl.BlockSpec(memory_space=pltpu.SMEM),
            pl.BlockSpec((WINDOW, Q_W), lambda b, n: (b * nb + n, 0)),
            pl.BlockSpec((WINDOW, 2 * KV_W), lambda b, n: (b * nb + jnp.maximum(n - 1, 0), 0)),
            pl.BlockSpec((WINDOW, 2 * KV_W), lambda b, n: (b * nb + n, 0)),
            up_spec,
            down_spec,
        ],
        out_specs=[pl.BlockSpec((WINDOW, Q_W), lambda b, n: (b * nb + n, 0)), up_spec, down_spec],
        out_shape=[
            jax.ShapeDtypeStruct((batch * seq, Q_W), BF16),
            jax.ShapeDtypeStruct(w_up.shape, BF16),
            jax.ShapeDtypeStruct(w_down.shape, BF16),
        ],
        compiler_params=_params("arbitrary", "arbitrary"),
        name="attn_prompt",
    )(sinks, q, kv, kv, w_up, w_down)


def _attn_sample_kernel(sinks_ref, q_ref, kvn_ref, ck_ref, cv_ref, o_ref, ko_ref, vo_ref):
    t = q_ref.shape[0]
    wc = ck_ref.shape[1]
    rows = N_HEADS * t
    ri = lax.broadcasted_iota(jnp.int32, (rows, wc), 0) & (t - 1)
    ci = lax.broadcasted_iota(jnp.int32, (rows, wc), 1)
    allowed_c = ci > ri
    ri_n = lax.broadcasted_iota(jnp.int32, (rows, t), 0) & (t - 1)
    ci_n = lax.broadcasted_iota(jnp.int32, (rows, t), 1)
    allowed_n = ci_n <= ri_n
    nt = (((1,), (1,)), ((), ()))
    s_c, s_n = [], []
    for g in range(N_KV_HEADS):
        ks = slice(g * HEAD_DIM, (g + 1) * HEAD_DIM)
        q = jnp.concatenate(
            [q_ref[:, (g * GROUP + h) * HEAD_DIM:(g * GROUP + h + 1) * HEAD_DIM] for h in range(GROUP)],
            axis=0)
        s_c.append(lax.dot_general(q.astype(BF16), ck_ref[0, :, ks].astype(BF16), nt, preferred_element_type=F32))
        s_n.append(lax.dot_general(q, kvn_ref[:, ks], nt, preferred_element_type=F32))
    s_c = jnp.where(allowed_c, jnp.concatenate(s_c, axis=0), NEG_INF)
    s_n = jnp.where(allowed_n, jnp.concatenate(s_n, axis=0), NEG_INF)
    sink = jnp.concatenate([jnp.full((t, 1), sinks_ref[h], F32) for h in range(N_HEADS)], axis=0)
    m = jnp.maximum(jnp.maximum(jnp.max(s_c, axis=-1, keepdims=True),
                                jnp.max(s_n, axis=-1, keepdims=True)), sink)
    p_c = jnp.exp(s_c - m)
    p_n = jnp.exp(s_n - m)
    inv_l = 1.0 / (jnp.sum(p_c, axis=-1, keepdims=True) + jnp.sum(p_n, axis=-1, keepdims=True)
                   + jnp.exp(sink - m))
    p_c = (p_c * inv_l).astype(BF16)
    p_n = p_n * inv_l
    gr = GROUP * t
    for g in range(N_KV_HEADS):
        ks = slice(g * HEAD_DIM, (g + 1) * HEAD_DIM)
        vs = slice(KV_W + g * HEAD_DIM, KV_W + (g + 1) * HEAD_DIM)
        o = (jnp.dot(p_c[g * gr:(g + 1) * gr], cv_ref[0, :, ks].astype(BF16), preferred_element_type=F32)
             + jnp.dot(p_n[g * gr:(g + 1) * gr], kvn_ref[:, vs], preferred_element_type=F32))
        for h in range(GROUP):
            c0 = (g * GROUP + h) * HEAD_DIM
            o_ref[:, c0:c0 + HEAD_DIM] = o[h * t:(h + 1) * t]
    ko_ref[0, 0:wc - t, :] = ck_ref[0, t:wc, :]
    ko_ref[0, wc - t:wc, :] = kvn_ref[:, 0:KV_W]
    vo_ref[0, 0:wc - t, :] = cv_ref[0, t:wc, :]
    vo_ref[0, wc - t:wc, :] = kvn_ref[:, KV_W:2 * KV_W]


def _attn_sample(sinks, q, kvn, ck, cv, t):
    db = ck.shape[0]
    wc = ck.shape[1]
    return pl.pallas_call(
        _attn_sample_kernel,
        grid=(db,),
        in_specs=[
            pl.BlockSpec(memory_space=pltpu.SMEM),
            pl.BlockSpec((t, Q_W), lambda b: (b, 0)),
            pl.BlockSpec((t, 2 * KV_W), lambda b: (b, 0)),
            pl.BlockSpec((1, wc, KV_W), lambda b: (b, 0, 0)),
            pl.BlockSpec((1, wc, KV_W), lambda b: (b, 0, 0)),
        ],
        out_specs=[
            pl.BlockSpec((t, Q_W), lambda b: (b, 0)),
            pl.BlockSpec((1, wc, KV_W), lambda b: (b, 0, 0)),
            pl.BlockSpec((1, wc, KV_W), lambda b: (b, 0, 0)),
        ],
        out_shape=[
            jax.ShapeDtypeStruct((db * t, Q_W), F32),
            jax.ShapeDtypeStruct((db, wc, KV_W), F32),
            jax.ShapeDtypeStruct((db, wc, KV_W), F32),
        ],
        compiler_params=_params("arbitrary"),
        name="attn_sample",
    )(sinks, q, kvn, ck, cv)


def _proj_post_kernel(has_bias, a_ref, w_ref, b_ref, x_ref, g_ref, gn_ref, o_ref, xn_ref):
    y = jnp.dot(a_ref[...].astype(BF16), w_ref[...], preferred_element_type=F32)
    if has_bias:
        y = y + b_ref[...]
    x_new = x_ref[...] + _rmsnorm(y, g_ref[...])
    o_ref[...] = x_new
    xn_ref[...] = _rmsnorm(x_new, gn_ref[...]).astype(xn_ref.dtype)


def _proj_post(a, w, b, x, g, g_next, tm):
    m, k = a.shape
    has_bias = b is not None
    if b is None:
        b = jnp.zeros((1, D_MODEL), F32)
    return pl.pallas_call(
        functools.partial(_proj_post_kernel, has_bias),
        grid=(m // tm,),
        in_specs=[
            pl.BlockSpec((tm, k), lambda i: (i, 0)),
            pl.BlockSpec((k, D_MODEL), lambda i: (0, 0)),
            pl.BlockSpec((1, D_MODEL), lambda i: (0, 0)),
            pl.BlockSpec((tm, D_MODEL), lambda i: (i, 0)),
            pl.BlockSpec((1, D_MODEL), lambda i: (0, 0)),
            pl.BlockSpec((1, D_MODEL), lambda i: (0, 0)),
        ],
        out_specs=[pl.BlockSpec((tm, D_MODEL), lambda i: (i, 0)), pl.BlockSpec((tm, D_MODEL), lambda i: (i, 0))],
        out_shape=[jax.ShapeDtypeStruct((m, D_MODEL), F32), jax.ShapeDtypeStruct((m, D_MODEL), BF16)],
        compiler_params=_params("arbitrary"),
        name="proj_post",
    )(a, w, b, x, g, g_next)


def _norm_matmul_kernel(has_bias, x_ref, g_ref, w_ref, b_ref, o_ref, xn_ref):
    @pl.when(pl.program_id(1) == 0)
    def _():
        xn_ref[...] = _rmsnorm(x_ref[...], g_ref[...]).astype(BF16)

    y = jnp.dot(xn_ref[...], w_ref[...], preferred_element_type=F32)
    if has_bias:
        y = y + b_ref[...]
    o_ref[...] = y


def _norm_matmul(x, g, w, b, tm, tn):
    m = x.shape[0]
    n = w.shape[1]
    has_bias = b is not None
    if b is None:
        b = jnp.zeros((1, n), F32)
    return pl.pallas_call(
        functools.partial(_norm_matmul_kernel, has_bias),
        grid=(m // tm, n // tn),
        in_specs=[
            pl.BlockSpec((tm, D_MODEL), lambda i, j: (i, 0)),
            pl.BlockSpec((1, D_MODEL), lambda i, j: (0, 0)),
            pl.BlockSpec((D_MODEL, tn), lambda i, j: (0, j)),
            pl.BlockSpec((1, tn), lambda i, j: (0, j)),
        ],
        out_specs=pl.BlockSpec((tm, tn), lambda i, j: (i, j)),
        out_shape=jax.ShapeDtypeStruct((m, n), F32),
        scratch_shapes=[pltpu.VMEM((tm, D_MODEL), BF16)],
        compiler_params=_params("arbitrary", "arbitrary"),
        name="norm_matmul",
    )(x, g, w, b)


def _norm_matmul_t_kernel(x_ref, g_ref, wt_ref, o_ref):
    xn = _rmsnorm(x_ref[...], g_ref[...]).astype(BF16)
    o_ref[...] = lax.dot_general(wt_ref[...], xn, (((1,), (1,)), ((), ())), preferred_element_type=F32)


def _norm_matmul_t(x, g, wt, tm):
    m = x.shape[0]
    n = wt.shape[0]
    return pl.pallas_call(
        _norm_matmul_t_kernel,
        grid=(m // tm,),
        in_specs=[
            pl.BlockSpec((tm, D_MODEL), lambda i: (i, 0)),
            pl.BlockSpec((1, D_MODEL), lambda i: (0, 0)),
            pl.BlockSpec((n, D_MODEL), lambda i: (0, 0)),
        ],
        out_specs=pl.BlockSpec((n, tm), lambda i: (0, i)),
        out_shape=jax.ShapeDtypeStruct((n, m), F32),
        compiler_params=_params("arbitrary"),
        name="norm_matmul_t",
    )(x, g, wt)


def _log_sigmoid(x):
    return jnp.minimum(x, 0.0) - jnp.log1p(jnp.exp(-jnp.abs(x)))


def _chunk_gates(li, lf, m_state):
    L = li.shape[0]
    ri = lax.broadcasted_iota(jnp.int32, (L, L), 0)
    ci = lax.broadcasted_iota(jnp.int32, (L, L), 1)
    eye = ri == ci
    tril = ci <= ri
    lf_row = jnp.sum(jnp.where(eye, lf, 0.0), axis=0, keepdims=True)
    li_row = jnp.sum(jnp.where(eye, li, 0.0), axis=0, keepdims=True)
    b_col = jnp.sum(jnp.where(tril, lf_row, 0.0), axis=1, keepdims=True)
    b_row = jnp.sum(jnp.where(ri <= ci, lf, 0.0), axis=0, keepdims=True)
    dmat = jnp.where(tril, b_col - b_row + li_row, -jnp.inf)
    inter = b_col + m_state
    m_row = jnp.maximum(inter, jnp.max(dmat, axis=1, keepdims=True))
    b_last = b_col[L - 1:L, :]
    gk_col = b_last - b_col + li
    gk_row = b_last - b_row + li_row
    m_new = jnp.maximum(b_last + m_state, jnp.max(gk_col, axis=0, keepdims=True))
    decay = jnp.exp(b_last + m_state - m_new)
    return jnp.exp(dmat - m_row), jnp.exp(inter - m_row), m_row, gk_col, gk_row, m_new, decay


def _mlstm_chunk(q, k, v, li, lf, c_state, n_state, m_state):
    dexp, w_inter, m_row, gk_col, _, m_new, decay = _chunk_gates(li, lf, m_state)
    s = lax.dot_general(q, k, (((1,), (1,)), ((), ())), preferred_element_type=F32)
    sm = s * dexp
    num = (jnp.dot(sm, v, preferred_element_type=F32)
           + w_inter * jnp.dot(q, c_state, preferred_element_type=F32))
    den = jnp.sum(sm, axis=1, keepdims=True) + w_inter * jnp.sum(q * n_state, axis=1, keepdims=True)
    h = num * (1.0 / jnp.maximum(jnp.abs(den), jnp.exp(-m_row)))
    kw = k * jnp.exp(gk_col - m_new)
    c_new = decay * c_state + lax.dot_general(kw, v, (((0,), (0,)), ((), ())), preferred_element_type=F32)
    n_new = decay * n_state + jnp.sum(kw, axis=0, keepdims=True)
    return h, c_new, n_new, m_new


def _mlstm_chunk_t(q, kt, v1, li, lf, c_aug, m_state):
    dk = q.shape[1]
    dv = v1.shape[1] - LANES
    scale = dk ** -0.5
    dexp, w_inter, m_row, _, gk_row, m_new, decay = _chunk_gates(li, lf, m_state)
    qm = q.astype(BF16)
    s = jnp.dot(qm, kt.astype(BF16), preferred_element_type=F32)
    sm = (s * scale) * dexp
    num = (jnp.dot(sm.astype(BF16), v1, preferred_element_type=F32)
           + (w_inter * scale) * jnp.dot(qm, c_aug.astype(BF16), preferred_element_type=F32))
    den = num[:, dv:dv + 1]
    h = num[:, :dv] * (1.0 / jnp.maximum(jnp.abs(den), jnp.exp(-m_row)))
    kwt = (kt * jnp.exp(gk_row - m_new)).astype(BF16)
    c_new = decay * c_aug + jnp.dot(kwt, v1, preferred_element_type=F32)
    return h, c_new, m_new


def _head_cols(hd):
    q0 = hd * M_QK_DIM
    v0 = QK_W + hd * M_V_DIM
    o0 = QK_W + V_W + hd * M_V_DIM
    g0 = QK_W + 2 * V_W
    return (q0, q0 + M_QK_DIM), (v0, v0 + M_V_DIM), (o0, o0 + M_V_DIM), g0 + hd, g0 + M_HEADS + hd


def _head_out(h, o, ng):
    return _rmsnorm(h, ng) * (0.5 * (jnp.tanh(0.5 * o) + 1.0))


def _mlstm_prompt_kernel(batch, proj_ref, *rest):
    kt_refs = rest[:batch]
    ng_ref, h_ref, c_ref, m_ref = rest[batch:]

    @pl.when(pl.program_id(0) == 0)
    def _():
        c_ref[...] = jnp.zeros(c_ref.shape, F32)
        m_ref[...] = jnp.zeros(m_ref.shape, F32)

    L = proj_ref.shape[1]
    one_col = (lax.broadcasted_iota(jnp.int32, (L, LANES), 1) == 0).astype(BF16)
    for b in range(batch):
        for hd in range(M_HEADS):
            r = b * M_HEADS + hd
            (q0, q1), (v0, v1), (o0, o1), gi, gf = _head_cols(hd)
            va = jnp.concatenate([proj_ref[b, :, v0:v1].astype(BF16), one_col], axis=1)
            h, c_new, m_new = _mlstm_chunk_t(
                proj_ref[b, :, q0:q1], kt_refs[b][hd * M_QK_DIM:(hd + 1) * M_QK_DIM, :], va,
                proj_ref[b, :, gi:gi + 1], _log_sigmoid(proj_ref[b, :, gf:gf + 1]),
                c_ref[r], m_ref[r:r + 1, 0:1])
            hn = _head_out(h, proj_ref[b, :, o0:o1], ng_ref[:, hd * M_V_DIM:(hd + 1) * M_V_DIM])
            h_ref[b, :, hd * M_V_DIM:(hd + 1) * M_V_DIM] = hn.astype(h_ref.dtype)
            c_ref[r] = c_new
            m_ref[r:r + 1, :] = jnp.broadcast_to(m_new, (1, LANES))


def _mlstm_prompt(proj, kt, ng, batch, seq):
    chunk = MLSTM_CHUNK
    nc = seq // chunk
    rows = batch * M_HEADS
    return pl.pallas_call(
        functools.partial(_mlstm_prompt_kernel, batch),
        grid=(nc,),
        in_specs=[pl.BlockSpec((batch, chunk, PROJ_W), lambda c: (0, c, 0))]
        + [pl.BlockSpec((QK_W, chunk), lambda c, b=b: (0, b * nc + c)) for b in range(batch)]
        + [pl.BlockSpec((1, V_W), lambda c: (0, 0))],
        out_specs=[
            pl.BlockSpec((batch, chunk, V_W), lambda c: (0, c, 0)),
            pl.BlockSpec((rows, M_QK_DIM, M_V_DIM + LANES), lambda c: (0, 0, 0)),
            pl.BlockSpec((rows, LANES), lambda c: (0, 0)),
        ],
        out_shape=[
            jax.ShapeDtypeStruct((batch, seq, V_W), BF16),
            jax.ShapeDtypeStruct((rows, M_QK_DIM, M_V_DIM + LANES), F32),
            jax.ShapeDtypeStruct((rows, LANES), F32),
        ],
        compiler_params=_params("arbitrary"),
        name="mlstm_prompt",
    )(proj.reshape(batch, seq, PROJ_W), *([kt] * batch), ng)


def _mlstm_sample_kernel(t, proj_ref, k_ref, ng_ref, c0_ref, n0_ref, m0_ref, h_ref, c_ref, n_ref, m_ref):
    for s in range(c0_ref.shape[0]):
        rows = slice(s * t, (s + 1) * t)
        for hd in range(M_HEADS):
            (q0, q1), (v0, v1), (o0, o1), gi, gf = _head_cols(hd)
            h, c_new, n_new, m_new = _mlstm_chunk(
                proj_ref[rows, q0:q1] * (M_QK_DIM ** -0.5), k_ref[rows, hd * M_QK_DIM:(hd + 1) * M_QK_DIM],
                proj_ref[rows, v0:v1], proj_ref[rows, gi:gi + 1], _log_sigmoid(proj_ref[rows, gf:gf + 1]),
                c0_ref[s, hd], n0_ref[s, hd:hd + 1, :], m0_ref[s, :, hd:hd + 1])
            h_ref[rows, hd * M_V_DIM:(hd + 1) * M_V_DIM] = _head_out(
                h, proj_ref[rows, o0:o1], ng_ref[:, hd * M_V_DIM:(hd + 1) * M_V_DIM])
            c_ref[s, hd] = c_new
            n_ref[s, hd:hd + 1, :] = n_new
            m_ref[s, :, hd:hd + 1] = m_new


def _mlstm_sample(proj, k, ng, c0, n0, m0, t):
    db = c0.shape[0]
    sb = MLSTM_SAMPLE_SEQS
    m0 = m0.reshape(db, 1, M_HEADS)
    state_specs = [
        pl.BlockSpec((sb, M_HEADS, M_QK_DIM, M_V_DIM), lambda b: (b, 0, 0, 0)),
        pl.BlockSpec((sb, M_HEADS, M_QK_DIM), lambda b: (b, 0, 0)),
        pl.BlockSpec((sb, 1, M_HEADS), lambda b: (b, 0, 0)),
    ]
    return pl.pallas_call(
        functools.partial(_mlstm_sample_kernel, t),
        grid=(db // sb,),
        in_specs=[
            pl.BlockSpec((sb * t, PROJ_W), lambda b: (b, 0)),
            pl.BlockSpec((sb * t, QK_W), lambda b: (b, 0)),
            pl.BlockSpec((1, V_W), lambda b: (0, 0)),
        ] + state_specs,
        out_specs=[pl.BlockSpec((sb * t, V_W), lambda b: (b, 0))] + state_specs,
        out_shape=[
            jax.ShapeDtypeStruct((db * t, V_W), F32),
            jax.ShapeDtypeStruct(c0.shape, F32),
            jax.ShapeDtypeStruct(n0.shape, F32),
            jax.ShapeDtypeStruct(m0.shape, F32),
        ],
        compiler_params=_params("arbitrary"),
        name="mlstm_sample",
    )(proj, k, ng, c0, n0, m0)


def _conv3(u, u1, u2, w, b):
    return ((b + w[0:1] * u2) + w[1:2] * u1) + w[2:3] * u


def _conv_stream(u, tail, w, b):
    u1 = pltpu.roll(u, 1, 0)
    u2 = pltpu.roll(u, 2, 0)
    c = _conv3(u, u1, u2, w, b)
    uf = u[0:SUBLANES]
    row = lax.broadcasted_iota(jnp.int32, uf.shape, 0)
    uf1 = jnp.where(row < 1, pltpu.roll(tail, 1, 0), pltpu.roll(uf, 1, 0))
    uf2 = jnp.where(row < 2, pltpu.roll(tail, 2, 0), pltpu.roll(uf, 2, 0))
    cf = _conv3(uf, uf1, uf2, w, b)
    return jnp.concatenate([cf, c[SUBLANES:]], axis=0)


def _conv_seq8(u, prev, w, b):
    nseq = u.shape[0] // SUBLANES
    p0 = jnp.broadcast_to(prev[:, 0:1, :], (nseq, SUBLANES, u.shape[1])).reshape(u.shape)
    p1 = jnp.broadcast_to(prev[:, 1:2, :], (nseq, SUBLANES, u.shape[1])).reshape(u.shape)
    row = lax.broadcasted_iota(jnp.int32, u.shape, 0) & (SUBLANES - 1)
    u1 = jnp.where(row == 0, p1, pltpu.roll(u, 1, 0))
    u2 = jnp.where(row == 0, p0, jnp.where(row == 1, p1, pltpu.roll(u, 2, 0)))
    return _conv3(u, u1, u2, w, b)


def _ffn_kernel(stream, tiles_per_seq, x_ref, xn_ref, wg_ref, wv_ref, cw_ref, cb_ref, wd_ref, gpost_ref, *rest):
    if stream:
        o_ref, sg_ref, sv_ref, tail_g, tail_v = rest
    else:
        pg_ref, pv_ref, o_ref, sg_ref, sv_ref = rest
    i = pl.program_id(0)
    j = pl.program_id(1)
    nf = pl.num_programs(1)
    tm = x_ref.shape[0]

    if stream:
        @pl.when(i % tiles_per_seq == 0)
        def _():
            tail_g[j] = jnp.zeros(tail_g.shape[1:], F32)
            tail_v[j] = jnp.zeros(tail_v.shape[1:], F32)

    xn = xn_ref[...]
    ug = jnp.dot(xn, wg_ref[...], preferred_element_type=F32)
    uv = jnp.dot(xn, wv_ref[...], preferred_element_type=F32)
    cwg, cwv, cbg, cbv = cw_ref[j], cw_ref[j + nf], cb_ref[j], cb_ref[j + nf]
    if stream:
        cg = _conv_stream(ug, tail_g[j], cwg, cbg)
        cv = _conv_stream(uv, tail_v[j], cwv, cbv)
        tail_g[j] = ug[tm - SUBLANES:]
        tail_v[j] = uv[tm - SUBLANES:]
        sg_ref[0, j] = ug[tm - SUBLANES:]
        sv_ref[0, j] = uv[tm - SUBLANES:]
    else:
        cg = _conv_seq8(ug, pg_ref[...], cwg, cbg)
        cv = _conv_seq8(uv, pv_ref[...], cwv, cbv)
        sg_ref[...] = ug.reshape(sg_ref.shape)
        sv_ref[...] = uv.reshape(sv_ref.shape)
    h = (jax.nn.gelu(cg, approximate=True) * cv).astype(BF16)
    acc = jnp.where(j == 0, 0.0, o_ref[...])
    o_ref[...] = acc + jnp.dot(h, wd_ref[...], preferred_element_type=F32)

    @pl.when(j == nf - 1)
    def _():
        o_ref[...] = x_ref[...] + _rmsnorm(o_ref[...], gpost_ref[...])


def _ffn(x, xn, prev, layer, w_up, conv_w, conv_b, w_down, gpost, tm, tf, seq):
    m = x.shape[0]
    nf = D_FF // tf
    stream = prev is None
    n_layers = conv_w.shape[0]
    conv_w = conv_w.reshape(n_layers, CONV_W, 2 * nf, tf).transpose(0, 2, 1, 3)
    conv_b = conv_b.reshape(n_layers, 2 * nf, 1, tf)
    in_specs = [
        pl.BlockSpec((tm, D_MODEL), lambda i, j: (i, 0)),
        pl.BlockSpec((tm, D_MODEL), lambda i, j: (i, 0)),
        pl.BlockSpec((None, D_MODEL, tf), lambda i, j: (layer, 0, j)),
        pl.BlockSpec((None, D_MODEL, tf), lambda i, j: (layer, 0, j + nf)),
        pl.BlockSpec((None, 2 * nf, CONV_W, tf), lambda i, j: (layer, 0, 0, 0)),
        pl.BlockSpec((None, 2 * nf, 1, tf), lambda i, j: (layer, 0, 0, 0)),
        pl.BlockSpec((None, tf, D_MODEL), lambda i, j: (layer, j, 0)),
        pl.BlockSpec((1, D_MODEL), lambda i, j: (0, 0)),
    ]
    args = [x, xn, w_up, w_up, conv_w, conv_b, w_down, gpost]
    scratch = []
    if stream:
        tiles_per_seq = seq // tm
        state_shape = (m // tm, nf, SUBLANES, tf)
        state_spec = pl.BlockSpec((1, nf, SUBLANES, tf), lambda i, j: (i, 0, 0, 0))
        scratch += [pltpu.VMEM((nf, SUBLANES, tf), F32), pltpu.VMEM((nf, SUBLANES, tf), F32)]
    else:
        assert seq == SUBLANES and m == tm
        tiles_per_seq = 1
        nseq = m // seq
        state_shape = (nseq, SUBLANES, D_FF)
        state_spec = pl.BlockSpec((nseq, SUBLANES, tf), lambda i, j: (0, 0, j))
        in_specs += [
            pl.BlockSpec((nseq, CONV_W - 1, tf), lambda i, j: (0, 0, j)),
            pl.BlockSpec((nseq, CONV_W - 1, tf), lambda i, j: (0, 0, j + nf)),
        ]
        args += [prev, prev]
    return pl.pallas_call(
        functools.partial(_ffn_kernel, stream, tiles_per_seq),
        grid=(m // tm, nf),
        in_specs=in_specs,
        out_specs=[pl.BlockSpec((tm, D_MODEL), lambda i, j: (i, 0)), state_spec, state_spec],
        out_shape=[
            jax.ShapeDtypeStruct((m, D_MODEL), F32),
            jax.ShapeDtypeStruct(state_shape, F32),
            jax.ShapeDtypeStruct(state_shape, F32),
        ],
        scratch_shapes=scratch,
        compiler_params=_params("arbitrary", "arbitrary"),
        name="conv_ffn",
    )(*args)


def _rope_tables(pos):
    half = HEAD_DIM // 2
    inv = ROPE_THETA ** (-jnp.arange(half, dtype=F32) / half)
    ang = pos.astype(F32)[:, None] * inv[None, :]
    cos = jnp.cos(ang)
    sin = jnp.sin(ang)
    reps = LANES // HEAD_DIM
    return (jnp.tile(cos, (1, 2 * reps)), jnp.tile(jnp.concatenate([-sin, sin], axis=1), (1, reps)))


def _conv_state(sg, sv, tiles_per_seq):
    keep = slice(SUBLANES - (CONV_W - 1), SUBLANES)
    if tiles_per_seq is None:
        return jnp.concatenate([sg[:, keep], sv[:, keep]], axis=-1)
    last = slice(tiles_per_seq - 1, None, tiles_per_seq)

    def rows(s):
        s = s[last, :, keep, :]
        return jnp.swapaxes(s, 1, 2).reshape(s.shape[0], CONV_W - 1, -1)

    return jnp.concatenate([rows(sg), rows(sv)], axis=-1)


def _trunk(x, pos, seq, tm, wide_tm, cache, state, state_conv, p):
    m = x.shape[0]
    nseq = m // seq
    sample = cache is not None
    row = lambda v: v.reshape(1, -1)

    cos, sin = _rope_tables(pos)
    q, kv = _qkv_rope(x, row(p["norm_mix_pre"][0]), p["w_qkv"], row(p["attn_b_qkv"][0]), cos, sin, tm,
                      F32 if sample else BF16)
    if sample:
        ck, cv = cache
        o, k_new, v_new = _attn_sample(p["attn_sinks"][0], q, kv, ck.reshape(nseq, WINDOW, KV_W),
                                       cv.reshape(nseq, WINDOW, KV_W), seq)
    else:
        o, p["w_up"], p["w_down"] = _attn_prompt(p["attn_sinks"][0], q, kv, nseq, seq, p["ffn_w_up"], p["ffn_w_down"])
        kv3 = kv.reshape(nseq, seq, 2 * KV_W)
        k_new = kv3[:, seq - WINDOW:, :KV_W]
        v_new = kv3[:, seq - WINDOW:, KV_W:]
    k_new = k_new.reshape(1, nseq, WINDOW, N_KV_HEADS, HEAD_DIM)
    v_new = v_new.reshape(1, nseq, WINDOW, N_KV_HEADS, HEAD_DIM)
    x, xn = _proj_post(o, p["w_o"], row(p["attn_b_o"][0]), x, row(p["norm_mix_post"][0]),
                       row(p["norm_ffn_pre"][0]), tm)
    x, sg0, sv0 = _ffn(x, xn, state_conv[0] if sample else None, 0, p["w_up"], p["ffn_conv_w"], p["ffn_conv_b"],
                       p["w_down"], row(p["norm_ffn_post"][0]), tm, FF_TILE, seq)

    g1 = row(p["norm_mix_pre"][1])
    proj = _norm_matmul(x, g1, p["w_in"], p["b_in"], wide_tm, PROJ_N_TILE)
    ng = row(p["mlstm_norm"][0])
    if sample:
        c0, n0, m0 = state
        k = _norm_matmul(x, g1, p["w_k"], None, wide_tm, QK_W)
        h, c_new, n_new, m_new = _mlstm_sample(proj, k, ng, c0, n0, m0, seq)
    else:
        kt = _norm_matmul_t(x, g1, p["w_k"].T, wide_tm)
        h, c_aug, m_new = _mlstm_prompt(proj, kt, ng, nseq, seq)
        h = h.reshape(m, V_W)
        c_new = c_aug[:, :, :M_V_DIM]
        n_new = c_aug[:, :, M_V_DIM]
        m_new = m_new[:, 0]
    c_new = c_new.reshape(1, nseq, M_HEADS, M_QK_DIM, M_V_DIM)
    n_new = n_new.reshape(1, nseq, M_HEADS, M_QK_DIM)
    m_new = m_new.reshape(1, nseq, M_HEADS)
    x, xn = _proj_post(h, p["w_out"], None, x, row(p["norm_mix_post"][1]), row(p["norm_ffn_pre"][1]), tm)
    x, sg1, sv1 = _ffn(x, xn, state_conv[1] if sample else None, 1, p["w_up"], p["ffn_conv_w"], p["ffn_conv_b"],
                       p["w_down"], row(p["norm_ffn_post"][1]), tm, FF_TILE, seq)
    tps = None if sample else seq // tm
    conv = jnp.stack([_conv_state(sg0, sv0, tps), _conv_state(sg1, sv1, tps)])
    return x, k_new, v_new, c_new, n_new, m_new, conv


def kernel(x_prompt, x_sample, cache_k, cache_v, state_C, state_n, state_m, state_conv, norm_mix_pre,
           norm_mix_post, norm_ffn_pre, norm_ffn_post, attn_w_qkv, attn_b_qkv, attn_w_o, attn_b_o, attn_sinks,
           mlstm_w_in, mlstm_b_gates, mlstm_norm, mlstm_w_out, ffn_w_up, ffn_conv_w, ffn_conv_b, ffn_w_down):
    batch, seq, _ = x_prompt.shape
    dec_batch, dec_seq, _ = x_sample.shape
    n_gates = 2 * M_HEADS
    w_in = mlstm_w_in[0]
    kv0 = 2 * QK_W
    w_main = jnp.concatenate(
        [w_in[:, :QK_W], w_in[:, kv0:kv0 + 2 * V_W],
         jnp.pad(w_in[:, kv0 + 2 * V_W:], ((0, 0), (0, GATE_PAD - n_gates)))], axis=1)
    p = dict(
        norm_mix_pre=norm_mix_pre, norm_mix_post=norm_mix_post, norm_ffn_pre=norm_ffn_pre,
        norm_ffn_post=norm_ffn_post, attn_b_qkv=attn_b_qkv, attn_b_o=attn_b_o, attn_sinks=attn_sinks,
        mlstm_norm=mlstm_norm, ffn_conv_w=ffn_conv_w, ffn_conv_b=ffn_conv_b,
        w_qkv=attn_w_qkv[0].astype(BF16),
        w_o=attn_w_o[0].astype(BF16),
        w_in=w_main.astype(BF16),
        w_k=w_in[:, QK_W:kv0].astype(BF16),
        b_in=jnp.pad(mlstm_b_gates[0], (QK_W + 2 * V_W, GATE_PAD - n_gates)).reshape(1, PROJ_W),
        w_out=mlstm_w_out[0].astype(BF16),
        ffn_w_up=ffn_w_up, ffn_w_down=ffn_w_down,
    )
    yp, k_p, v_p, c_p, n_p, m_p, conv_p = _trunk(
        x_prompt.reshape(batch * seq, D_MODEL), jnp.arange(seq), seq, ROW_TILE, WIDE_ROW_TILE, None, None, None, p)
    ys, k_s, v_s, c_s, n_s, m_s, conv_s = _trunk(
        x_sample.reshape(dec_batch * dec_seq, D_MODEL),
        jnp.tile(PAST_LEN + jnp.arange(dec_seq), dec_batch), dec_seq, dec_batch * dec_seq, dec_batch * dec_seq,
        (cache_k[0], cache_v[0]), (state_C[0], state_n[0], state_m[0]), state_conv, p)
    return (yp.reshape(batch, seq, D_MODEL), ys.reshape(dec_batch, dec_seq, D_MODEL),
            k_p, v_p, k_s, v_s, c_p, n_p, m_p, c_s, n_s, m_s, conv_p, conv_s)
```

```python
import functools

import jax
import jax.numpy as jnp
from jax import lax
from jax.experimental import pallas as pl
from jax.experimental.pallas import tpu as pltpu

F32 = jnp.float32
BF16 = jnp.bfloat16

D_MODEL = 2048
WINDOW = 128
HEAD_DIM = 64
N_HEADS = 32
N_KV_HEADS = 4
GROUP = N_HEADS // N_KV_HEADS
Q_W = N_HEADS * HEAD_DIM
KV_W = N_KV_HEADS * HEAD_DIM
ROPE_THETA = 10000.0
M_HEADS = 4
M_QK_DIM = 256
M_V_DIM = 512
QK_W = M_HEADS * M_QK_DIM
V_W = M_HEADS * M_V_DIM
GATE_PAD = 256
PROJ_W = QK_W + 2 * V_W + GATE_PAD
D_FF = 4 * D_MODEL
CONV_W = 3
EPS = 1e-6
NEG_INF = -1e30
PAST_LEN = 16384

SUBLANES = 8
LANES = 128
VMEM_LIMIT_BYTES = 60 * 1024 * 1024

ROW_TILE = 512
WIDE_ROW_TILE = 1024
FF_TILE = 1024
PROJ_N_TILE = 896
MLSTM_CHUNK = 256
MLSTM_SAMPLE_SEQS = 2


def _params(*sem):
    return pltpu.CompilerParams(dimension_semantics=sem, vmem_limit_bytes=VMEM_LIMIT_BYTES)


def _rmsnorm(xf, g):
    r = xf * lax.rsqrt(jnp.mean(xf * xf, axis=-1, keepdims=True) + EPS)
    return r * g


def _qkv_rope_kernel(x_ref, g_ref, w_ref, b_ref, cos_ref, sin_ref, q_ref, kv_ref):
    xn = _rmsnorm(x_ref[...], g_ref[...]).astype(BF16)
    y = jnp.dot(xn, w_ref[...], preferred_element_type=F32) + b_ref[...]
    cos = cos_ref[...]
    sin = sin_ref[...]
    lane = lax.broadcasted_iota(jnp.int32, cos.shape, 1)
    first_half = (lane & (HEAD_DIM - 1)) < (HEAD_DIM // 2)
    n_rot = (Q_W + KV_W) // LANES
    for c in range(n_rot):
        blk = y[:, c * LANES:(c + 1) * LANES]
        sw = jnp.where(first_half, pltpu.roll(blk, LANES - HEAD_DIM // 2, 1),
                       pltpu.roll(blk, HEAD_DIM // 2, 1))
        r = blk * cos + sw * sin
        if c < Q_W // LANES:
            q_ref[:, c * LANES:(c + 1) * LANES] = (r * (HEAD_DIM ** -0.5)).astype(q_ref.dtype)
        else:
            o = c * LANES - Q_W
            kv_ref[:, o:o + LANES] = r
    kv_ref[:, KV_W:2 * KV_W] = y[:, Q_W + KV_W:Q_W + 2 * KV_W]


def _qkv_rope(x, g, w, b, cos, sin, tm, q_dtype):
    m = x.shape[0]
    n = w.shape[1]
    n_pos_tiles = cos.shape[0] // tm
    return pl.pallas_call(
        _qkv_rope_kernel,
        grid=(m // tm,),
        in_specs=[
            pl.BlockSpec((tm, D_MODEL), lambda i: (i, 0)),
            pl.BlockSpec((1, D_MODEL), lambda i: (0, 0)),
            pl.BlockSpec((D_MODEL, n), lambda i: (0, 0)),
            pl.BlockSpec((1, n), lambda i: (0, 0)),
            pl.BlockSpec((tm, LANES), lambda i: (i % n_pos_tiles, 0)),
            pl.BlockSpec((tm, LANES), lambda i: (i % n_pos_tiles, 0)),
        ],
        out_specs=[
            pl.BlockSpec((tm, Q_W), lambda i: (i, 0)),
            pl.BlockSpec((tm, 2 * KV_W), lambda i: (i, 0)),
        ],
        out_shape=[
            jax.ShapeDtypeStruct((m, Q_W), q_dtype),
            jax.ShapeDtypeStruct((m, 2 * KV_W), F32),
        ],
        compiler_params=_params("arbitrary"),
        name="qkv_rope",
    )(x, g, w, b, cos, sin)


def _attn_prompt_kernel(sinks_ref, q_ref, kvp_ref, kvc_ref, wu_ref, wd_ref, o_ref, wu_bf_ref, wd_bf_ref):
    wu_bf_ref[...] = wu_ref[...].astype(BF16)
    wd_bf_ref[...] = wd_ref[...].astype(BF16)
    n = pl.program_id(1)
    w = WINDOW
    ri = lax.broadcasted_iota(jnp.int32, (w, 2 * w), 0)
    ci = lax.broadcasted_iota(jnp.int32, (w, 2 * w), 1)
    has_prev = jnp.full((w, 2 * w), n, jnp.int32) > 0
    allowed = ((ci < w) & (ci > ri) & has_prev) | ((ci >= w) & ((ci - w) <= ri))
    low = lax.broadcasted_iota(jnp.int32, (w, LANES), 1) < HEAD_DIM
    ones = jnp.ones((2 * w, HEAD_DIM), BF16)
    for g in range(N_KV_HEADS):
        ks = slice(g * HEAD_DIM, (g + 1) * HEAD_DIM)
        vs = slice(KV_W + g * HEAD_DIM, KV_W + (g + 1) * HEAD_DIM)
        k = jnp.concatenate([kvp_ref[:, ks], kvc_ref[:, ks]], axis=0).astype(BF16)
        v = jnp.concatenate([kvp_ref[:, vs], kvc_ref[:, vs]], axis=0).astype(BF16)
        v1 = jnp.concatenate([v, ones], axis=1)
        q = jnp.concatenate(
            [q_ref[:, (g * GROUP + h) * HEAD_DIM:(g * GROUP + h + 1) * HEAD_DIM] for h in range(GROUP)],
            axis=0)
        s = lax.dot_general(q, k, (((1,), (1,)), ((), ())), preferred_element_type=F32)
        ps, es = [], []
        for h in range(GROUP):
            sh = jnp.where(allowed, s[h * w:(h + 1) * w], NEG_INF)
            sink = sinks_ref[g * GROUP + h]
            m = jnp.maximum(jnp.max(sh, axis=-1, keepdims=True), sink)
            ps.append(jnp.exp(sh - m).astype(BF16))
            es.append(jnp.exp(sink - m))
        oa = jnp.dot(jnp.concatenate(ps, axis=0), v1, preferred_element_type=F32)
        for h in range(0, GROUP, 2):
            a = oa[h * w:(h + 1) * w]
            b = oa[(h + 1) * w:(h + 2) * w]
            num = jnp.where(low, a, pltpu.roll(b, HEAD_DIM, 1))
            den = jnp.where(low, pltpu.roll(a, HEAD_DIM, 1) + es[h], b + es[h + 1])
            c0 = (g * GROUP + h) * HEAD_DIM
            o_ref[:, c0:c0 + LANES] = (num / den).astype(o_ref.dtype)


def _attn_prompt(sinks, q, kv, batch, seq, w_up, w_down, layer):
    nb = seq // WINDOW
    steps = batch * nb
    up_rows = w_up.shape[1] // steps
    down_rows = w_down.shape[1] // steps
    up_in = pl.BlockSpec((1, up_rows, w_up.shape[2]), lambda b, n: (layer, b * nb + n, 0))
    down_in = pl.BlockSpec((1, down_rows, w_down.shape[2]), lambda b, n: (layer, b * nb + n, 0))
    up_out = pl.BlockSpec((1, up_rows, w_up.shape[2]), lambda b, n: (0, b * nb + n, 0))
    down_out = pl.BlockSpec((1, down_rows, w_down.shape[2]), lambda b, n: (0, b * nb + n, 0))
    return pl.pallas_call(
        _attn_prompt_kernel,
        grid=(batch, nb),
        in_specs=[
            pl.BlockSpec(memory_space=pltpu.SMEM),
            pl.BlockSpec((WINDOW, Q_W), lambda b, n: (b * nb + n, 0)),
            pl.BlockSpec((WINDOW, 2 * KV_W), lambda b, n: (b * nb + jnp.maximum(n - 1, 0), 0)),
            pl.BlockSpec((WINDOW, 2 * KV_W), lambda b, n: (b * nb + n, 0)),
            up_in,
            down_in,
        ],
        out_specs=[pl.BlockSpec((WINDOW, Q_W), lambda b, n: (b * nb + n, 0)), up_out, down_out],
        out_shape=[
            jax.ShapeDtypeStruct((batch * seq, Q_W), BF16),
            jax.ShapeDtypeStruct((1,) + w_up.shape[1:], BF16),
            jax.ShapeDtypeStruct((1,) + w_down.shape[1:], BF16),
        ],
        compiler_params=_params("arbitrary", "arbitrary"),
        name="attn_prompt",
    )(sinks, q, kv, kv, w_up, w_down)


def _attn_sample_kernel(sinks_ref, q_ref, kvn_ref, ck_ref, cv_ref, o_ref, ko_ref, vo_ref):
    t = q_ref.shape[0]
    wc = ck_ref.shape[1]
    rows = N_HEADS * t
    ri = lax.broadcasted_iota(jnp.int32, (rows, wc), 0) & (t - 1)
    ci = lax.broadcasted_iota(jnp.int32, (rows, wc), 1)
    allowed_c = ci > ri
    ri_n = lax.broadcasted_iota(jnp.int32, (rows, t), 0) & (t - 1)
    ci_n = lax.broadcasted_iota(jnp.int32, (rows, t), 1)
    allowed_n = ci_n <= ri_n
    nt = (((1,), (1,)), ((), ()))
    s_c, s_n = [], []
    for g in range(N_KV_HEADS):
        ks = slice(g * HEAD_DIM, (g + 1) * HEAD_DIM)
        q = jnp.concatenate(
            [q_ref[:, (g * GROUP + h) * HEAD_DIM:(g * GROUP + h + 1) * HEAD_DIM] for h in range(GROUP)],
            axis=0)
        s_c.append(lax.dot_general(q.astype(BF16), ck_ref[0, :, ks].astype(BF16), nt, preferred_element_type=F32))
        s_n.append(lax.dot_general(q, kvn_ref[:, ks], nt, preferred_element_type=F32))
    s_c = jnp.where(allowed_c, jnp.concatenate(s_c, axis=0), NEG_INF)
    s_n = jnp.where(allowed_n, jnp.concatenate(s_n, axis=0), NEG_INF)
    sink = jnp.concatenate([jnp.full((t, 1), sinks_ref[h], F32) for h in range(N_HEADS)], axis=0)
    m = jnp.maximum(jnp.maximum(jnp.max(s_c, axis=-1, keepdims=True),
                                jnp.max(s_n, axis=-1, keepdims=True)), sink)
    p_c = jnp.exp(s_c - m)
    p_n = jnp.exp(s_n - m)
    inv_l = 1.0 / (jnp.sum(p_c, axis=-1, keepdims=True) + jnp.sum(p_n, axis=-1, keepdims=True)
                   + jnp.exp(sink - m))
    p_c = (p_c * inv_l).astype(BF16)
    p_n = p_n * inv_l
    gr = GROUP * t
    for g in range(N_KV_HEADS):
        ks = slice(g * HEAD_DIM, (g + 1) * HEAD_DIM)
        vs = slice(KV_W + g * HEAD_DIM, KV_W + (g + 1) * HEAD_DIM)
        o = (jnp.dot(p_c[g * gr:(g + 1) * gr], cv_ref[0, :, ks].astype(BF16), preferred_element_type=F32)
             + jnp.dot(p_n[g * gr:(g + 1) * gr], kvn_ref[:, vs], preferred_element_type=F32))
        for h in range(GROUP):
            c0 = (g * GROUP + h) * HEAD_DIM
            o_ref[:, c0:c0 + HEAD_DIM] = o[h * t:(h + 1) * t]
    ko_ref[0, 0:wc - t, :] = ck_ref[0, t:wc, :]
    ko_ref[0, wc - t:wc, :] = kvn_ref[:, 0:KV_W]
    vo_ref[0, 0:wc - t, :] = cv_ref[0, t:wc, :]
    vo_ref[0, wc - t:wc, :] = kvn_ref[:, KV_W:2 * KV_W]


def _attn_sample(sinks, q, kvn, ck, cv, t):
    db = ck.shape[0]
    wc = ck.shape[1]
    return pl.pallas_call(
        _attn_sample_kernel,
        grid=(db,),
        in_specs=[
            pl.BlockSpec(memory_space=pltpu.SMEM),
            pl.BlockSpec((t, Q_W), lambda b: (b, 0)),
            pl.BlockSpec((t, 2 * KV_W), lambda b: (b, 0)),
            pl.BlockSpec((1, wc, KV_W), lambda b: (b, 0, 0)),
            pl.BlockSpec((1, wc, KV_W), lambda b: (b, 0, 0)),
        ],
        out_specs=[
            pl.BlockSpec((t, Q_W), lambda b: (b, 0)),
            pl.BlockSpec((1, wc, KV_W), lambda b: (b, 0, 0)),
            pl.BlockSpec((1, wc, KV_W), lambda b: (b, 0, 0)),
        ],
        out_shape=[
            jax.ShapeDtypeStruct((db * t, Q_W), F32),
            jax.ShapeDtypeStruct((db, wc, KV_W), F32),
            jax.ShapeDtypeStruct((db, wc, KV_W), F32),
        ],
        compiler_params=_params("arbitrary"),
        name="attn_sample",
    )(sinks, q, kvn, ck, cv)


def _proj_post_kernel(has_bias, a_ref, w_ref, b_ref, x_ref, g_ref, gn_ref, o_ref, xn_ref):
    y = jnp.dot(a_ref[...].astype(BF16), w_ref[...], preferred_element_type=F32)
    if has_bias:
        y = y + b_ref[...]
    x_new = x_ref[...] + _rmsnorm(y, g_ref[...])
    o_ref[...] = x_new
    xn_ref[...] = _rmsnorm(x_new, gn_ref[...]).astype(xn_ref.dtype)


def _proj_post(a, w, b, x, g, g_next, tm):
    m, k = a.shape
    has_bias = b is not None
    if b is None:
        b = jnp.zeros((1, D_MODEL), F32)
    return pl.pallas_call(
        functools.partial(_proj_post_kernel, has_bias),
        grid=(m // tm,),
        in_specs=[
            pl.BlockSpec((tm, k), lambda i: (i, 0)),
            pl.BlockSpec((k, D_MODEL), lambda i: (0, 0)),
            pl.BlockSpec((1, D_MODEL), lambda i: (0, 0)),
            pl.BlockSpec((tm, D_MODEL), lambda i: (i, 0)),
            pl.BlockSpec((1, D_MODEL), lambda i: (0, 0)),
            pl.BlockSpec((1, D_MODEL), lambda i: (0, 0)),
        ],
        out_specs=[pl.BlockSpec((tm, D_MODEL), lambda i: (i, 0)), pl.BlockSpec((tm, D_MODEL), lambda i: (i, 0))],
        out_shape=[jax.ShapeDtypeStruct((m, D_MODEL), F32), jax.ShapeDtypeStruct((m, D_MODEL), BF16)],
        compiler_params=_params("arbitrary"),
        name="proj_post",
    )(a, w, b, x, g, g_next)


_NT = (((1,), (1,)), ((), ()))


def _norm_matmul_kernel(has_bias, with_kt, x_ref, g_ref, wt_ref, b_ref, *rest):
    if with_kt:
        wkt_ref, o_ref, kt_ref, xn_ref = rest
    else:
        o_ref, xn_ref = rest

    @pl.when(pl.program_id(1) == 0)
    def _():
        xn = _rmsnorm(x_ref[...], g_ref[...]).astype(BF16)
        xn_ref[...] = xn
        if with_kt:
            kt_ref[...] = lax.dot_general(wkt_ref[...], xn, _NT, preferred_element_type=F32)

    y = lax.dot_general(xn_ref[...], wt_ref[...], _NT, preferred_element_type=F32)
    if has_bias:
        y = y + b_ref[...]
    o_ref[...] = y


def _norm_matmul(x, g, wt, b, tm, tn, wkt=None):
    m = x.shape[0]
    n = wt.shape[0]
    has_bias = b is not None
    with_kt = wkt is not None
    if b is None:
        b = jnp.zeros((1, n), F32)
    in_specs = [
        pl.BlockSpec((tm, D_MODEL), lambda i, j: (i, 0)),
        pl.BlockSpec((1, D_MODEL), lambda i, j: (0, 0)),
        pl.BlockSpec((tn, D_MODEL), lambda i, j: (j, 0)),
        pl.BlockSpec((1, tn), lambda i, j: (0, j)),
    ]
    out_specs = [pl.BlockSpec((tm, tn), lambda i, j: (i, j))]
    out_shape = [jax.ShapeDtypeStruct((m, n), F32)]
    args = [x, g, wt, b]
    if with_kt:
        nk = wkt.shape[0]
        in_specs.append(pl.BlockSpec((nk, D_MODEL), lambda i, j: (0, 0)))
        out_specs.append(pl.BlockSpec((nk, tm), lambda i, j: (0, i)))
        out_shape.append(jax.ShapeDtypeStruct((nk, m), F32))
        args.append(wkt)
    out = pl.pallas_call(
        functools.partial(_norm_matmul_kernel, has_bias, with_kt),
        grid=(m // tm, n // tn),
        in_specs=in_specs,
        out_specs=out_specs,
        out_shape=out_shape,
        scratch_shapes=[pltpu.VMEM((tm, D_MODEL), BF16)],
        compiler_params=_params("arbitrary", "arbitrary"),
        name="norm_matmul",
    )(*args)
    return out if with_kt else out[0]


def _log_sigmoid(x):
    return jnp.minimum(x, 0.0) - jnp.log1p(jnp.exp(-jnp.abs(x)))


def _chunk_gates(li, lf, m_state):
    L = li.shape[0]
    ri = lax.broadcasted_iota(jnp.int32, (L, L), 0)
    ci = lax.broadcasted_iota(jnp.int32, (L, L), 1)
    eye = ri == ci
    tril = ci <= ri
    lf_row = jnp.sum(jnp.where(eye, lf, 0.0), axis=0, keepdims=True)
    li_row = jnp.sum(jnp.where(eye, li, 0.0), axis=0, keepdims=True)
    b_col = jnp.sum(jnp.where(tril, lf_row, 0.0), axis=1, keepdims=True)
    b_row = jnp.sum(jnp.where(ri <= ci, lf, 0.0), axis=0, keepdims=True)
    dmat = jnp.where(tril, b_col - b_row + li_row, -jnp.inf)
    inter = b_col + m_state
    m_row = jnp.maximum(inter, jnp.max(dmat, axis=1, keepdims=True))
    b_last = b_col[L - 1:L, :]
    gk_col = b_last - b_col + li
    gk_row = b_last - b_row + li_row
    m_new = jnp.maximum(b_last + m_state, jnp.max(gk_col, axis=0, keepdims=True))
    decay = jnp.exp(b_last + m_state - m_new)
    return jnp.exp(dmat - m_row), jnp.exp(inter - m_row), m_row, gk_col, gk_row, m_new, decay


def _mlstm_chunk(q, k, v, li, lf, c_state, n_state, m_state):
    dexp, w_inter, m_row, gk_col, _, m_new, decay = _chunk_gates(li, lf, m_state)
    s = lax.dot_general(q, k, (((1,), (1,)), ((), ())), preferred_element_type=F32)
    sm = s * dexp
    num = (jnp.dot(sm, v, preferred_element_type=F32)
           + w_inter * jnp.dot(q, c_state, preferred_element_type=F32))
    den = jnp.sum(sm, axis=1, keepdims=True) + w_inter * jnp.sum(q * n_state, axis=1, keepdims=True)
    h = num * (1.0 / jnp.maximum(jnp.abs(den), jnp.exp(-m_row)))
    kw = k * jnp.exp(gk_col - m_new)
    c_new = decay * c_state + lax.dot_general(kw, v, (((0,), (0,)), ((), ())), preferred_element_type=F32)
    n_new = decay * n_state + jnp.sum(kw, axis=0, keepdims=True)
    return h, c_new, n_new, m_new


def _mlstm_chunk_t(q, kt, v1, li, lf, c_aug, m_state):
    dk = q.shape[1]
    dv = v1.shape[1] - LANES
    scale = dk ** -0.5
    dexp, w_inter, m_row, _, gk_row, m_new, decay = _chunk_gates(li, lf, m_state)
    qm = q.astype(BF16)
    s = jnp.dot(qm, kt.astype(BF16), preferred_element_type=F32)
    sm = (s * scale) * dexp
    num = (jnp.dot(sm.astype(BF16), v1, preferred_element_type=F32)
           + (w_inter * scale) * jnp.dot(qm, c_aug.astype(BF16), preferred_element_type=F32))
    den = num[:, dv:dv + 1]
    h = num[:, :dv] * (1.0 / jnp.maximum(jnp.abs(den), jnp.exp(-m_row)))
    kwt = (kt * jnp.exp(gk_row - m_new)).astype(BF16)
    c_new = decay * c_aug + jnp.dot(kwt, v1, preferred_element_type=F32)
    return h, c_new, m_new


def _head_cols(hd):
    q0 = hd * M_QK_DIM
    v0 = QK_W + hd * M_V_DIM
    o0 = QK_W + V_W + hd * M_V_DIM
    g0 = QK_W + 2 * V_W
    return (q0, q0 + M_QK_DIM), (v0, v0 + M_V_DIM), (o0, o0 + M_V_DIM), g0 + hd, g0 + M_HEADS + hd


def _head_out(h, o, ng):
    return _rmsnorm(h, ng) * (0.5 * (jnp.tanh(0.5 * o) + 1.0))


def _mlstm_prompt_kernel(batch, proj_ref, *rest):
    kt_refs = rest[:batch]
    ng_ref, h_ref, c_ref, m_ref = rest[batch:]

    @pl.when(pl.program_id(0) == 0)
    def _():
        c_ref[...] = jnp.zeros(c_ref.shape, F32)
        m_ref[...] = jnp.zeros(m_ref.shape, F32)

    L = proj_ref.shape[1]
    one_col = (lax.broadcasted_iota(jnp.int32, (L, LANES), 1) == 0).astype(BF16)
    for b in range(batch):
        for hd in range(M_HEADS):
            r = b * M_HEADS + hd
            (q0, q1), (v0, v1), (o0, o1), gi, gf = _head_cols(hd)
            va = jnp.concatenate([proj_ref[b, :, v0:v1].astype(BF16), one_col], axis=1)
            h, c_new, m_new = _mlstm_chunk_t(
                proj_ref[b, :, q0:q1], kt_refs[b][hd * M_QK_DIM:(hd + 1) * M_QK_DIM, :], va,
                proj_ref[b, :, gi:gi + 1], _log_sigmoid(proj_ref[b, :, gf:gf + 1]),
                c_ref[r], m_ref[r:r + 1, 0:1])
            hn = _head_out(h, proj_ref[b, :, o0:o1], ng_ref[:, hd * M_V_DIM:(hd + 1) * M_V_DIM])
            h_ref[b, :, hd * M_V_DIM:(hd + 1) * M_V_DIM] = hn.astype(h_ref.dtype)
            c_ref[r] = c_new
            m_ref[r:r + 1, :] = jnp.broadcast_to(m_new, (1, LANES))


def _mlstm_prompt(proj, kt, ng, batch, seq):
    chunk = MLSTM_CHUNK
    nc = seq // chunk
    rows = batch * M_HEADS
    return pl.pallas_call(
        functools.partial(_mlstm_prompt_kernel, batch),
        grid=(nc,),
        in_specs=[pl.BlockSpec((batch, chunk, PROJ_W), lambda c: (0, c, 0))]
        + [pl.BlockSpec((QK_W, chunk), lambda c, b=b: (0, b * nc + c)) for b in range(batch)]
        + [pl.BlockSpec((1, V_W), lambda c: (0, 0))],
        out_specs=[
            pl.BlockSpec((batch, chunk, V_W), lambda c: (0, c, 0)),
            pl.BlockSpec((rows, M_QK_DIM, M_V_DIM + LANES), lambda c: (0, 0, 0)),
            pl.BlockSpec((rows, LANES), lambda c: (0, 0)),
        ],
        out_shape=[
            jax.ShapeDtypeStruct((batch, seq, V_W), BF16),
            jax.ShapeDtypeStruct((rows, M_QK_DIM, M_V_DIM + LANES), F32),
            jax.ShapeDtypeStruct((rows, LANES), F32),
        ],
        compiler_params=_params("arbitrary"),
        name="mlstm_prompt",
    )(proj.reshape(batch, seq, PROJ_W), *([kt] * batch), ng)


def _mlstm_sample_kernel(t, proj_ref, k_ref, ng_ref, c0_ref, n0_ref, m0_ref, h_ref, c_ref, n_ref, m_ref):
    for s in range(c0_ref.shape[0]):
        rows = slice(s * t, (s + 1) * t)
        for hd in range(M_HEADS):
            (q0, q1), (v0, v1), (o0, o1), gi, gf = _head_cols(hd)
            h, c_new, n_new, m_new = _mlstm_chunk(
                proj_ref[rows, q0:q1] * (M_QK_DIM ** -0.5), k_ref[rows, hd * M_QK_DIM:(hd + 1) * M_QK_DIM],
                proj_ref[rows, v0:v1], proj_ref[rows, gi:gi + 1], _log_sigmoid(proj_ref[rows, gf:gf + 1]),
                c0_ref[s, hd], n0_ref[s, hd:hd + 1, :], m0_ref[s, :, hd:hd + 1])
            h_ref[rows, hd * M_V_DIM:(hd + 1) * M_V_DIM] = _head_out(
                h, proj_ref[rows, o0:o1], ng_ref[:, hd * M_V_DIM:(hd + 1) * M_V_DIM])
            c_ref[s, hd] = c_new
            n_ref[s, hd:hd + 1, :] = n_new
            m_ref[s, :, hd:hd + 1] = m_new


def _mlstm_sample(proj, k, ng, c0, n0, m0, t):
    db = c0.shape[0]
    sb = MLSTM_SAMPLE_SEQS
    m0 = m0.reshape(db, 1, M_HEADS)
    state_specs = [
        pl.BlockSpec((sb, M_HEADS, M_QK_DIM, M_V_DIM), lambda b: (b, 0, 0, 0)),
        pl.BlockSpec((sb, M_HEADS, M_QK_DIM), lambda b: (b, 0, 0)),
        pl.BlockSpec((sb, 1, M_HEADS), lambda b: (b, 0, 0)),
    ]
    return pl.pallas_call(
        functools.partial(_mlstm_sample_kernel, t),
        grid=(db // sb,),
        in_specs=[
            pl.BlockSpec((sb * t, PROJ_W), lambda b: (b, 0)),
            pl.BlockSpec((sb * t, QK_W), lambda b: (b, 0)),
            pl.BlockSpec((1, V_W), lambda b: (0, 0)),
        ] + state_specs,
        out_specs=[pl.BlockSpec((sb * t, V_W), lambda b: (b, 0))] + state_specs,
        out_shape=[
            jax.ShapeDtypeStruct((db * t, V_W), F32),
            jax.ShapeDtypeStruct(c0.shape, F32),
            jax.ShapeDtypeStruct(n0.shape, F32),
            jax.ShapeDtypeStruct(m0.shape, F32),
        ],
        compiler_params=_params("arbitrary"),
        name="mlstm_sample",
    )(proj, k, ng, c0, n0, m0)


def _conv3(u, u1, u2, w, b):
    return ((b + w[0:1] * u2) + w[1:2] * u1) + w[2:3] * u


def _conv_stream(u, tail, w, b):
    u1 = pltpu.roll(u, 1, 0)
    u2 = pltpu.roll(u, 2, 0)
    c = _conv3(u, u1, u2, w, b)
    uf = u[0:SUBLANES]
    row = lax.broadcasted_iota(jnp.int32, uf.shape, 0)
    uf1 = jnp.where(row < 1, pltpu.roll(tail, 1, 0), pltpu.roll(uf, 1, 0))
    uf2 = jnp.where(row < 2, pltpu.roll(tail, 2, 0), pltpu.roll(uf, 2, 0))
    cf = _conv3(uf, uf1, uf2, w, b)
    return jnp.concatenate([cf, c[SUBLANES:]], axis=0)


def _conv_seq8(u, prev, w, b):
    nseq = u.shape[0] // SUBLANES
    p0 = jnp.broadcast_to(prev[:, 0:1, :], (nseq, SUBLANES, u.shape[1])).reshape(u.shape)
    p1 = jnp.broadcast_to(prev[:, 1:2, :], (nseq, SUBLANES, u.shape[1])).reshape(u.shape)
    row = lax.broadcasted_iota(jnp.int32, u.shape, 0) & (SUBLANES - 1)
    u1 = jnp.where(row == 0, p1, pltpu.roll(u, 1, 0))
    u2 = jnp.where(row == 0, p0, jnp.where(row == 1, p1, pltpu.roll(u, 2, 0)))
    return _conv3(u, u1, u2, w, b)


def _ffn_kernel(stream, hosts_cast, tiles_per_seq, x_ref, xn_ref, wg_ref, wv_ref, cw_ref, cb_ref, wd_ref, gpost_ref,
                *rest):
    if hosts_cast:
        nu_ref, nd_ref, o_ref, sg_ref, sv_ref, nu_bf_ref, nd_bf_ref, tail_g, tail_v = rest
        nu_bf_ref[...] = nu_ref[...].astype(BF16)
        nd_bf_ref[...] = nd_ref[...].astype(BF16)
    elif stream:
        o_ref, sg_ref, sv_ref, tail_g, tail_v = rest
    else:
        pg_ref, pv_ref, o_ref, sg_ref, sv_ref = rest
    i = pl.program_id(0)
    j = pl.program_id(1)
    nf = pl.num_programs(1)
    tm = x_ref.shape[0]

    if stream:
        @pl.when(i % tiles_per_seq == 0)
        def _():
            tail_g[j] = jnp.zeros(tail_g.shape[1:], F32)
            tail_v[j] = jnp.zeros(tail_v.shape[1:], F32)

    xn = xn_ref[...]
    ug = jnp.dot(xn, wg_ref[...], preferred_element_type=F32)
    uv = jnp.dot(xn, wv_ref[...], preferred_element_type=F32)
    cwg, cwv, cbg, cbv = cw_ref[j], cw_ref[j + nf], cb_ref[j], cb_ref[j + nf]
    if stream:
        cg = _conv_stream(ug, tail_g[j], cwg, cbg)
        cv = _conv_stream(uv, tail_v[j], cwv, cbv)
        tail_g[j] = ug[tm - SUBLANES:]
        tail_v[j] = uv[tm - SUBLANES:]
        sg_ref[0, j] = ug[tm - SUBLANES:]
        sv_ref[0, j] = uv[tm - SUBLANES:]
    else:
        cg = _conv_seq8(ug, pg_ref[...], cwg, cbg)
        cv = _conv_seq8(uv, pv_ref[...], cwv, cbv)
        sg_ref[...] = ug.reshape(sg_ref.shape)
        sv_ref[...] = uv.reshape(sv_ref.shape)
    h = (jax.nn.gelu(cg, approximate=True) * cv).astype(BF16)
    acc = jnp.where(j == 0, 0.0, o_ref[...])
    o_ref[...] = acc + jnp.dot(h, wd_ref[...], preferred_element_type=F32)

    @pl.when(j == nf - 1)
    def _():
        o_ref[...] = x_ref[...] + _rmsnorm(o_ref[...], gpost_ref[...])


def _ffn(x, xn, prev, layer, w_up, conv_w, conv_b, w_down, gpost, tm, tf, seq, cast_next=None):
    m = x.shape[0]
    nf = D_FF // tf
    stream = prev is None
    n_layers = conv_w.shape[0]
    conv_w = conv_w.reshape(n_layers, CONV_W, 2 * nf, tf).transpose(0, 2, 1, 3)
    conv_b = conv_b.reshape(n_layers, 2 * nf, 1, tf)
    in_specs = [
        pl.BlockSpec((tm, D_MODEL), lambda i, j: (i, 0)),
        pl.BlockSpec((tm, D_MODEL), lambda i, j: (i, 0)),
        pl.BlockSpec((None, D_MODEL, tf), lambda i, j: (layer, 0, j)),
        pl.BlockSpec((None, D_MODEL, tf), lambda i, j: (layer, 0, j + nf)),
        pl.BlockSpec((None, 2 * nf, CONV_W, tf), lambda i, j: (layer, 0, 0, 0)),
        pl.BlockSpec((None, 2 * nf, 1, tf), lambda i, j: (layer, 0, 0, 0)),
        pl.BlockSpec((None, tf, D_MODEL), lambda i, j: (layer, j, 0)),
        pl.BlockSpec((1, D_MODEL), lambda i, j: (0, 0)),
    ]
    args = [x, xn, w_up, w_up, conv_w, conv_b, w_down, gpost]
    scratch = []
    if stream:
        tiles_per_seq = seq // tm
        state_shape = (m // tm, nf, SUBLANES, tf)
        state_spec = pl.BlockSpec((1, nf, SUBLANES, tf), lambda i, j: (i, 0, 0, 0))
        scratch += [pltpu.VMEM((nf, SUBLANES, tf), F32), pltpu.VMEM((nf, SUBLANES, tf), F32)]
    else:
        assert seq == SUBLANES and m == tm
        tiles_per_seq = 1
        nseq = m // seq
        state_shape = (nseq, SUBLANES, D_FF)
        state_spec = pl.BlockSpec((nseq, SUBLANES, tf), lambda i, j: (0, 0, j))
        in_specs += [
            pl.BlockSpec((nseq, CONV_W - 1, tf), lambda i, j: (0, 0, j)),
            pl.BlockSpec((nseq, CONV_W - 1, tf), lambda i, j: (0, 0, j + nf)),
        ]
        args += [prev, prev]
    out_specs = [pl.BlockSpec((tm, D_MODEL), lambda i, j: (i, 0)), state_spec, state_spec]
    out_shape = [
        jax.ShapeDtypeStruct((m, D_MODEL), F32),
        jax.ShapeDtypeStruct(state_shape, F32),
        jax.ShapeDtypeStruct(state_shape, F32),
    ]
    if cast_next is not None:
        assert stream
        nu, nd, layer_next = cast_next
        steps = (m // tm) * nf
        for w in (nu, nd):
            rows = w.shape[1] // steps
            in_specs.append(pl.BlockSpec((1, rows, w.shape[2]), lambda i, j: (layer_next, i * nf + j, 0)))
            out_specs.append(pl.BlockSpec((1, rows, w.shape[2]), lambda i, j: (0, i * nf + j, 0)))
            out_shape.append(jax.ShapeDtypeStruct((1,) + w.shape[1:], BF16))
            args.append(w)
    return pl.pallas_call(
        functools.partial(_ffn_kernel, stream, cast_next is not None, tiles_per_seq),
        grid=(m // tm, nf),
        in_specs=in_specs,
        out_specs=out_specs,
        out_shape=out_shape,
        scratch_shapes=scratch,
        compiler_params=_params("arbitrary", "arbitrary"),
        name="conv_ffn",
    )(*args)


def _rope_tables(pos):
    half = HEAD_DIM // 2
    inv = ROPE_THETA ** (-jnp.arange(half, dtype=F32) / half)
    ang = pos.astype(F32)[:, None] * inv[None, :]
    cos = jnp.cos(ang)
    sin = jnp.sin(ang)
    reps = LANES // HEAD_DIM
    return (jnp.tile(cos, (1, 2 * reps)), jnp.tile(jnp.concatenate([-sin, sin], axis=1), (1, reps)))


def _conv_state(sg, sv, tiles_per_seq):
    keep = slice(SUBLANES - (CONV_W - 1), SUBLANES)
    if tiles_per_seq is None:
        return jnp.concatenate([sg[:, keep], sv[:, keep]], axis=-1)
    last = slice(tiles_per_seq - 1, None, tiles_per_seq)

    def rows(s):
        s = s[last, :, keep, :]
        return jnp.swapaxes(s, 1, 2).reshape(s.shape[0], CONV_W - 1, -1)

    return jnp.concatenate([rows(sg), rows(sv)], axis=-1)


def _trunk(x, pos, seq, tm, wide_tm, cache, state, state_conv, p):
    m = x.shape[0]
    nseq = m // seq
    sample = cache is not None
    row = lambda v: v.reshape(1, -1)

    cos, sin = _rope_tables(pos)
    q, kv = _qkv_rope(x, row(p["norm_mix_pre"][0]), p["w_qkv"], row(p["attn_b_qkv"][0]), cos, sin, tm,
                      F32 if sample else BF16)
    if sample:
        ck, cv = cache
        o, k_new, v_new = _attn_sample(p["attn_sinks"][0], q, kv, ck.reshape(nseq, WINDOW, KV_W),
                                       cv.reshape(nseq, WINDOW, KV_W), seq)
    else:
        o, p["w_up0"], p["w_down0"] = _attn_prompt(p["attn_sinks"][0], q, kv, nseq, seq, p["ffn_w_up"],
                                                   p["ffn_w_down"], 0)
        kv3 = kv.reshape(nseq, seq, 2 * KV_W)
        k_new = kv3[:, seq - WINDOW:, :KV_W]
        v_new = kv3[:, seq - WINDOW:, KV_W:]
    k_new = k_new.reshape(1, nseq, WINDOW, N_KV_HEADS, HEAD_DIM)
    v_new = v_new.reshape(1, nseq, WINDOW, N_KV_HEADS, HEAD_DIM)
    x, xn = _proj_post(o, p["w_o"], row(p["attn_b_o"][0]), x, row(p["norm_mix_post"][0]),
                       row(p["norm_ffn_pre"][0]), tm)
    if sample:
        x, sg0, sv0 = _ffn(x, xn, state_conv[0], 0, p["w_up0"], p["ffn_conv_w"][0:1], p["ffn_conv_b"][0:1],
                           p["w_down0"], row(p["norm_ffn_post"][0]), tm, FF_TILE, seq)
    else:
        x, sg0, sv0, p["w_up1"], p["w_down1"] = _ffn(
            x, xn, None, 0, p["w_up0"], p["ffn_conv_w"][0:1], p["ffn_conv_b"][0:1], p["w_down0"],
            row(p["norm_ffn_post"][0]), tm, FF_TILE, seq, cast_next=(p["ffn_w_up"], p["ffn_w_down"], 1))

    g1 = row(p["norm_mix_pre"][1])
    ng = row(p["mlstm_norm"][0])
    if sample:
        c0, n0, m0 = state
        proj = _norm_matmul(x, g1, p["w_in_t"], p["b_in"], wide_tm, PROJ_N_TILE)
        k = _norm_matmul(x, g1, p["w_k_t"], None, wide_tm, QK_W)
        h, c_new, n_new, m_new = _mlstm_sample(proj, k, ng, c0, n0, m0, seq)
    else:
        proj, kt = _norm_matmul(x, g1, p["w_in_t"], p["b_in"], wide_tm, PROJ_N_TILE, wkt=p["w_k_t"])
        h, c_aug, m_new = _mlstm_prompt(proj, kt, ng, nseq, seq)
        h = h.reshape(m, V_W)
        c_new = c_aug[:, :, :M_V_DIM]
        n_new = c_aug[:, :, M_V_DIM]
        m_new = m_new[:, 0]
    c_new = c_new.reshape(1, nseq, M_HEADS, M_QK_DIM, M_V_DIM)
    n_new = n_new.reshape(1, nseq, M_HEADS, M_QK_DIM)
    m_new = m_new.reshape(1, nseq, M_HEADS)
    x, xn = _proj_post(h, p["w_out"], None, x, row(p["norm_mix_post"][1]), row(p["norm_ffn_pre"][1]), tm)
    x, sg1, sv1 = _ffn(x, xn, state_conv[1] if sample else None, 0, p["w_up1"], p["ffn_conv_w"][1:2],
                       p["ffn_conv_b"][1:2], p["w_down1"], row(p["norm_ffn_post"][1]), tm, FF_TILE, seq)
    tps = None if sample else seq // tm
    conv = jnp.stack([_conv_state(sg0, sv0, tps), _conv_state(sg1, sv1, tps)])
    return x, k_new, v_new, c_new, n_new, m_new, conv


def kernel(x_prompt, x_sample, cache_k, cache_v, state_C, state_n, state_m, state_conv, norm_mix_pre,
           norm_mix_post, norm_ffn_pre, norm_ffn_post, attn_w_qkv, attn_b_qkv, attn_w_o, attn_b_o, attn_sinks,
           mlstm_w_in, mlstm_b_gates, mlstm_norm, mlstm_w_out, ffn_w_up, ffn_conv_w, ffn_conv_b, ffn_w_down):
    batch, seq, _ = x_prompt.shape
    dec_batch, dec_seq, _ = x_sample.shape
    n_gates = 2 * M_HEADS
    w_in_t = mlstm_w_in[0].T
    kv0 = 2 * QK_W
    w_main_t = jnp.concatenate(
        [w_in_t[:QK_W], w_in_t[kv0:kv0 + 2 * V_W],
         jnp.pad(w_in_t[kv0 + 2 * V_W:], ((0, GATE_PAD - n_gates), (0, 0)))], axis=0)
    p = dict(
        norm_mix_pre=norm_mix_pre, norm_mix_post=norm_mix_post, norm_ffn_pre=norm_ffn_pre,
        norm_ffn_post=norm_ffn_post, attn_b_qkv=attn_b_qkv, attn_b_o=attn_b_o, attn_sinks=attn_sinks,
        mlstm_norm=mlstm_norm, ffn_conv_w=ffn_conv_w, ffn_conv_b=ffn_conv_b,
        w_qkv=attn_w_qkv[0].astype(BF16),
        w_o=attn_w_o[0].astype(BF16),
        w_in_t=w_main_t.astype(BF16),
        w_k_t=w_in_t[QK_W:kv0].astype(BF16),
        b_in=jnp.pad(mlstm_b_gates[0], (QK_W + 2 * V_W, GATE_PAD - n_gates)).reshape(1, PROJ_W),
        w_out=mlstm_w_out[0].astype(BF16),
        ffn_w_up=ffn_w_up, ffn_w_down=ffn_w_down,
    )
    yp, k_p, v_p, c_p, n_p, m_p, conv_p = _trunk(
        x_prompt.reshape(batch * seq, D_MODEL), jnp.arange(seq), seq, ROW_TILE, WIDE_ROW_TILE, None, None, None, p)
    ys, k_s, v_s, c_s, n_s, m_s, conv_s = _trunk(
        x_sample.reshape(dec_batch * dec_seq, D_MODEL),
        jnp.tile(PAST_LEN + jnp.arange(dec_seq), dec_batch), dec_seq, dec_batch * dec_seq, dec_batch * dec_seq,
        (cache_k[0], cache_v[0]), (state_C[0], state_n[0], state_m[0]), state_conv, p)
    return (yp.reshape(batch, seq, D_MODEL), ys.reshape(dec_batch, dec_seq, D_MODEL),
            k_p, v_p, k_s, v_s, c_p, n_p, m_p, c_s, n_s, m_s, conv_p, conv_s)
```

```python
import functools

import jax
import jax.numpy as jnp
from jax import lax
from jax.experimental import pallas as pl
from jax.experimental.pallas import tpu as pltpu

F32 = jnp.float32
BF16 = jnp.bfloat16

D_MODEL = 2048
WINDOW = 128
HEAD_DIM = 64
N_HEADS = 32
N_KV_HEADS = 4
GROUP = N_HEADS // N_KV_HEADS
Q_W = N_HEADS * HEAD_DIM
KV_W = N_KV_HEADS * HEAD_DIM
ROPE_THETA = 10000.0
M_HEADS = 4
M_QK_DIM = 256
M_V_DIM = 512
QK_W = M_HEADS * M_QK_DIM
V_W = M_HEADS * M_V_DIM
GATE_PAD = 128
PROJ_W = QK_W + 2 * V_W
D_FF = 4 * D_MODEL
CONV_W = 3
EPS = 1e-6
NEG_INF = -1e30
PAST_LEN = 16384

SUBLANES = 8
LANES = 128
VMEM_LIMIT_BYTES = 60 * 1024 * 1024

ROW_TILE = 512
WIDE_ROW_TILE = 1024
FF_TILE = 1024
PROJ_N_TILE = 1024
MLSTM_CHUNK = 256
MLSTM_SAMPLE_SEQS = 2


def _params(*sem):
    return pltpu.CompilerParams(dimension_semantics=sem, vmem_limit_bytes=VMEM_LIMIT_BYTES)


def _rmsnorm(xf, g):
    r = xf * lax.rsqrt(jnp.mean(xf * xf, axis=-1, keepdims=True) + EPS)
    return r * g


def _qkv_rope_kernel(x_ref, g_ref, w_ref, b_ref, cos_ref, sin_ref, q_ref, kv_ref):
    xn = _rmsnorm(x_ref[...], g_ref[...]).astype(BF16)
    y = jnp.dot(xn, w_ref[...], preferred_element_type=F32) + b_ref[...]
    cos = cos_ref[...]
    sin = sin_ref[...]
    lane = lax.broadcasted_iota(jnp.int32, cos.shape, 1)
    first_half = (lane & (HEAD_DIM - 1)) < (HEAD_DIM // 2)
    n_rot = (Q_W + KV_W) // LANES
    for c in range(n_rot):
        blk = y[:, c * LANES:(c + 1) * LANES]
        sw = jnp.where(first_half, pltpu.roll(blk, LANES - HEAD_DIM // 2, 1),
                       pltpu.roll(blk, HEAD_DIM // 2, 1))
        r = blk * cos + sw * sin
        if c < Q_W // LANES:
            q_ref[:, c * LANES:(c + 1) * LANES] = (r * (HEAD_DIM ** -0.5)).astype(q_ref.dtype)
        else:
            o = c * LANES - Q_W
            kv_ref[:, o:o + LANES] = r
    kv_ref[:, KV_W:2 * KV_W] = y[:, Q_W + KV_W:Q_W + 2 * KV_W]


def _qkv_rope(x, g, w, b, cos, sin, tm, q_dtype):
    m = x.shape[0]
    n = w.shape[1]
    n_pos_tiles = cos.shape[0] // tm
    return pl.pallas_call(
        _qkv_rope_kernel,
        grid=(m // tm,),
        in_specs=[
            pl.BlockSpec((tm, D_MODEL), lambda i: (i, 0)),
            pl.BlockSpec((1, D_MODEL), lambda i: (0, 0)),
            pl.BlockSpec((D_MODEL, n), lambda i: (0, 0)),
            pl.BlockSpec((1, n), lambda i: (0, 0)),
            pl.BlockSpec((tm, LANES), lambda i: (i % n_pos_tiles, 0)),
            pl.BlockSpec((tm, LANES), lambda i: (i % n_pos_tiles, 0)),
        ],
        out_specs=[
            pl.BlockSpec((tm, Q_W), lambda i: (i, 0)),
            pl.BlockSpec((tm, 2 * KV_W), lambda i: (i, 0)),
        ],
        out_shape=[
            jax.ShapeDtypeStruct((m, Q_W), q_dtype),
            jax.ShapeDtypeStruct((m, 2 * KV_W), F32),
        ],
        compiler_params=_params("arbitrary"),
        name="qkv_rope",
    )(x, g, w, b, cos, sin)


def _attn_prompt_kernel(sinks_ref, q_ref, kvp_ref, kvc_ref, wu_ref, wd_ref, o_ref, wu_bf_ref, wd_bf_ref):
    wu_bf_ref[...] = wu_ref[...].astype(BF16)
    wd_bf_ref[...] = wd_ref[...].astype(BF16)
    n = pl.program_id(1)
    w = WINDOW
    ri = lax.broadcasted_iota(jnp.int32, (w, 2 * w), 0)
    ci = lax.broadcasted_iota(jnp.int32, (w, 2 * w), 1)
    has_prev = jnp.full((w, 2 * w), n, jnp.int32) > 0
    allowed = ((ci < w) & (ci > ri) & has_prev) | ((ci >= w) & ((ci - w) <= ri))
    low = lax.broadcasted_iota(jnp.int32, (w, LANES), 1) < HEAD_DIM
    ones = jnp.ones((2 * w, HEAD_DIM), BF16)
    for g in range(N_KV_HEADS):
        ks = slice(g * HEAD_DIM, (g + 1) * HEAD_DIM)
        vs = slice(KV_W + g * HEAD_DIM, KV_W + (g + 1) * HEAD_DIM)
        k = jnp.concatenate([kvp_ref[:, ks], kvc_ref[:, ks]], axis=0).astype(BF16)
        v = jnp.concatenate([kvp_ref[:, vs], kvc_ref[:, vs]], axis=0).astype(BF16)
        v1 = jnp.concatenate([v, ones], axis=1)
        q = jnp.concatenate(
            [q_ref[:, (g * GROUP + h) * HEAD_DIM:(g * GROUP + h + 1) * HEAD_DIM] for h in range(GROUP)],
            axis=0)
        s = lax.dot_general(q, k, (((1,), (1,)), ((), ())), preferred_element_type=F32)
        ps, es = [], []
        for h in range(GROUP):
            sh = jnp.where(allowed, s[h * w:(h + 1) * w], NEG_INF)
            sink = sinks_ref[g * GROUP + h]
            m = jnp.maximum(jnp.max(sh, axis=-1, keepdims=True), sink)
            ps.append(jnp.exp(sh - m).astype(BF16))
            es.append(jnp.exp(sink - m))
        oa = jnp.dot(jnp.concatenate(ps, axis=0), v1, preferred_element_type=F32)
        for h in range(0, GROUP, 2):
            a = oa[h * w:(h + 1) * w]
            b = oa[(h + 1) * w:(h + 2) * w]
            num = jnp.where(low, a, pltpu.roll(b, HEAD_DIM, 1))
            den = jnp.where(low, pltpu.roll(a, HEAD_DIM, 1) + es[h], b + es[h + 1])
            c0 = (g * GROUP + h) * HEAD_DIM
            o_ref[:, c0:c0 + LANES] = (num / den).astype(o_ref.dtype)


def _attn_prompt(sinks, q, kv, batch, seq, w_up, w_down, layer):
    nb = seq // WINDOW
    steps = batch * nb
    up_rows = w_up.shape[1] // steps
    down_rows = w_down.shape[1] // steps
    up_in = pl.BlockSpec((1, up_rows, w_up.shape[2]), lambda b, n: (layer, b * nb + n, 0))
    down_in = pl.BlockSpec((1, down_rows, w_down.shape[2]), lambda b, n: (layer, b * nb + n, 0))
    up_out = pl.BlockSpec((1, up_rows, w_up.shape[2]), lambda b, n: (0, b * nb + n, 0))
    down_out = pl.BlockSpec((1, down_rows, w_down.shape[2]), lambda b, n: (0, b * nb + n, 0))
    return pl.pallas_call(
        _attn_prompt_kernel,
        grid=(batch, nb),
        in_specs=[
            pl.BlockSpec(memory_space=pltpu.SMEM),
            pl.BlockSpec((WINDOW, Q_W), lambda b, n: (b * nb + n, 0)),
            pl.BlockSpec((WINDOW, 2 * KV_W), lambda b, n: (b * nb + jnp.maximum(n - 1, 0), 0)),
            pl.BlockSpec((WINDOW, 2 * KV_W), lambda b, n: (b * nb + n, 0)),
            up_in,
            down_in,
        ],
        out_specs=[pl.BlockSpec((WINDOW, Q_W), lambda b, n: (b * nb + n, 0)), up_out, down_out],
        out_shape=[
            jax.ShapeDtypeStruct((batch * seq, Q_W), BF16),
            jax.ShapeDtypeStruct((1,) + w_up.shape[1:], BF16),
            jax.ShapeDtypeStruct((1,) + w_down.shape[1:], BF16),
        ],
        compiler_params=_params("arbitrary", "arbitrary"),
        name="attn_prompt",
    )(sinks, q, kv, kv, w_up, w_down)


def _attn_sample_kernel(sinks_ref, q_ref, kvn_ref, ck_ref, cv_ref, o_ref, ko_ref, vo_ref):
    t = q_ref.shape[0]
    wc = ck_ref.shape[1]
    rows = N_HEADS * t
    ri = lax.broadcasted_iota(jnp.int32, (rows, wc), 0) & (t - 1)
    ci = lax.broadcasted_iota(jnp.int32, (rows, wc), 1)
    allowed_c = ci > ri
    ri_n = lax.broadcasted_iota(jnp.int32, (rows, t), 0) & (t - 1)
    ci_n = lax.broadcasted_iota(jnp.int32, (rows, t), 1)
    allowed_n = ci_n <= ri_n
    nt = (((1,), (1,)), ((), ()))
    s_c, s_n = [], []
    for g in range(N_KV_HEADS):
        ks = slice(g * HEAD_DIM, (g + 1) * HEAD_DIM)
        q = jnp.concatenate(
            [q_ref[:, (g * GROUP + h) * HEAD_DIM:(g * GROUP + h + 1) * HEAD_DIM] for h in range(GROUP)],
            axis=0)
        s_c.append(lax.dot_general(q.astype(BF16), ck_ref[0, :, ks].astype(BF16), nt, preferred_element_type=F32))
        s_n.append(lax.dot_general(q, kvn_ref[:, ks], nt, preferred_element_type=F32))
    s_c = jnp.where(allowed_c, jnp.concatenate(s_c, axis=0), NEG_INF)
    s_n = jnp.where(allowed_n, jnp.concatenate(s_n, axis=0), NEG_INF)
    sink = jnp.concatenate([jnp.full((t, 1), sinks_ref[h], F32) for h in range(N_HEADS)], axis=0)
    m = jnp.maximum(jnp.maximum(jnp.max(s_c, axis=-1, keepdims=True),
                                jnp.max(s_n, axis=-1, keepdims=True)), sink)
    p_c = jnp.exp(s_c - m)
    p_n = jnp.exp(s_n - m)
    inv_l = 1.0 / (jnp.sum(p_c, axis=-1, keepdims=True) + jnp.sum(p_n, axis=-1, keepdims=True)
                   + jnp.exp(sink - m))
    p_c = (p_c * inv_l).astype(BF16)
    p_n = p_n * inv_l
    gr = GROUP * t
    for g in range(N_KV_HEADS):
        ks = slice(g * HEAD_DIM, (g + 1) * HEAD_DIM)
        vs = slice(KV_W + g * HEAD_DIM, KV_W + (g + 1) * HEAD_DIM)
        o = (jnp.dot(p_c[g * gr:(g + 1) * gr], cv_ref[0, :, ks].astype(BF16), preferred_element_type=F32)
             + jnp.dot(p_n[g * gr:(g + 1) * gr], kvn_ref[:, vs], preferred_element_type=F32))
        for h in range(GROUP):
            c0 = (g * GROUP + h) * HEAD_DIM
            o_ref[:, c0:c0 + HEAD_DIM] = o[h * t:(h + 1) * t]
    ko_ref[0, 0:wc - t, :] = ck_ref[0, t:wc, :]
    ko_ref[0, wc - t:wc, :] = kvn_ref[:, 0:KV_W]
    vo_ref[0, 0:wc - t, :] = cv_ref[0, t:wc, :]
    vo_ref[0, wc - t:wc, :] = kvn_ref[:, KV_W:2 * KV_W]


def _attn_sample(sinks, q, kvn, ck, cv, t):
    db = ck.shape[0]
    wc = ck.shape[1]
    return pl.pallas_call(
        _attn_sample_kernel,
        grid=(db,),
        in_specs=[
            pl.BlockSpec(memory_space=pltpu.SMEM),
            pl.BlockSpec((t, Q_W), lambda b: (b, 0)),
            pl.BlockSpec((t, 2 * KV_W), lambda b: (b, 0)),
            pl.BlockSpec((1, wc, KV_W), lambda b: (b, 0, 0)),
            pl.BlockSpec((1, wc, KV_W), lambda b: (b, 0, 0)),
        ],
        out_specs=[
            pl.BlockSpec((t, Q_W), lambda b: (b, 0)),
            pl.BlockSpec((1, wc, KV_W), lambda b: (b, 0, 0)),
            pl.BlockSpec((1, wc, KV_W), lambda b: (b, 0, 0)),
        ],
        out_shape=[
            jax.ShapeDtypeStruct((db * t, Q_W), F32),
            jax.ShapeDtypeStruct((db, wc, KV_W), F32),
            jax.ShapeDtypeStruct((db, wc, KV_W), F32),
        ],
        compiler_params=_params("arbitrary"),
        name="attn_sample",
    )(sinks, q, kvn, ck, cv)


def _proj_post_kernel(has_bias, a_ref, w_ref, b_ref, x_ref, g_ref, gn_ref, o_ref, xn_ref):
    y = jnp.dot(a_ref[...].astype(BF16), w_ref[...], preferred_element_type=F32)
    if has_bias:
        y = y + b_ref[...]
    x_new = x_ref[...] + _rmsnorm(y, g_ref[...])
    o_ref[...] = x_new
    xn_ref[...] = _rmsnorm(x_new, gn_ref[...]).astype(xn_ref.dtype)


def _proj_post(a, w, b, x, g, g_next, tm):
    m, k = a.shape
    has_bias = b is not None
    if b is None:
        b = jnp.zeros((1, D_MODEL), F32)
    return pl.pallas_call(
        functools.partial(_proj_post_kernel, has_bias),
        grid=(m // tm,),
        in_specs=[
            pl.BlockSpec((tm, k), lambda i: (i, 0)),
            pl.BlockSpec((k, D_MODEL), lambda i: (0, 0)),
            pl.BlockSpec((1, D_MODEL), lambda i: (0, 0)),
            pl.BlockSpec((tm, D_MODEL), lambda i: (i, 0)),
            pl.BlockSpec((1, D_MODEL), lambda i: (0, 0)),
            pl.BlockSpec((1, D_MODEL), lambda i: (0, 0)),
        ],
        out_specs=[pl.BlockSpec((tm, D_MODEL), lambda i: (i, 0)), pl.BlockSpec((tm, D_MODEL), lambda i: (i, 0))],
        out_shape=[jax.ShapeDtypeStruct((m, D_MODEL), F32), jax.ShapeDtypeStruct((m, D_MODEL), BF16)],
        compiler_params=_params("arbitrary"),
        name="proj_post",
    )(a, w, b, x, g, g_next)


_NT = (((1,), (1,)), ((), ()))


def _mlstm_proj_kernel(keys_transposed, x_ref, g_ref, wt_ref, wk_ref, wg_ref, bg_ref, o_ref, k_ref, gates_ref, xn_ref):
    @pl.when(pl.program_id(1) == 0)
    def _():
        xn = _rmsnorm(x_ref[...], g_ref[...]).astype(BF16)
        xn_ref[...] = xn
        if keys_transposed:
            k_ref[...] = lax.dot_general(wk_ref[...], xn, _NT, preferred_element_type=F32)
        else:
            k_ref[...] = lax.dot_general(xn, wk_ref[...], _NT, preferred_element_type=F32)
        gates_ref[...] = lax.dot_general(xn, wg_ref[...], _NT, preferred_element_type=F32) + bg_ref[...]

    o_ref[...] = lax.dot_general(xn_ref[...], wt_ref[...], _NT, preferred_element_type=F32)


def _mlstm_proj(x, g, w_t, wg_t, bg, tm, keys_transposed):
    m = x.shape[0]
    tn = PROJ_N_TILE
    assert QK_W == tn
    k_spec = pl.BlockSpec((QK_W, tm), lambda i, j: (0, i), pipeline_mode=pl.Buffered(1)) if keys_transposed else \
        pl.BlockSpec((tm, QK_W), lambda i, j: (i, 0), pipeline_mode=pl.Buffered(1))
    return pl.pallas_call(
        functools.partial(_mlstm_proj_kernel, keys_transposed),
        grid=(m // tm, PROJ_W // tn),
        in_specs=[
            pl.BlockSpec((tm, D_MODEL), lambda i, j: (i, 0)),
            pl.BlockSpec((1, D_MODEL), lambda i, j: (0, 0)),
            pl.BlockSpec((tn, D_MODEL), lambda i, j: (j + jnp.minimum(j, 1), 0)),
            pl.BlockSpec((QK_W, D_MODEL), lambda i, j: (1, 0), pipeline_mode=pl.Buffered(1)),
            pl.BlockSpec((GATE_PAD, D_MODEL), lambda i, j: (0, 0)),
            pl.BlockSpec((1, GATE_PAD), lambda i, j: (0, 0)),
        ],
        out_specs=[
            pl.BlockSpec((tm, tn), lambda i, j: (i, j)),
            k_spec,
            pl.BlockSpec((tm, GATE_PAD), lambda i, j: (i, 0)),
        ],
        out_shape=[
            jax.ShapeDtypeStruct((m, PROJ_W), F32),
            jax.ShapeDtypeStruct((QK_W, m) if keys_transposed else (m, QK_W), F32),
            jax.ShapeDtypeStruct((m, GATE_PAD), F32),
        ],
        scratch_shapes=[pltpu.VMEM((tm, D_MODEL), BF16)],
        compiler_params=_params("arbitrary", "arbitrary"),
        name="mlstm_proj",
    )(x, g, w_t, w_t, wg_t, bg)


def _log_sigmoid(x):
    return jnp.minimum(x, 0.0) - jnp.log1p(jnp.exp(-jnp.abs(x)))


def _chunk_gates(li, lf, m_state):
    L = li.shape[0]
    ri = lax.broadcasted_iota(jnp.int32, (L, L), 0)
    ci = lax.broadcasted_iota(jnp.int32, (L, L), 1)
    eye = ri == ci
    tril = ci <= ri
    lf_row = jnp.sum(jnp.where(eye, lf, 0.0), axis=0, keepdims=True)
    li_row = jnp.sum(jnp.where(eye, li, 0.0), axis=0, keepdims=True)
    b_col = jnp.sum(jnp.where(tril, lf_row, 0.0), axis=1, keepdims=True)
    b_row = jnp.sum(jnp.where(ri <= ci, lf, 0.0), axis=0, keepdims=True)
    dmat = jnp.where(tril, b_col - b_row + li_row, -jnp.inf)
    inter = b_col + m_state
    m_row = jnp.maximum(inter, jnp.max(dmat, axis=1, keepdims=True))
    b_last = b_col[L - 1:L, :]
    gk_col = b_last - b_col + li
    gk_row = b_last - b_row + li_row
    m_new = jnp.maximum(b_last + m_state, jnp.max(gk_col, axis=0, keepdims=True))
    decay = jnp.exp(b_last + m_state - m_new)
    return jnp.exp(dmat - m_row), jnp.exp(inter - m_row), m_row, gk_col, gk_row, m_new, decay


def _mlstm_chunk(q, k, v, li, lf, c_state, n_state, m_state):
    dexp, w_inter, m_row, gk_col, _, m_new, decay = _chunk_gates(li, lf, m_state)
    s = lax.dot_general(q, k, (((1,), (1,)), ((), ())), preferred_element_type=F32)
    sm = s * dexp
    num = (jnp.dot(sm, v, preferred_element_type=F32)
           + w_inter * jnp.dot(q, c_state, preferred_element_type=F32))
    den = jnp.sum(sm, axis=1, keepdims=True) + w_inter * jnp.sum(q * n_state, axis=1, keepdims=True)
    h = num * (1.0 / jnp.maximum(jnp.abs(den), jnp.exp(-m_row)))
    kw = k * jnp.exp(gk_col - m_new)
    c_new = decay * c_state + lax.dot_general(kw, v, (((0,), (0,)), ((), ())), preferred_element_type=F32)
    n_new = decay * n_state + jnp.sum(kw, axis=0, keepdims=True)
    return h, c_new, n_new, m_new


def _mlstm_chunk_t(q, kt, v1, li, lf, c_aug, m_state):
    dk = q.shape[1]
    dv = v1.shape[1] - LANES
    scale = dk ** -0.5
    dexp, w_inter, m_row, _, gk_row, m_new, decay = _chunk_gates(li, lf, m_state)
    qm = q.astype(BF16)
    s = jnp.dot(qm, kt.astype(BF16), preferred_element_type=F32)
    sm = (s * scale) * dexp
    num = (jnp.dot(sm.astype(BF16), v1, preferred_element_type=F32)
           + (w_inter * scale) * jnp.dot(qm, c_aug.astype(BF16), preferred_element_type=F32))
    den = num[:, dv:dv + 1]
    h = num[:, :dv] * (1.0 / jnp.maximum(jnp.abs(den), jnp.exp(-m_row)))
    kwt = (kt * jnp.exp(gk_row - m_new)).astype(BF16)
    c_new = decay * c_aug + jnp.dot(kwt, v1, preferred_element_type=F32)
    return h, c_new, m_new


def _head_cols(hd):
    q0 = hd * M_QK_DIM
    v0 = QK_W + hd * M_V_DIM
    o0 = QK_W + V_W + hd * M_V_DIM
    return (q0, q0 + M_QK_DIM), (v0, v0 + M_V_DIM), (o0, o0 + M_V_DIM), hd, M_HEADS + hd


def _head_out(h, o, ng):
    return _rmsnorm(h, ng) * (0.5 * (jnp.tanh(0.5 * o) + 1.0))


def _mlstm_prompt_kernel(batch, proj_ref, gates_ref, *rest):
    kt_refs = rest[:batch]
    ng_ref, h_ref, c_ref, m_ref = rest[batch:]

    @pl.when(pl.program_id(0) == 0)
    def _():
        c_ref[...] = jnp.zeros(c_ref.shape, F32)
        m_ref[...] = jnp.zeros(m_ref.shape, F32)

    L = proj_ref.shape[1]
    one_col = (lax.broadcasted_iota(jnp.int32, (L, LANES), 1) == 0).astype(BF16)
    for b in range(batch):
        for hd in range(M_HEADS):
            r = b * M_HEADS + hd
            (q0, q1), (v0, v1), (o0, o1), gi, gf = _head_cols(hd)
            va = jnp.concatenate([proj_ref[b, :, v0:v1].astype(BF16), one_col], axis=1)
            h, c_new, m_new = _mlstm_chunk_t(
                proj_ref[b, :, q0:q1], kt_refs[b][hd * M_QK_DIM:(hd + 1) * M_QK_DIM, :], va,
                gates_ref[b, :, gi:gi + 1], _log_sigmoid(gates_ref[b, :, gf:gf + 1]),
                c_ref[r], m_ref[r:r + 1, 0:1])
            hn = _head_out(h, proj_ref[b, :, o0:o1], ng_ref[:, hd * M_V_DIM:(hd + 1) * M_V_DIM])
            h_ref[b, :, hd * M_V_DIM:(hd + 1) * M_V_DIM] = hn.astype(h_ref.dtype)
            c_ref[r] = c_new
            m_ref[r:r + 1, :] = jnp.broadcast_to(m_new, (1, LANES))


def _mlstm_prompt(proj, kt, gates, ng, batch, seq):
    chunk = MLSTM_CHUNK
    nc = seq // chunk
    rows = batch * M_HEADS
    return pl.pallas_call(
        functools.partial(_mlstm_prompt_kernel, batch),
        grid=(nc,),
        in_specs=[pl.BlockSpec((batch, chunk, PROJ_W), lambda c: (0, c, 0)),
                  pl.BlockSpec((batch, chunk, GATE_PAD), lambda c: (0, c, 0))]
        + [pl.BlockSpec((QK_W, chunk), lambda c, b=b: (0, b * nc + c)) for b in range(batch)]
        + [pl.BlockSpec((1, V_W), lambda c: (0, 0))],
        out_specs=[
            pl.BlockSpec((batch, chunk, V_W), lambda c: (0, c, 0)),
            pl.BlockSpec((rows, M_QK_DIM, M_V_DIM + LANES), lambda c: (0, 0, 0)),
            pl.BlockSpec((rows, LANES), lambda c: (0, 0)),
        ],
        out_shape=[
            jax.ShapeDtypeStruct((batch, seq, V_W), BF16),
            jax.ShapeDtypeStruct((rows, M_QK_DIM, M_V_DIM + LANES), F32),
            jax.ShapeDtypeStruct((rows, LANES), F32),
        ],
        compiler_params=_params("arbitrary"),
        name="mlstm_prompt",
    )(proj.reshape(batch, seq, PROJ_W), gates.reshape(batch, seq, GATE_PAD), *([kt] * batch), ng)


def _mlstm_sample_kernel(t, proj_ref, k_ref, gates_ref, ng_ref, c0_ref, n0_ref, m0_ref, h_ref, c_ref, n_ref, m_ref):
    for s in range(c0_ref.shape[0]):
        rows = slice(s * t, (s + 1) * t)
        for hd in range(M_HEADS):
            (q0, q1), (v0, v1), (o0, o1), gi, gf = _head_cols(hd)
            h, c_new, n_new, m_new = _mlstm_chunk(
                proj_ref[rows, q0:q1] * (M_QK_DIM ** -0.5), k_ref[rows, hd * M_QK_DIM:(hd + 1) * M_QK_DIM],
                proj_ref[rows, v0:v1], gates_ref[rows, gi:gi + 1], _log_sigmoid(gates_ref[rows, gf:gf + 1]),
                c0_ref[s, hd], n0_ref[s, hd:hd + 1, :], m0_ref[s, :, hd:hd + 1])
            h_ref[rows, hd * M_V_DIM:(hd + 1) * M_V_DIM] = _head_out(
                h, proj_ref[rows, o0:o1], ng_ref[:, hd * M_V_DIM:(hd + 1) * M_V_DIM])
            c_ref[s, hd] = c_new
            n_ref[s, hd:hd + 1, :] = n_new
            m_ref[s, :, hd:hd + 1] = m_new


def _mlstm_sample(proj, k, gates, ng, c0, n0, m0, t):
    db = c0.shape[0]
    sb = MLSTM_SAMPLE_SEQS
    m0 = m0.reshape(db, 1, M_HEADS)
    state_specs = [
        pl.BlockSpec((sb, M_HEADS, M_QK_DIM, M_V_DIM), lambda b: (b, 0, 0, 0)),
        pl.BlockSpec((sb, M_HEADS, M_QK_DIM), lambda b: (b, 0, 0)),
        pl.BlockSpec((sb, 1, M_HEADS), lambda b: (b, 0, 0)),
    ]
    return pl.pallas_call(
        functools.partial(_mlstm_sample_kernel, t),
        grid=(db // sb,),
        in_specs=[
            pl.BlockSpec((sb * t, PROJ_W), lambda b: (b, 0)),
            pl.BlockSpec((sb * t, QK_W), lambda b: (b, 0)),
            pl.BlockSpec((sb * t, GATE_PAD), lambda b: (b, 0)),
            pl.BlockSpec((1, V_W), lambda b: (0, 0)),
        ] + state_specs,
        out_specs=[pl.BlockSpec((sb * t, V_W), lambda b: (b, 0))] + state_specs,
        out_shape=[
            jax.ShapeDtypeStruct((db * t, V_W), F32),
            jax.ShapeDtypeStruct(c0.shape, F32),
            jax.ShapeDtypeStruct(n0.shape, F32),
            jax.ShapeDtypeStruct(m0.shape, F32),
        ],
        compiler_params=_params("arbitrary"),
        name="mlstm_sample",
    )(proj, k, gates, ng, c0, n0, m0)


def _conv3(u, u1, u2, w, b):
    return ((b + w[0:1] * u2) + w[1:2] * u1) + w[2:3] * u


def _conv_stream(u, tail, w, b):
    u1 = pltpu.roll(u, 1, 0)
    u2 = pltpu.roll(u, 2, 0)
    c = _conv3(u, u1, u2, w, b)
    uf = u[0:SUBLANES]
    row = lax.broadcasted_iota(jnp.int32, uf.shape, 0)
    uf1 = jnp.where(row < 1, pltpu.roll(tail, 1, 0), pltpu.roll(uf, 1, 0))
    uf2 = jnp.where(row < 2, pltpu.roll(tail, 2, 0), pltpu.roll(uf, 2, 0))
    cf = _conv3(uf, uf1, uf2, w, b)
    return jnp.concatenate([cf, c[SUBLANES:]], axis=0)


def _conv_seq8(u, prev, w, b):
    nseq = u.shape[0] // SUBLANES
    p0 = jnp.broadcast_to(prev[:, 0:1, :], (nseq, SUBLANES, u.shape[1])).reshape(u.shape)
    p1 = jnp.broadcast_to(prev[:, 1:2, :], (nseq, SUBLANES, u.shape[1])).reshape(u.shape)
    row = lax.broadcasted_iota(jnp.int32, u.shape, 0) & (SUBLANES - 1)
    u1 = jnp.where(row == 0, p1, pltpu.roll(u, 1, 0))
    u2 = jnp.where(row == 0, p0, jnp.where(row == 1, p1, pltpu.roll(u, 2, 0)))
    return _conv3(u, u1, u2, w, b)


def _ffn_kernel(stream, hosts_cast, tiles_per_seq, x_ref, xn_ref, wg_ref, wv_ref, cw_ref, cb_ref, wd_ref, gpost_ref,
                *rest):
    if hosts_cast:
        nu_ref, nd_ref, o_ref, sg_ref, sv_ref, nu_bf_ref, nd_bf_ref, tail_g, tail_v = rest
        nu_bf_ref[...] = nu_ref[...].astype(BF16)
        nd_bf_ref[...] = nd_ref[...].astype(BF16)
    elif stream:
        o_ref, sg_ref, sv_ref, tail_g, tail_v = rest
    else:
        pg_ref, pv_ref, o_ref, sg_ref, sv_ref = rest
    i = pl.program_id(0)
    j = pl.program_id(1)
    nf = pl.num_programs(1)
    tm = x_ref.shape[0]

    if stream:
        @pl.when(i % tiles_per_seq == 0)
        def _():
            tail_g[j] = jnp.zeros(tail_g.shape[1:], F32)
            tail_v[j] = jnp.zeros(tail_v.shape[1:], F32)

    xn = xn_ref[...]
    ug = jnp.dot(xn, wg_ref[...], preferred_element_type=F32)
    uv = jnp.dot(xn, wv_ref[...], preferred_element_type=F32)
    cwg, cwv, cbg, cbv = cw_ref[j], cw_ref[j + nf], cb_ref[j], cb_ref[j + nf]
    if stream:
        cg = _conv_stream(ug, tail_g[j], cwg, cbg)
        cv = _conv_stream(uv, tail_v[j], cwv, cbv)
        tail_g[j] = ug[tm - SUBLANES:]
        tail_v[j] = uv[tm - SUBLANES:]
        sg_ref[0, j] = ug[tm - SUBLANES:]
        sv_ref[0, j] = uv[tm - SUBLANES:]
    else:
        cg = _conv_seq8(ug, pg_ref[...], cwg, cbg)
        cv = _conv_seq8(uv, pv_ref[...], cwv, cbv)
        sg_ref[...] = ug.reshape(sg_ref.shape)
        sv_ref[...] = uv.reshape(sv_ref.shape)
    h = (jax.nn.gelu(cg, approximate=True) * cv).astype(BF16)
    acc = jnp.where(j == 0, 0.0, o_ref[...])
    o_ref[...] = acc + jnp.dot(h, wd_ref[...], preferred_element_type=F32)

    @pl.when(j == nf - 1)
    def _():
        o_ref[...] = x_ref[...] + _rmsnorm(o_ref[...], gpost_ref[...])


def _ffn(x, xn, prev, w_up, conv_w, conv_b, w_down, gpost, tm, tf, seq, cast_next=None):
    m = x.shape[0]
    nf = D_FF // tf
    stream = prev is None
    conv_w = conv_w.reshape(CONV_W, 2 * nf, tf).transpose(1, 0, 2)
    conv_b = conv_b.reshape(2 * nf, 1, tf)
    in_specs = [
        pl.BlockSpec((tm, D_MODEL), lambda i, j: (i, 0)),
        pl.BlockSpec((tm, D_MODEL), lambda i, j: (i, 0)),
        pl.BlockSpec((None, D_MODEL, tf), lambda i, j: (0, 0, j)),
        pl.BlockSpec((None, D_MODEL, tf), lambda i, j: (0, 0, j + nf)),
        pl.BlockSpec((2 * nf, CONV_W, tf), lambda i, j: (0, 0, 0)),
        pl.BlockSpec((2 * nf, 1, tf), lambda i, j: (0, 0, 0)),
        pl.BlockSpec((None, tf, D_MODEL), lambda i, j: (0, j, 0)),
        pl.BlockSpec((1, D_MODEL), lambda i, j: (0, 0)),
    ]
    args = [x, xn, w_up, w_up, conv_w, conv_b, w_down, gpost]
    scratch = []
    if stream:
        tiles_per_seq = seq // tm
        state_shape = (m // tm, nf, SUBLANES, tf)
        state_spec = pl.BlockSpec((1, nf, SUBLANES, tf), lambda i, j: (i, 0, 0, 0))
        scratch += [pltpu.VMEM((nf, SUBLANES, tf), F32), pltpu.VMEM((nf, SUBLANES, tf), F32)]
    else:
        assert seq == SUBLANES and m == tm
        tiles_per_seq = 1
        nseq = m // seq
        state_shape = (nseq, SUBLANES, D_FF)
        state_spec = pl.BlockSpec((nseq, SUBLANES, tf), lambda i, j: (0, 0, j))
        in_specs += [
            pl.BlockSpec((nseq, CONV_W - 1, tf), lambda i, j: (0, 0, j)),
            pl.BlockSpec((nseq, CONV_W - 1, tf), lambda i, j: (0, 0, j + nf)),
        ]
        args += [prev, prev]
    out_specs = [pl.BlockSpec((tm, D_MODEL), lambda i, j: (i, 0)), state_spec, state_spec]
    out_shape = [
        jax.ShapeDtypeStruct((m, D_MODEL), F32),
        jax.ShapeDtypeStruct(state_shape, F32),
        jax.ShapeDtypeStruct(state_shape, F32),
    ]
    if cast_next is not None:
        assert stream
        nu, nd, layer_next = cast_next
        steps = (m // tm) * nf
        for w in (nu, nd):
            rows = w.shape[1] // steps
            in_specs.append(pl.BlockSpec((1, rows, w.shape[2]), lambda i, j: (layer_next, i * nf + j, 0)))
            out_specs.append(pl.BlockSpec((1, rows, w.shape[2]), lambda i, j: (0, i * nf + j, 0)))
            out_shape.append(jax.ShapeDtypeStruct((1,) + w.shape[1:], BF16))
            args.append(w)
    return pl.pallas_call(
        functools.partial(_ffn_kernel, stream, cast_next is not None, tiles_per_seq),
        grid=(m // tm, nf),
        in_specs=in_specs,
        out_specs=out_specs,
        out_shape=out_shape,
        scratch_shapes=scratch,
        compiler_params=_params("arbitrary", "arbitrary"),
        name="conv_ffn",
    )(*args)


def _rope_tables(pos):
    half = HEAD_DIM // 2
    inv = ROPE_THETA ** (-jnp.arange(half, dtype=F32) / half)
    ang = pos.astype(F32)[:, None] * inv[None, :]
    cos = jnp.cos(ang)
    sin = jnp.sin(ang)
    reps = LANES // HEAD_DIM
    return (jnp.tile(cos, (1, 2 * reps)), jnp.tile(jnp.concatenate([-sin, sin], axis=1), (1, reps)))


def _conv_state(sg, sv, tiles_per_seq):
    keep = slice(SUBLANES - (CONV_W - 1), SUBLANES)
    if tiles_per_seq is None:
        return jnp.concatenate([sg[:, keep], sv[:, keep]], axis=-1)
    last = slice(tiles_per_seq - 1, None, tiles_per_seq)

    def rows(s):
        s = s[last, :, keep, :]
        return jnp.swapaxes(s, 1, 2).reshape(s.shape[0], CONV_W - 1, -1)

    return jnp.concatenate([rows(sg), rows(sv)], axis=-1)


def _trunk(x, pos, seq, tm, wide_tm, cache, state, state_conv, p):
    m = x.shape[0]
    nseq = m // seq
    sample = cache is not None
    row = lambda v: v.reshape(1, -1)

    cos, sin = _rope_tables(pos)
    q, kv = _qkv_rope(x, row(p["norm_mix_pre"][0]), p["w_qkv"], row(p["attn_b_qkv"][0]), cos, sin, tm,
                      F32 if sample else BF16)
    if sample:
        ck, cv = cache
        o, k_new, v_new = _attn_sample(p["attn_sinks"][0], q, kv, ck.reshape(nseq, WINDOW, KV_W),
                                       cv.reshape(nseq, WINDOW, KV_W), seq)
    else:
        o, p["w_up0"], p["w_down0"] = _attn_prompt(p["attn_sinks"][0], q, kv, nseq, seq, p["ffn_w_up"],
                                                   p["ffn_w_down"], 0)
        kv3 = kv.reshape(nseq, seq, 2 * KV_W)
        k_new = kv3[:, seq - WINDOW:, :KV_W]
        v_new = kv3[:, seq - WINDOW:, KV_W:]
    k_new = k_new.reshape(1, nseq, WINDOW, N_KV_HEADS, HEAD_DIM)
    v_new = v_new.reshape(1, nseq, WINDOW, N_KV_HEADS, HEAD_DIM)
    x, xn = _proj_post(o, p["w_o"], row(p["attn_b_o"][0]), x, row(p["norm_mix_post"][0]),
                       row(p["norm_ffn_pre"][0]), tm)
    if sample:
        x, sg0, sv0 = _ffn(x, xn, state_conv[0], p["w_up0"], p["ffn_conv_w"][0], p["ffn_conv_b"][0],
                           p["w_down0"], row(p["norm_ffn_post"][0]), tm, FF_TILE, seq)
    else:
        x, sg0, sv0, p["w_up1"], p["w_down1"] = _ffn(
            x, xn, None, p["w_up0"], p["ffn_conv_w"][0], p["ffn_conv_b"][0], p["w_down0"],
            row(p["norm_ffn_post"][0]), tm, FF_TILE, seq, cast_next=(p["ffn_w_up"], p["ffn_w_down"], 1))

    g1 = row(p["norm_mix_pre"][1])
    ng = row(p["mlstm_norm"][0])
    proj, keys, gates = _mlstm_proj(x, g1, p["w_in_t"], p["w_gates_t"], p["b_gates"], wide_tm, not sample)
    if sample:
        c0, n0, m0 = state
        h, c_new, n_new, m_new = _mlstm_sample(proj, keys, gates, ng, c0, n0, m0, seq)
    else:
        h, c_aug, m_new = _mlstm_prompt(proj, keys, gates, ng, nseq, seq)
        h = h.reshape(m, V_W)
        c_new = c_aug[:, :, :M_V_DIM]
        n_new = c_aug[:, :, M_V_DIM]
        m_new = m_new[:, 0]
    c_new = c_new.reshape(1, nseq, M_HEADS, M_QK_DIM, M_V_DIM)
    n_new = n_new.reshape(1, nseq, M_HEADS, M_QK_DIM)
    m_new = m_new.reshape(1, nseq, M_HEADS)
    x, xn = _proj_post(h, p["w_out"], None, x, row(p["norm_mix_post"][1]), row(p["norm_ffn_pre"][1]), tm)
    x, sg1, sv1 = _ffn(x, xn, state_conv[1] if sample else None, p["w_up1"], p["ffn_conv_w"][1],
                       p["ffn_conv_b"][1], p["w_down1"], row(p["norm_ffn_post"][1]), tm, FF_TILE, seq)
    tps = None if sample else seq // tm
    conv = jnp.stack([_conv_state(sg0, sv0, tps), _conv_state(sg1, sv1, tps)])
    return x, k_new, v_new, c_new, n_new, m_new, conv


def kernel(x_prompt, x_sample, cache_k, cache_v, state_C, state_n, state_m, state_conv, norm_mix_pre,
           norm_mix_post, norm_ffn_pre, norm_ffn_post, attn_w_qkv, attn_b_qkv, attn_w_o, attn_b_o, attn_sinks,
           mlstm_w_in, mlstm_b_gates, mlstm_norm, mlstm_w_out, ffn_w_up, ffn_conv_w, ffn_conv_b, ffn_w_down):
    batch, seq, _ = x_prompt.shape
    dec_batch, dec_seq, _ = x_sample.shape
    n_gates = 2 * M_HEADS
    w_in_t = mlstm_w_in[0].T.astype(BF16)
    g0 = 2 * QK_W + 2 * V_W
    p = dict(
        norm_mix_pre=norm_mix_pre, norm_mix_post=norm_mix_post, norm_ffn_pre=norm_ffn_pre,
        norm_ffn_post=norm_ffn_post, attn_b_qkv=attn_b_qkv, attn_b_o=attn_b_o, attn_sinks=attn_sinks,
        mlstm_norm=mlstm_norm, ffn_conv_w=ffn_conv_w, ffn_conv_b=ffn_conv_b,
        w_qkv=attn_w_qkv[0].astype(BF16),
        w_o=attn_w_o[0].astype(BF16),
        w_in_t=w_in_t,
        w_gates_t=jnp.pad(w_in_t[g0:], ((0, GATE_PAD - n_gates), (0, 0))),
        b_gates=jnp.pad(mlstm_b_gates[0], (0, GATE_PAD - n_gates)).reshape(1, GATE_PAD),
        w_out=mlstm_w_out[0].astype(BF16),
        ffn_w_up=ffn_w_up, ffn_w_down=ffn_w_down,
    )
    yp, k_p, v_p, c_p, n_p, m_p, conv_p = _trunk(
        x_prompt.reshape(batch * seq, D_MODEL), jnp.arange(seq), seq, ROW_TILE, WIDE_ROW_TILE, None, None, None, p)
    ys, k_s, v_s, c_s, n_s, m_s, conv_s = _trunk(
        x_sample.reshape(dec_batch * dec_seq, D_MODEL),
        jnp.tile(PAST_LEN + jnp.arange(dec_seq), dec_batch), dec_seq, dec_batch * dec_seq, dec_batch * dec_seq,
        (cache_k[0], cache_v[0]), (state_C[0], state_n[0], state_m[0]), state_conv, p)
    return (yp.reshape(batch, seq, D_MODEL), ys.reshape(dec_batch, dec_seq, D_MODEL),
            k_p, v_p, k_s, v_s, c_p, n_p, m_p, c_s, n_s, m_s, conv_p, conv_s)
```

```python
import functools

import jax
import jax.numpy as jnp
from jax import lax
from jax.experimental import pallas as pl
from jax.experimental.pallas import tpu as pltpu

F32 = jnp.float32
BF16 = jnp.bfloat16

D_MODEL = 2048
WINDOW = 128
HEAD_DIM = 64
N_HEADS = 32
N_KV_HEADS = 4
GROUP = N_HEADS // N_KV_HEADS
Q_W = N_HEADS * HEAD_DIM
KV_W = N_KV_HEADS * HEAD_DIM
ROPE_THETA = 10000.0
M_HEADS = 4
M_QK_DIM = 256
M_V_DIM = 512
QK_W = M_HEADS * M_QK_DIM
V_W = M_HEADS * M_V_DIM
GATE_PAD = 128
PROJ_W = QK_W + 2 * V_W
D_FF = 4 * D_MODEL
CONV_W = 3
EPS = 1e-6
NEG_INF = -1e30
PAST_LEN = 16384

SUBLANES = 8
LANES = 128
VMEM_LIMIT_BYTES = 60 * 1024 * 1024

ROW_TILE = 512
WIDE_ROW_TILE = 1024
FF_TILE = 1024
PROJ_N_TILE = 1024
MLSTM_CHUNK = 256
MLSTM_SAMPLE_SEQS = 2


def _params(*sem):
    return pltpu.CompilerParams(dimension_semantics=sem, vmem_limit_bytes=VMEM_LIMIT_BYTES)


def _rmsnorm(xf, g):
    r = xf * lax.rsqrt(jnp.mean(xf * xf, axis=-1, keepdims=True) + EPS)
    return r * g


def _qkv_rope_kernel(x_ref, g_ref, w_ref, b_ref, cos_ref, sin_ref, q_ref, kv_ref):
    xn = _rmsnorm(x_ref[...], g_ref[...]).astype(BF16)
    y = jnp.dot(xn, w_ref[...], preferred_element_type=F32) + b_ref[...]
    cos = cos_ref[...]
    sin = sin_ref[...]
    lane = lax.broadcasted_iota(jnp.int32, cos.shape, 1)
    first_half = (lane & (HEAD_DIM - 1)) < (HEAD_DIM // 2)
    n_rot = (Q_W + KV_W) // LANES
    for c in range(n_rot):
        blk = y[:, c * LANES:(c + 1) * LANES]
        sw = jnp.where(first_half, pltpu.roll(blk, LANES - HEAD_DIM // 2, 1),
                       pltpu.roll(blk, HEAD_DIM // 2, 1))
        r = blk * cos + sw * sin
        if c < Q_W // LANES:
            q_ref[:, c * LANES:(c + 1) * LANES] = (r * (HEAD_DIM ** -0.5)).astype(q_ref.dtype)
        else:
            o = c * LANES - Q_W
            kv_ref[:, o:o + LANES] = r
    kv_ref[:, KV_W:2 * KV_W] = y[:, Q_W + KV_W:Q_W + 2 * KV_W]


def _qkv_rope(x, g, w, b, cos, sin, tm, q_dtype):
    m = x.shape[0]
    n = w.shape[1]
    n_pos_tiles = cos.shape[0] // tm
    return pl.pallas_call(
        _qkv_rope_kernel,
        grid=(m // tm,),
        in_specs=[
            pl.BlockSpec((tm, D_MODEL), lambda i: (i, 0)),
            pl.BlockSpec((1, D_MODEL), lambda i: (0, 0)),
            pl.BlockSpec((D_MODEL, n), lambda i: (0, 0)),
            pl.BlockSpec((1, n), lambda i: (0, 0)),
            pl.BlockSpec((tm, LANES), lambda i: (i % n_pos_tiles, 0)),
            pl.BlockSpec((tm, LANES), lambda i: (i % n_pos_tiles, 0)),
        ],
        out_specs=[
            pl.BlockSpec((tm, Q_W), lambda i: (i, 0)),
            pl.BlockSpec((tm, 2 * KV_W), lambda i: (i, 0)),
        ],
        out_shape=[
            jax.ShapeDtypeStruct((m, Q_W), q_dtype),
            jax.ShapeDtypeStruct((m, 2 * KV_W), F32),
        ],
        compiler_params=_params("arbitrary"),
        name="qkv_rope",
    )(x, g, w, b, cos, sin)


def _attn_prompt_kernel(n_cast, sinks_ref, q_ref, kvp_ref, kvc_ref, *rest):
    o_ref = rest[n_cast]
    for src_ref, dst_ref in zip(rest[:n_cast], rest[n_cast + 1:]):
        dst_ref[...] = src_ref[...].astype(BF16)
    n = pl.program_id(1)
    w = WINDOW
    ri = lax.broadcasted_iota(jnp.int32, (w, 2 * w), 0)
    ci = lax.broadcasted_iota(jnp.int32, (w, 2 * w), 1)
    has_prev = jnp.full((w, 2 * w), n, jnp.int32) > 0
    allowed = ((ci < w) & (ci > ri) & has_prev) | ((ci >= w) & ((ci - w) <= ri))
    low = lax.broadcasted_iota(jnp.int32, (w, LANES), 1) < HEAD_DIM
    ones = jnp.ones((2 * w, HEAD_DIM), BF16)
    for g in range(N_KV_HEADS):
        ks = slice(g * HEAD_DIM, (g + 1) * HEAD_DIM)
        vs = slice(KV_W + g * HEAD_DIM, KV_W + (g + 1) * HEAD_DIM)
        k = jnp.concatenate([kvp_ref[:, ks], kvc_ref[:, ks]], axis=0).astype(BF16)
        v = jnp.concatenate([kvp_ref[:, vs], kvc_ref[:, vs]], axis=0).astype(BF16)
        v1 = jnp.concatenate([v, ones], axis=1)
        q = jnp.concatenate(
            [q_ref[:, (g * GROUP + h) * HEAD_DIM:(g * GROUP + h + 1) * HEAD_DIM] for h in range(GROUP)],
            axis=0)
        s = lax.dot_general(q, k, (((1,), (1,)), ((), ())), preferred_element_type=F32)
        ps, es = [], []
        for h in range(GROUP):
            sh = jnp.where(allowed, s[h * w:(h + 1) * w], NEG_INF)
            sink = sinks_ref[g * GROUP + h]
            m = jnp.maximum(jnp.max(sh, axis=-1, keepdims=True), sink)
            ps.append(jnp.exp(sh - m).astype(BF16))
            es.append(jnp.exp(sink - m))
        oa = jnp.dot(jnp.concatenate(ps, axis=0), v1, preferred_element_type=F32)
        for h in range(0, GROUP, 2):
            a = oa[h * w:(h + 1) * w]
            b = oa[(h + 1) * w:(h + 2) * w]
            num = jnp.where(low, a, pltpu.roll(b, HEAD_DIM, 1))
            den = jnp.where(low, pltpu.roll(a, HEAD_DIM, 1) + es[h], b + es[h + 1])
            c0 = (g * GROUP + h) * HEAD_DIM
            o_ref[:, c0:c0 + LANES] = (num / den).astype(o_ref.dtype)


def _attn_prompt(sinks, q, kv, batch, seq, casts):
    nb = seq // WINDOW
    steps = batch * nb
    cast_in, cast_out, cast_shape = [], [], []
    for w, layer in casts:
        rows = w.shape[1] // steps
        cast_in.append(pl.BlockSpec((1, rows, w.shape[2]), lambda b, n, layer=layer: (layer, b * nb + n, 0)))
        cast_out.append(pl.BlockSpec((1, rows, w.shape[2]), lambda b, n: (0, b * nb + n, 0)))
        cast_shape.append(jax.ShapeDtypeStruct((1, rows * steps, w.shape[2]), BF16))
    return pl.pallas_call(
        functools.partial(_attn_prompt_kernel, len(casts)),
        grid=(batch, nb),
        in_specs=[
            pl.BlockSpec(memory_space=pltpu.SMEM),
            pl.BlockSpec((WINDOW, Q_W), lambda b, n: (b * nb + n, 0)),
            pl.BlockSpec((WINDOW, 2 * KV_W), lambda b, n: (b * nb + jnp.maximum(n - 1, 0), 0)),
            pl.BlockSpec((WINDOW, 2 * KV_W), lambda b, n: (b * nb + n, 0)),
        ] + cast_in,
        out_specs=[pl.BlockSpec((WINDOW, Q_W), lambda b, n: (b * nb + n, 0))] + cast_out,
        out_shape=[jax.ShapeDtypeStruct((batch * seq, Q_W), BF16)] + cast_shape,
        compiler_params=_params("arbitrary", "arbitrary"),
        name="attn_prompt",
    )(sinks, q, kv, kv, *[w for w, _ in casts])


def _attn_sample_kernel(sinks_ref, q_ref, kvn_ref, ck_ref, cv_ref, o_ref, ko_ref, vo_ref):
    t = q_ref.shape[0]
    wc = ck_ref.shape[1]
    rows = N_HEADS * t
    ri = lax.broadcasted_iota(jnp.int32, (rows, wc), 0) & (t - 1)
    ci = lax.broadcasted_iota(jnp.int32, (rows, wc), 1)
    allowed_c = ci > ri
    ri_n = lax.broadcasted_iota(jnp.int32, (rows, t), 0) & (t - 1)
    ci_n = lax.broadcasted_iota(jnp.int32, (rows, t), 1)
    allowed_n = ci_n <= ri_n
    nt = (((1,), (1,)), ((), ()))
    s_c, s_n = [], []
    for g in range(N_KV_HEADS):
        ks = slice(g * HEAD_DIM, (g + 1) * HEAD_DIM)
        q = jnp.concatenate(
            [q_ref[:, (g * GROUP + h) * HEAD_DIM:(g * GROUP + h + 1) * HEAD_DIM] for h in range(GROUP)],
            axis=0)
        s_c.append(lax.dot_general(q.astype(BF16), ck_ref[0, :, ks].astype(BF16), nt, preferred_element_type=F32))
        s_n.append(lax.dot_general(q, kvn_ref[:, ks], nt, preferred_element_type=F32))
    s_c = jnp.where(allowed_c, jnp.concatenate(s_c, axis=0), NEG_INF)
    s_n = jnp.where(allowed_n, jnp.concatenate(s_n, axis=0), NEG_INF)
    sink = jnp.concatenate([jnp.full((t, 1), sinks_ref[h], F32) for h in range(N_HEADS)], axis=0)
    m = jnp.maximum(jnp.maximum(jnp.max(s_c, axis=-1, keepdims=True),
                                jnp.max(s_n, axis=-1, keepdims=True)), sink)
    p_c = jnp.exp(s_c - m)
    p_n = jnp.exp(s_n - m)
    inv_l = 1.0 / (jnp.sum(p_c, axis=-1, keepdims=True) + jnp.sum(p_n, axis=-1, keepdims=True)
                   + jnp.exp(sink - m))
    p_c = (p_c * inv_l).astype(BF16)
    p_n = p_n * inv_l
    gr = GROUP * t
    for g in range(N_KV_HEADS):
        ks = slice(g * HEAD_DIM, (g + 1) * HEAD_DIM)
        vs = slice(KV_W + g * HEAD_DIM, KV_W + (g + 1) * HEAD_DIM)
        o = (jnp.dot(p_c[g * gr:(g + 1) * gr], cv_ref[0, :, ks].astype(BF16), preferred_element_type=F32)
             + jnp.dot(p_n[g * gr:(g + 1) * gr], kvn_ref[:, vs], preferred_element_type=F32))
        for h in range(GROUP):
            c0 = (g * GROUP + h) * HEAD_DIM
            o_ref[:, c0:c0 + HEAD_DIM] = o[h * t:(h + 1) * t]
    ko_ref[0, 0:wc - t, :] = ck_ref[0, t:wc, :]
    ko_ref[0, wc - t:wc, :] = kvn_ref[:, 0:KV_W]
    vo_ref[0, 0:wc - t, :] = cv_ref[0, t:wc, :]
    vo_ref[0, wc - t:wc, :] = kvn_ref[:, KV_W:2 * KV_W]


def _attn_sample(sinks, q, kvn, ck, cv, t):
    db = ck.shape[0]
    wc = ck.shape[1]
    return pl.pallas_call(
        _attn_sample_kernel,
        grid=(db,),
        in_specs=[
            pl.BlockSpec(memory_space=pltpu.SMEM),
            pl.BlockSpec((t, Q_W), lambda b: (b, 0)),
            pl.BlockSpec((t, 2 * KV_W), lambda b: (b, 0)),
            pl.BlockSpec((1, wc, KV_W), lambda b: (b, 0, 0)),
            pl.BlockSpec((1, wc, KV_W), lambda b: (b, 0, 0)),
        ],
        out_specs=[
            pl.BlockSpec((t, Q_W), lambda b: (b, 0)),
            pl.BlockSpec((1, wc, KV_W), lambda b: (b, 0, 0)),
            pl.BlockSpec((1, wc, KV_W), lambda b: (b, 0, 0)),
        ],
        out_shape=[
            jax.ShapeDtypeStruct((db * t, Q_W), F32),
            jax.ShapeDtypeStruct((db, wc, KV_W), F32),
            jax.ShapeDtypeStruct((db, wc, KV_W), F32),
        ],
        compiler_params=_params("arbitrary"),
        name="attn_sample",
    )(sinks, q, kvn, ck, cv)


def _proj_post_kernel(has_bias, a_ref, w_ref, b_ref, x_ref, g_ref, gn_ref, o_ref, xn_ref):
    y = jnp.dot(a_ref[...].astype(BF16), w_ref[...], preferred_element_type=F32)
    if has_bias:
        y = y + b_ref[...]
    x_new = x_ref[...] + _rmsnorm(y, g_ref[...])
    o_ref[...] = x_new
    xn_ref[...] = _rmsnorm(x_new, gn_ref[...]).astype(xn_ref.dtype)


def _proj_post(a, w, b, x, g, g_next, tm):
    m, k = a.shape
    has_bias = b is not None
    if b is None:
        b = jnp.zeros((1, D_MODEL), F32)
    return pl.pallas_call(
        functools.partial(_proj_post_kernel, has_bias),
        grid=(m // tm,),
        in_specs=[
            pl.BlockSpec((tm, k), lambda i: (i, 0)),
            pl.BlockSpec((k, D_MODEL), lambda i: (0, 0)),
            pl.BlockSpec((1, D_MODEL), lambda i: (0, 0)),
            pl.BlockSpec((tm, D_MODEL), lambda i: (i, 0)),
            pl.BlockSpec((1, D_MODEL), lambda i: (0, 0)),
            pl.BlockSpec((1, D_MODEL), lambda i: (0, 0)),
        ],
        out_specs=[pl.BlockSpec((tm, D_MODEL), lambda i: (i, 0)), pl.BlockSpec((tm, D_MODEL), lambda i: (i, 0))],
        out_shape=[jax.ShapeDtypeStruct((m, D_MODEL), F32), jax.ShapeDtypeStruct((m, D_MODEL), BF16)],
        compiler_params=_params("arbitrary"),
        name="proj_post",
    )(a, w, b, x, g, g_next)


_NT = (((1,), (1,)), ((), ()))


def _mlstm_proj_kernel(keys_transposed, x_ref, g_ref, wt_ref, wk_ref, wg_ref, bg_ref, o_ref, k_ref, gates_ref, xn_ref):
    @pl.when(pl.program_id(1) == 0)
    def _():
        xn = _rmsnorm(x_ref[...], g_ref[...]).astype(BF16)
        xn_ref[...] = xn
        if keys_transposed:
            k_ref[...] = lax.dot_general(wk_ref[...], xn, _NT, preferred_element_type=F32)
        else:
            k_ref[...] = lax.dot_general(xn, wk_ref[...], _NT, preferred_element_type=F32)
        gates_ref[...] = lax.dot_general(xn, wg_ref[...], _NT, preferred_element_type=F32) + bg_ref[...]

    o_ref[...] = lax.dot_general(xn_ref[...], wt_ref[...], _NT, preferred_element_type=F32)


def _mlstm_proj(x, g, w_t, wg_t, bg, tm, keys_transposed):
    m = x.shape[0]
    tn = PROJ_N_TILE
    assert QK_W == tn
    k_spec = pl.BlockSpec((QK_W, tm), lambda i, j: (0, i), pipeline_mode=pl.Buffered(1)) if keys_transposed else \
        pl.BlockSpec((tm, QK_W), lambda i, j: (i, 0), pipeline_mode=pl.Buffered(1))
    return pl.pallas_call(
        functools.partial(_mlstm_proj_kernel, keys_transposed),
        grid=(m // tm, PROJ_W // tn),
        in_specs=[
            pl.BlockSpec((tm, D_MODEL), lambda i, j: (i, 0)),
            pl.BlockSpec((1, D_MODEL), lambda i, j: (0, 0)),
            pl.BlockSpec((tn, D_MODEL), lambda i, j: (j + jnp.minimum(j, 1), 0)),
            pl.BlockSpec((QK_W, D_MODEL), lambda i, j: (1, 0), pipeline_mode=pl.Buffered(1)),
            pl.BlockSpec((GATE_PAD, D_MODEL), lambda i, j: (0, 0)),
            pl.BlockSpec((1, GATE_PAD), lambda i, j: (0, 0)),
        ],
        out_specs=[
            pl.BlockSpec((tm, tn), lambda i, j: (i, j)),
            k_spec,
            pl.BlockSpec((tm, GATE_PAD), lambda i, j: (i, 0)),
        ],
        out_shape=[
            jax.ShapeDtypeStruct((m, PROJ_W), F32),
            jax.ShapeDtypeStruct((QK_W, m) if keys_transposed else (m, QK_W), F32),
            jax.ShapeDtypeStruct((m, GATE_PAD), F32),
        ],
        scratch_shapes=[pltpu.VMEM((tm, D_MODEL), BF16)],
        compiler_params=_params("arbitrary", "arbitrary"),
        name="mlstm_proj",
    )(x, g, w_t, w_t, wg_t, bg)


def _log_sigmoid(x):
    return jnp.minimum(x, 0.0) - jnp.log1p(jnp.exp(-jnp.abs(x)))


def _chunk_gates(li, lf, m_state):
    L = li.shape[0]
    ri = lax.broadcasted_iota(jnp.int32, (L, L), 0)
    ci = lax.broadcasted_iota(jnp.int32, (L, L), 1)
    eye = ri == ci
    tril = ci <= ri
    lf_row = jnp.sum(jnp.where(eye, lf, 0.0), axis=0, keepdims=True)
    li_row = jnp.sum(jnp.where(eye, li, 0.0), axis=0, keepdims=True)
    b_col = jnp.sum(jnp.where(tril, lf_row, 0.0), axis=1, keepdims=True)
    b_row = jnp.sum(jnp.where(ri <= ci, lf, 0.0), axis=0, keepdims=True)
    dmat = jnp.where(tril, b_col - b_row + li_row, -jnp.inf)
    inter = b_col + m_state
    m_row = jnp.maximum(inter, jnp.max(dmat, axis=1, keepdims=True))
    b_last = b_col[L - 1:L, :]
    gk_col = b_last - b_col + li
    gk_row = b_last - b_row + li_row
    m_new = jnp.maximum(b_last + m_state, jnp.max(gk_col, axis=0, keepdims=True))
    decay = jnp.exp(b_last + m_state - m_new)
    return jnp.exp(dmat - m_row), jnp.exp(inter - m_row), m_row, gk_col, gk_row, m_new, decay


def _mlstm_chunk(q, k, v, li, lf, c_state, n_state, m_state):
    dexp, w_inter, m_row, gk_col, _, m_new, decay = _chunk_gates(li, lf, m_state)
    s = lax.dot_general(q, k, (((1,), (1,)), ((), ())), preferred_element_type=F32)
    sm = s * dexp
    num = (jnp.dot(sm, v, preferred_element_type=F32)
           + w_inter * jnp.dot(q, c_state, preferred_element_type=F32))
    den = jnp.sum(sm, axis=1, keepdims=True) + w_inter * jnp.sum(q * n_state, axis=1, keepdims=True)
    h = num * (1.0 / jnp.maximum(jnp.abs(den), jnp.exp(-m_row)))
    kw = k * jnp.exp(gk_col - m_new)
    c_new = decay * c_state + lax.dot_general(kw, v, (((0,), (0,)), ((), ())), preferred_element_type=F32)
    n_new = decay * n_state + jnp.sum(kw, axis=0, keepdims=True)
    return h, c_new, n_new, m_new


def _mlstm_chunk_t(q, kt, v1, li, lf, c_aug, m_state):
    dk = q.shape[1]
    dv = v1.shape[1] - LANES
    scale = dk ** -0.5
    dexp, w_inter, m_row, _, gk_row, m_new, decay = _chunk_gates(li, lf, m_state)
    qm = q.astype(BF16)
    s = jnp.dot(qm, kt.astype(BF16), preferred_element_type=F32)
    sm = (s * scale) * dexp
    num = (jnp.dot(sm.astype(BF16), v1, preferred_element_type=F32)
           + (w_inter * scale) * jnp.dot(qm, c_aug.astype(BF16), preferred_element_type=F32))
    den = num[:, dv:dv + 1]
    h = num[:, :dv] * (1.0 / jnp.maximum(jnp.abs(den), jnp.exp(-m_row)))
    kwt = (kt * jnp.exp(gk_row - m_new)).astype(BF16)
    c_new = decay * c_aug + jnp.dot(kwt, v1, preferred_element_type=F32)
    return h, c_new, m_new


def _head_cols(hd):
    q0 = hd * M_QK_DIM
    v0 = QK_W + hd * M_V_DIM
    o0 = QK_W + V_W + hd * M_V_DIM
    return (q0, q0 + M_QK_DIM), (v0, v0 + M_V_DIM), (o0, o0 + M_V_DIM), hd, M_HEADS + hd


def _head_out(h, o, ng):
    return _rmsnorm(h, ng) * (0.5 * (jnp.tanh(0.5 * o) + 1.0))


def _mlstm_prompt_kernel(batch, proj_ref, gates_ref, *rest):
    kt_refs = rest[:batch]
    ng_ref, h_ref, c_ref, m_ref = rest[batch:]

    @pl.when(pl.program_id(0) == 0)
    def _():
        c_ref[...] = jnp.zeros(c_ref.shape, F32)
        m_ref[...] = jnp.zeros(m_ref.shape, F32)

    L = proj_ref.shape[1]
    one_col = (lax.broadcasted_iota(jnp.int32, (L, LANES), 1) == 0).astype(BF16)
    for b in range(batch):
        for hd in range(M_HEADS):
            r = b * M_HEADS + hd
            (q0, q1), (v0, v1), (o0, o1), gi, gf = _head_cols(hd)
            va = jnp.concatenate([proj_ref[b, :, v0:v1].astype(BF16), one_col], axis=1)
            h, c_new, m_new = _mlstm_chunk_t(
                proj_ref[b, :, q0:q1], kt_refs[b][hd * M_QK_DIM:(hd + 1) * M_QK_DIM, :], va,
                gates_ref[b, :, gi:gi + 1], _log_sigmoid(gates_ref[b, :, gf:gf + 1]),
                c_ref[r], m_ref[r:r + 1, 0:1])
            hn = _head_out(h, proj_ref[b, :, o0:o1], ng_ref[:, hd * M_V_DIM:(hd + 1) * M_V_DIM])
            h_ref[b, :, hd * M_V_DIM:(hd + 1) * M_V_DIM] = hn.astype(h_ref.dtype)
            c_ref[r] = c_new
            m_ref[r:r + 1, :] = jnp.broadcast_to(m_new, (1, LANES))


def _mlstm_prompt(proj, kt, gates, ng, batch, seq):
    chunk = MLSTM_CHUNK
    nc = seq // chunk
    rows = batch * M_HEADS
    return pl.pallas_call(
        functools.partial(_mlstm_prompt_kernel, batch),
        grid=(nc,),
        in_specs=[pl.BlockSpec((batch, chunk, PROJ_W), lambda c: (0, c, 0)),
                  pl.BlockSpec((batch, chunk, GATE_PAD), lambda c: (0, c, 0))]
        + [pl.BlockSpec((QK_W, chunk), lambda c, b=b: (0, b * nc + c)) for b in range(batch)]
        + [pl.BlockSpec((1, V_W), lambda c: (0, 0))],
        out_specs=[
            pl.BlockSpec((batch, chunk, V_W), lambda c: (0, c, 0)),
            pl.BlockSpec((rows, M_QK_DIM, M_V_DIM + LANES), lambda c: (0, 0, 0)),
            pl.BlockSpec((rows, LANES), lambda c: (0, 0)),
        ],
        out_shape=[
            jax.ShapeDtypeStruct((batch, seq, V_W), BF16),
            jax.ShapeDtypeStruct((rows, M_QK_DIM, M_V_DIM + LANES), F32),
            jax.ShapeDtypeStruct((rows, LANES), F32),
        ],
        compiler_params=_params("arbitrary"),
        name="mlstm_prompt",
    )(proj.reshape(batch, seq, PROJ_W), gates.reshape(batch, seq, GATE_PAD), *([kt] * batch), ng)


def _mlstm_sample_kernel(t, proj_ref, k_ref, gates_ref, ng_ref, c0_ref, n0_ref, m0_ref, h_ref, c_ref, n_ref, m_ref):
    for s in range(c0_ref.shape[0]):
        rows = slice(s * t, (s + 1) * t)
        for hd in range(M_HEADS):
            (q0, q1), (v0, v1), (o0, o1), gi, gf = _head_cols(hd)
            h, c_new, n_new, m_new = _mlstm_chunk(
                proj_ref[rows, q0:q1] * (M_QK_DIM ** -0.5), k_ref[rows, hd * M_QK_DIM:(hd + 1) * M_QK_DIM],
                proj_ref[rows, v0:v1], gates_ref[rows, gi:gi + 1], _log_sigmoid(gates_ref[rows, gf:gf + 1]),
                c0_ref[s, hd], n0_ref[s, hd:hd + 1, :], m0_ref[s, :, hd:hd + 1])
            h_ref[rows, hd * M_V_DIM:(hd + 1) * M_V_DIM] = _head_out(
                h, proj_ref[rows, o0:o1], ng_ref[:, hd * M_V_DIM:(hd + 1) * M_V_DIM])
            c_ref[s, hd] = c_new
            n_ref[s, hd:hd + 1, :] = n_new
            m_ref[s, :, hd:hd + 1] = m_new


def _mlstm_sample(proj, k, gates, ng, c0, n0, m0, t):
    db = c0.shape[0]
    sb = MLSTM_SAMPLE_SEQS
    m0 = m0.reshape(db, 1, M_HEADS)
    state_specs = [
        pl.BlockSpec((sb, M_HEADS, M_QK_DIM, M_V_DIM), lambda b: (b, 0, 0, 0)),
        pl.BlockSpec((sb, M_HEADS, M_QK_DIM), lambda b: (b, 0, 0)),
        pl.BlockSpec((sb, 1, M_HEADS), lambda b: (b, 0, 0)),
    ]
    return pl.pallas_call(
        functools.partial(_mlstm_sample_kernel, t),
        grid=(db // sb,),
        in_specs=[
            pl.BlockSpec((sb * t, PROJ_W), lambda b: (b, 0)),
            pl.BlockSpec((sb * t, QK_W), lambda b: (b, 0)),
            pl.BlockSpec((sb * t, GATE_PAD), lambda b: (b, 0)),
            pl.BlockSpec((1, V_W), lambda b: (0, 0)),
        ] + state_specs,
        out_specs=[pl.BlockSpec((sb * t, V_W), lambda b: (b, 0))] + state_specs,
        out_shape=[
            jax.ShapeDtypeStruct((db * t, V_W), F32),
            jax.ShapeDtypeStruct(c0.shape, F32),
            jax.ShapeDtypeStruct(n0.shape, F32),
            jax.ShapeDtypeStruct(m0.shape, F32),
        ],
        compiler_params=_params("arbitrary"),
        name="mlstm_sample",
    )(proj, k, gates, ng, c0, n0, m0)


def _conv3(u, u1, u2, w, b):
    return ((b + w[0:1] * u2) + w[1:2] * u1) + w[2:3] * u


def _conv_stream(u, tail, w, b):
    u1 = pltpu.roll(u, 1, 0)
    u2 = pltpu.roll(u, 2, 0)
    c = _conv3(u, u1, u2, w, b)
    uf = u[0:SUBLANES]
    row = lax.broadcasted_iota(jnp.int32, uf.shape, 0)
    uf1 = jnp.where(row < 1, pltpu.roll(tail, 1, 0), pltpu.roll(uf, 1, 0))
    uf2 = jnp.where(row < 2, pltpu.roll(tail, 2, 0), pltpu.roll(uf, 2, 0))
    cf = _conv3(uf, uf1, uf2, w, b)
    return jnp.concatenate([cf, c[SUBLANES:]], axis=0)


def _conv_seq8(u, prev, w, b):
    nseq = u.shape[0] // SUBLANES
    p0 = jnp.broadcast_to(prev[:, 0:1, :], (nseq, SUBLANES, u.shape[1])).reshape(u.shape)
    p1 = jnp.broadcast_to(prev[:, 1:2, :], (nseq, SUBLANES, u.shape[1])).reshape(u.shape)
    row = lax.broadcasted_iota(jnp.int32, u.shape, 0) & (SUBLANES - 1)
    u1 = jnp.where(row == 0, p1, pltpu.roll(u, 1, 0))
    u2 = jnp.where(row == 0, p0, jnp.where(row == 1, p1, pltpu.roll(u, 2, 0)))
    return _conv3(u, u1, u2, w, b)


def _ffn_kernel(stream, hosts_cast, tiles_per_seq, x_ref, xn_ref, wg_ref, wv_ref, cw_ref, cb_ref, wd_ref, gpost_ref,
                *rest):
    if hosts_cast:
        nu_ref, nd_ref, o_ref, sg_ref, sv_ref, nu_bf_ref, nd_bf_ref, tail_g, tail_v = rest
        nu_bf_ref[...] = nu_ref[...].astype(BF16)
        nd_bf_ref[...] = nd_ref[...].astype(BF16)
    elif stream:
        o_ref, sg_ref, sv_ref, tail_g, tail_v = rest
    else:
        pg_ref, pv_ref, o_ref, sg_ref, sv_ref = rest
    i = pl.program_id(0)
    j = pl.program_id(1)
    nf = pl.num_programs(1)
    tm = x_ref.shape[0]

    if stream:
        @pl.when(i % tiles_per_seq == 0)
        def _():
            tail_g[j] = jnp.zeros(tail_g.shape[1:], F32)
            tail_v[j] = jnp.zeros(tail_v.shape[1:], F32)

    xn = xn_ref[...]
    ug = jnp.dot(xn, wg_ref[...], preferred_element_type=F32)
    uv = jnp.dot(xn, wv_ref[...], preferred_element_type=F32)
    cwg, cwv, cbg, cbv = cw_ref[j], cw_ref[j + nf], cb_ref[j], cb_ref[j + nf]
    if stream:
        cg = _conv_stream(ug, tail_g[j], cwg, cbg)
        cv = _conv_stream(uv, tail_v[j], cwv, cbv)
        tail_g[j] = ug[tm - SUBLANES:]
        tail_v[j] = uv[tm - SUBLANES:]
        sg_ref[0, j] = ug[tm - SUBLANES:]
        sv_ref[0, j] = uv[tm - SUBLANES:]
    else:
        cg = _conv_seq8(ug, pg_ref[...], cwg, cbg)
        cv = _conv_seq8(uv, pv_ref[...], cwv, cbv)
        sg_ref[...] = ug.reshape(sg_ref.shape)
        sv_ref[...] = uv.reshape(sv_ref.shape)
    h = (jax.nn.gelu(cg, approximate=True) * cv).astype(BF16)
    acc = jnp.where(j == 0, 0.0, o_ref[...])
    o_ref[...] = acc + jnp.dot(h, wd_ref[...], preferred_element_type=F32)

    @pl.when(j == nf - 1)
    def _():
        o_ref[...] = x_ref[...] + _rmsnorm(o_ref[...], gpost_ref[...])


def _ffn(x, xn, prev, w_up, conv_w, conv_b, w_down, gpost, tm, tf, seq, cast_next=None):
    m = x.shape[0]
    nf = D_FF // tf
    stream = prev is None
    conv_w = conv_w.reshape(CONV_W, 2 * nf, tf).transpose(1, 0, 2)
    conv_b = conv_b.reshape(2 * nf, 1, tf)
    in_specs = [
        pl.BlockSpec((tm, D_MODEL), lambda i, j: (i, 0)),
        pl.BlockSpec((tm, D_MODEL), lambda i, j: (i, 0)),
        pl.BlockSpec((None, D_MODEL, tf), lambda i, j: (0, 0, j)),
        pl.BlockSpec((None, D_MODEL, tf), lambda i, j: (0, 0, j + nf)),
        pl.BlockSpec((2 * nf, CONV_W, tf), lambda i, j: (0, 0, 0)),
        pl.BlockSpec((2 * nf, 1, tf), lambda i, j: (0, 0, 0)),
        pl.BlockSpec((None, tf, D_MODEL), lambda i, j: (0, j, 0)),
        pl.BlockSpec((1, D_MODEL), lambda i, j: (0, 0)),
    ]
    args = [x, xn, w_up, w_up, conv_w, conv_b, w_down, gpost]
    scratch = []
    if stream:
        tiles_per_seq = seq // tm
        state_shape = (m // tm, nf, SUBLANES, tf)
        state_spec = pl.BlockSpec((1, nf, SUBLANES, tf), lambda i, j: (i, 0, 0, 0))
        scratch += [pltpu.VMEM((nf, SUBLANES, tf), F32), pltpu.VMEM((nf, SUBLANES, tf), F32)]
    else:
        assert seq == SUBLANES and m == tm
        tiles_per_seq = 1
        nseq = m // seq
        state_shape = (nseq, SUBLANES, D_FF)
        state_spec = pl.BlockSpec((nseq, SUBLANES, tf), lambda i, j: (0, 0, j))
        in_specs += [
            pl.BlockSpec((nseq, CONV_W - 1, tf), lambda i, j: (0, 0, j)),
            pl.BlockSpec((nseq, CONV_W - 1, tf), lambda i, j: (0, 0, j + nf)),
        ]
        args += [prev, prev]
    out_specs = [pl.BlockSpec((tm, D_MODEL), lambda i, j: (i, 0)), state_spec, state_spec]
    out_shape = [
        jax.ShapeDtypeStruct((m, D_MODEL), F32),
        jax.ShapeDtypeStruct(state_shape, F32),
        jax.ShapeDtypeStruct(state_shape, F32),
    ]
    if cast_next is not None:
        assert stream
        nu, nd, layer_next = cast_next
        steps = (m // tm) * nf
        for w in (nu, nd):
            rows = w.shape[1] // steps
            in_specs.append(pl.BlockSpec((1, rows, w.shape[2]), lambda i, j: (layer_next, i * nf + j, 0)))
            out_specs.append(pl.BlockSpec((1, rows, w.shape[2]), lambda i, j: (0, i * nf + j, 0)))
            out_shape.append(jax.ShapeDtypeStruct((1,) + w.shape[1:], BF16))
            args.append(w)
    return pl.pallas_call(
        functools.partial(_ffn_kernel, stream, cast_next is not None, tiles_per_seq),
        grid=(m // tm, nf),
        in_specs=in_specs,
        out_specs=out_specs,
        out_shape=out_shape,
        scratch_shapes=scratch,
        compiler_params=_params("arbitrary", "arbitrary"),
        name="conv_ffn",
    )(*args)


def _rope_tables(pos):
    half = HEAD_DIM // 2
    inv = ROPE_THETA ** (-jnp.arange(half, dtype=F32) / half)
    ang = pos.astype(F32)[:, None] * inv[None, :]
    cos = jnp.cos(ang)
    sin = jnp.sin(ang)
    reps = LANES // HEAD_DIM
    return (jnp.tile(cos, (1, 2 * reps)), jnp.tile(jnp.concatenate([-sin, sin], axis=1), (1, reps)))


def _conv_state(sg, sv, tiles_per_seq):
    keep = slice(SUBLANES - (CONV_W - 1), SUBLANES)
    if tiles_per_seq is None:
        return jnp.concatenate([sg[:, keep], sv[:, keep]], axis=-1)
    last = slice(tiles_per_seq - 1, None, tiles_per_seq)

    def rows(s):
        s = s[last, :, keep, :]
        return jnp.swapaxes(s, 1, 2).reshape(s.shape[0], CONV_W - 1, -1)

    return jnp.concatenate([rows(sg), rows(sv)], axis=-1)


def _trunk(x, pos, seq, tm, wide_tm, cache, state, state_conv, p):
    m = x.shape[0]
    nseq = m // seq
    sample = cache is not None
    row = lambda v: v.reshape(1, -1)

    cos, sin = _rope_tables(pos)
    q, kv = _qkv_rope(x, row(p["norm_mix_pre"][0]), p["w_qkv"], row(p["attn_b_qkv"][0]), cos, sin, tm,
                      F32 if sample else BF16)
    if sample:
        ck, cv = cache
        o, k_new, v_new = _attn_sample(p["attn_sinks"][0], q, kv, ck.reshape(nseq, WINDOW, KV_W),
                                       cv.reshape(nseq, WINDOW, KV_W), seq)
    else:
        o, p["w_up0"], p["w_down0"], w_in_t, w_out = _attn_prompt(
            p["attn_sinks"][0], q, kv, nseq, seq,
            [(p["ffn_w_up"], 0), (p["ffn_w_down"], 0), (p["mlstm_w_in_t"], 0), (p["mlstm_w_out"], 0)])
        p["w_in_t"], p["w_out"] = w_in_t[0], w_out[0]
        kv3 = kv.reshape(nseq, seq, 2 * KV_W)
        k_new = kv3[:, seq - WINDOW:, :KV_W]
        v_new = kv3[:, seq - WINDOW:, KV_W:]
    k_new = k_new.reshape(1, nseq, WINDOW, N_KV_HEADS, HEAD_DIM)
    v_new = v_new.reshape(1, nseq, WINDOW, N_KV_HEADS, HEAD_DIM)
    x, xn = _proj_post(o, p["w_o"], row(p["attn_b_o"][0]), x, row(p["norm_mix_post"][0]),
                       row(p["norm_ffn_pre"][0]), tm)
    if sample:
        x, sg0, sv0 = _ffn(x, xn, state_conv[0], p["w_up0"], p["ffn_conv_w"][0], p["ffn_conv_b"][0],
                           p["w_down0"], row(p["norm_ffn_post"][0]), tm, FF_TILE, seq)
    else:
        x, sg0, sv0, p["w_up1"], p["w_down1"] = _ffn(
            x, xn, None, p["w_up0"], p["ffn_conv_w"][0], p["ffn_conv_b"][0], p["w_down0"],
            row(p["norm_ffn_post"][0]), tm, FF_TILE, seq, cast_next=(p["ffn_w_up"], p["ffn_w_down"], 1))

    g1 = row(p["norm_mix_pre"][1])
    ng = row(p["mlstm_norm"][0])
    proj, keys, gates = _mlstm_proj(x, g1, p["w_in_t"], p["w_gates_t"], p["b_gates"], wide_tm, not sample)
    if sample:
        c0, n0, m0 = state
        h, c_new, n_new, m_new = _mlstm_sample(proj, keys, gates, ng, c0, n0, m0, seq)
    else:
        h, c_aug, m_new = _mlstm_prompt(proj, keys, gates, ng, nseq, seq)
        h = h.reshape(m, V_W)
        c_new = c_aug[:, :, :M_V_DIM]
        n_new = c_aug[:, :, M_V_DIM]
        m_new = m_new[:, 0]
    c_new = c_new.reshape(1, nseq, M_HEADS, M_QK_DIM, M_V_DIM)
    n_new = n_new.reshape(1, nseq, M_HEADS, M_QK_DIM)
    m_new = m_new.reshape(1, nseq, M_HEADS)
    x, xn = _proj_post(h, p["w_out"], None, x, row(p["norm_mix_post"][1]), row(p["norm_ffn_pre"][1]), tm)
    x, sg1, sv1 = _ffn(x, xn, state_conv[1] if sample else None, p["w_up1"], p["ffn_conv_w"][1],
                       p["ffn_conv_b"][1], p["w_down1"], row(p["norm_ffn_post"][1]), tm, FF_TILE, seq)
    tps = None if sample else seq // tm
    conv = jnp.stack([_conv_state(sg0, sv0, tps), _conv_state(sg1, sv1, tps)])
    return x, k_new, v_new, c_new, n_new, m_new, conv


def kernel(x_prompt, x_sample, cache_k, cache_v, state_C, state_n, state_m, state_conv, norm_mix_pre,
           norm_mix_post, norm_ffn_pre, norm_ffn_post, attn_w_qkv, attn_b_qkv, attn_w_o, attn_b_o, attn_sinks,
           mlstm_w_in, mlstm_b_gates, mlstm_norm, mlstm_w_out, ffn_w_up, ffn_conv_w, ffn_conv_b, ffn_w_down):
    batch, seq, _ = x_prompt.shape
    dec_batch, dec_seq, _ = x_sample.shape
    n_gates = 2 * M_HEADS
    w_in_t = jnp.swapaxes(mlstm_w_in, 1, 2)
    g0 = 2 * QK_W + 2 * V_W
    p = dict(
        norm_mix_pre=norm_mix_pre, norm_mix_post=norm_mix_post, norm_ffn_pre=norm_ffn_pre,
        norm_ffn_post=norm_ffn_post, attn_b_qkv=attn_b_qkv, attn_b_o=attn_b_o, attn_sinks=attn_sinks,
        mlstm_norm=mlstm_norm, ffn_conv_w=ffn_conv_w, ffn_conv_b=ffn_conv_b,
        w_qkv=attn_w_qkv[0].astype(BF16),
        w_o=attn_w_o[0].astype(BF16),
        mlstm_w_in_t=w_in_t, mlstm_w_out=mlstm_w_out,
        w_gates_t=jnp.pad(w_in_t[0, g0:].astype(BF16), ((0, GATE_PAD - n_gates), (0, 0))),
        b_gates=jnp.pad(mlstm_b_gates[0], (0, GATE_PAD - n_gates)).reshape(1, GATE_PAD),
        ffn_w_up=ffn_w_up, ffn_w_down=ffn_w_down,
    )
    yp, k_p, v_p, c_p, n_p, m_p, conv_p = _trunk(
        x_prompt.reshape(batch * seq, D_MODEL), jnp.arange(seq), seq, ROW_TILE, WIDE_ROW_TILE, None, None, None, p)
    ys, k_s, v_s, c_s, n_s, m_s, conv_s = _trunk(
        x_sample.reshape(dec_batch * dec_seq, D_MODEL),
        jnp.tile(PAST_LEN + jnp.arange(dec_seq), dec_batch), dec_seq, dec_batch * dec_seq, dec_batch * dec_seq,
        (cache_k[0], cache_v[0]), (state_C[0], state_n[0], state_m[0]), state_conv, p)
    return (yp.reshape(batch, seq, D_MODEL), ys.reshape(dec_batch, dec_seq, D_MODEL),
            k_p, v_p, k_s, v_s, c_p, n_p, m_p, c_s, n_s, m_s, conv_p, conv_s)
```

```python
import functools

import jax
import jax.numpy as jnp
from jax import lax
from jax.experimental import pallas as pl
from jax.experimental.pallas import tpu as pltpu

F32 = jnp.float32
BF16 = jnp.bfloat16

D_MODEL = 2048
WINDOW = 128
HEAD_DIM = 64
N_HEADS = 32
N_KV_HEADS = 4
GROUP = N_HEADS // N_KV_HEADS
Q_W = N_HEADS * HEAD_DIM
KV_W = N_KV_HEADS * HEAD_DIM
ROPE_THETA = 10000.0
M_HEADS = 4
M_QK_DIM = 256
M_V_DIM = 512
QK_W = M_HEADS * M_QK_DIM
V_W = M_HEADS * M_V_DIM
GATE_PAD = 128
PROJ_W = QK_W + 2 * V_W
D_FF = 4 * D_MODEL
CONV_W = 3
EPS = 1e-6
NEG_INF = -1e30
PAST_LEN = 16384

SUBLANES = 8
LANES = 128
VMEM_LIMIT_BYTES = 60 * 1024 * 1024

ROW_TILE = 512
WIDE_ROW_TILE = 1024
FF_TILE = 1024
PROJ_N_TILE = 1024
MLSTM_CHUNK = 256
MLSTM_SAMPLE_SEQS = 2


def _params(*sem):
    return pltpu.CompilerParams(dimension_semantics=sem, vmem_limit_bytes=VMEM_LIMIT_BYTES)


def _rmsnorm(xf, g):
    r = xf * lax.rsqrt(jnp.mean(xf * xf, axis=-1, keepdims=True) + EPS)
    return r * g


def _qkv_rope_kernel(x_ref, g_ref, w_ref, b_ref, cos_ref, sin_ref, q_ref, kv_ref):
    xn = _rmsnorm(x_ref[...], g_ref[...]).astype(BF16)
    y = jnp.dot(xn, w_ref[...], preferred_element_type=F32) + b_ref[...]
    cos = cos_ref[...]
    sin = sin_ref[...]
    lane = lax.broadcasted_iota(jnp.int32, cos.shape, 1)
    first_half = (lane & (HEAD_DIM - 1)) < (HEAD_DIM // 2)
    n_rot = (Q_W + KV_W) // LANES
    for c in range(n_rot):
        blk = y[:, c * LANES:(c + 1) * LANES]
        sw = jnp.where(first_half, pltpu.roll(blk, LANES - HEAD_DIM // 2, 1),
                       pltpu.roll(blk, HEAD_DIM // 2, 1))
        r = blk * cos + sw * sin
        if c < Q_W // LANES:
            q_ref[:, c * LANES:(c + 1) * LANES] = (r * (HEAD_DIM ** -0.5)).astype(q_ref.dtype)
        else:
            o = c * LANES - Q_W
            kv_ref[:, o:o + LANES] = r
    kv_ref[:, KV_W:2 * KV_W] = y[:, Q_W + KV_W:Q_W + 2 * KV_W]


def _qkv_rope(x, g, w, b, cos, sin, tm, q_dtype):
    m = x.shape[0]
    n = w.shape[1]
    n_pos_tiles = cos.shape[0] // tm
    return pl.pallas_call(
        _qkv_rope_kernel,
        grid=(m // tm,),
        in_specs=[
            pl.BlockSpec((tm, D_MODEL), lambda i: (i, 0)),
            pl.BlockSpec((1, D_MODEL), lambda i: (0, 0)),
            pl.BlockSpec((D_MODEL, n), lambda i: (0, 0)),
            pl.BlockSpec((1, n), lambda i: (0, 0)),
            pl.BlockSpec((tm, LANES), lambda i: (i % n_pos_tiles, 0)),
            pl.BlockSpec((tm, LANES), lambda i: (i % n_pos_tiles, 0)),
        ],
        out_specs=[
            pl.BlockSpec((tm, Q_W), lambda i: (i, 0)),
            pl.BlockSpec((tm, 2 * KV_W), lambda i: (i, 0)),
        ],
        out_shape=[
            jax.ShapeDtypeStruct((m, Q_W), q_dtype),
            jax.ShapeDtypeStruct((m, 2 * KV_W), F32),
        ],
        compiler_params=_params("arbitrary"),
        name="qkv_rope",
    )(x, g, w, b, cos, sin)


def _attn_prompt_kernel(n_cast, sinks_ref, q_ref, kvp_ref, kvc_ref, *rest):
    o_ref = rest[n_cast]
    for src_ref, dst_ref in zip(rest[:n_cast], rest[n_cast + 1:]):
        dst_ref[...] = src_ref[...].astype(BF16)
    n = pl.program_id(1)
    w = WINDOW
    ri = lax.broadcasted_iota(jnp.int32, (w, 2 * w), 0)
    ci = lax.broadcasted_iota(jnp.int32, (w, 2 * w), 1)
    has_prev = jnp.full((w, 2 * w), n, jnp.int32) > 0
    allowed = ((ci < w) & (ci > ri) & has_prev) | ((ci >= w) & ((ci - w) <= ri))
    low = lax.broadcasted_iota(jnp.int32, (w, LANES), 1) < HEAD_DIM
    ones = jnp.ones((2 * w, HEAD_DIM), BF16)
    for g in range(N_KV_HEADS):
        ks = slice(g * HEAD_DIM, (g + 1) * HEAD_DIM)
        vs = slice(KV_W + g * HEAD_DIM, KV_W + (g + 1) * HEAD_DIM)
        k = jnp.concatenate([kvp_ref[:, ks], kvc_ref[:, ks]], axis=0).astype(BF16)
        v = jnp.concatenate([kvp_ref[:, vs], kvc_ref[:, vs]], axis=0).astype(BF16)
        v1 = jnp.concatenate([v, ones], axis=1)
        q = jnp.concatenate(
            [q_ref[:, (g * GROUP + h) * HEAD_DIM:(g * GROUP + h + 1) * HEAD_DIM] for h in range(GROUP)],
            axis=0)
        s = lax.dot_general(q, k, (((1,), (1,)), ((), ())), preferred_element_type=F32)
        ps, es = [], []
        for h in range(GROUP):
            sh = jnp.where(allowed, s[h * w:(h + 1) * w], NEG_INF)
            sink = sinks_ref[g * GROUP + h]
            m = jnp.maximum(jnp.max(sh, axis=-1, keepdims=True), sink)
            ps.append(jnp.exp(sh - m).astype(BF16))
            es.append(jnp.exp(sink - m))
        oa = jnp.dot(jnp.concatenate(ps, axis=0), v1, preferred_element_type=F32)
        for h in range(0, GROUP, 2):
            a = oa[h * w:(h + 1) * w]
            b = oa[(h + 1) * w:(h + 2) * w]
            num = jnp.where(low, a, pltpu.roll(b, HEAD_DIM, 1))
            den = jnp.where(low, pltpu.roll(a, HEAD_DIM, 1) + es[h], b + es[h + 1])
            c0 = (g * GROUP + h) * HEAD_DIM
            o_ref[:, c0:c0 + LANES] = (num / den).astype(o_ref.dtype)


def _attn_prompt(sinks, q, kv, batch, seq, casts):
    nb = seq // WINDOW
    steps = batch * nb
    cast_in, cast_out, cast_shape = [], [], []
    for w, layer in casts:
        rows = w.shape[1] // steps
        cast_in.append(pl.BlockSpec((1, rows, w.shape[2]), lambda b, n, layer=layer: (layer, b * nb + n, 0)))
        cast_out.append(pl.BlockSpec((1, rows, w.shape[2]), lambda b, n: (0, b * nb + n, 0)))
        cast_shape.append(jax.ShapeDtypeStruct((1, rows * steps, w.shape[2]), BF16))
    return pl.pallas_call(
        functools.partial(_attn_prompt_kernel, len(casts)),
        grid=(batch, nb),
        in_specs=[
            pl.BlockSpec(memory_space=pltpu.SMEM),
            pl.BlockSpec((WINDOW, Q_W), lambda b, n: (b * nb + n, 0)),
            pl.BlockSpec((WINDOW, 2 * KV_W), lambda b, n: (b * nb + jnp.maximum(n - 1, 0), 0)),
            pl.BlockSpec((WINDOW, 2 * KV_W), lambda b, n: (b * nb + n, 0)),
        ] + cast_in,
        out_specs=[pl.BlockSpec((WINDOW, Q_W), lambda b, n: (b * nb + n, 0))] + cast_out,
        out_shape=[jax.ShapeDtypeStruct((batch * seq, Q_W), BF16)] + cast_shape,
        compiler_params=_params("arbitrary", "arbitrary"),
        name="attn_prompt",
    )(sinks, q, kv, kv, *[w for w, _ in casts])


def _attn_sample_kernel(sinks_ref, q_ref, kvn_ref, ck_ref, cv_ref, o_ref, ko_ref, vo_ref):
    t = q_ref.shape[0]
    wc = ck_ref.shape[1]
    rows = N_HEADS * t
    ri = lax.broadcasted_iota(jnp.int32, (rows, wc), 0) & (t - 1)
    ci = lax.broadcasted_iota(jnp.int32, (rows, wc), 1)
    allowed_c = ci > ri
    ri_n = lax.broadcasted_iota(jnp.int32, (rows, t), 0) & (t - 1)
    ci_n = lax.broadcasted_iota(jnp.int32, (rows, t), 1)
    allowed_n = ci_n <= ri_n
    nt = (((1,), (1,)), ((), ()))
    s_c, s_n = [], []
    for g in range(N_KV_HEADS):
        ks = slice(g * HEAD_DIM, (g + 1) * HEAD_DIM)
        q = jnp.concatenate(
            [q_ref[:, (g * GROUP + h) * HEAD_DIM:(g * GROUP + h + 1) * HEAD_DIM] for h in range(GROUP)],
            axis=0)
        s_c.append(lax.dot_general(q.astype(BF16), ck_ref[0, :, ks].astype(BF16), nt, preferred_element_type=F32))
        s_n.append(lax.dot_general(q, kvn_ref[:, ks], nt, preferred_element_type=F32))
    s_c = jnp.where(allowed_c, jnp.concatenate(s_c, axis=0), NEG_INF)
    s_n = jnp.where(allowed_n, jnp.concatenate(s_n, axis=0), NEG_INF)
    sink = jnp.concatenate([jnp.full((t, 1), sinks_ref[h], F32) for h in range(N_HEADS)], axis=0)
    m = jnp.maximum(jnp.maximum(jnp.max(s_c, axis=-1, keepdims=True),
                                jnp.max(s_n, axis=-1, keepdims=True)), sink)
    p_c = jnp.exp(s_c - m)
    p_n = jnp.exp(s_n - m)
    inv_l = 1.0 / (jnp.sum(p_c, axis=-1, keepdims=True) + jnp.sum(p_n, axis=-1, keepdims=True)
                   + jnp.exp(sink - m))
    p_c = (p_c * inv_l).astype(BF16)
    p_n = p_n * inv_l
    gr = GROUP * t
    for g in range(N_KV_HEADS):
        ks = slice(g * HEAD_DIM, (g + 1) * HEAD_DIM)
        vs = slice(KV_W + g * HEAD_DIM, KV_W + (g + 1) * HEAD_DIM)
        o = (jnp.dot(p_c[g * gr:(g + 1) * gr], cv_ref[0, :, ks].astype(BF16), preferred_element_type=F32)
             + jnp.dot(p_n[g * gr:(g + 1) * gr], kvn_ref[:, vs], preferred_element_type=F32))
        for h in range(GROUP):
            c0 = (g * GROUP + h) * HEAD_DIM
            o_ref[:, c0:c0 + HEAD_DIM] = o[h * t:(h + 1) * t]
    ko_ref[0, 0:wc - t, :] = ck_ref[0, t:wc, :]
    ko_ref[0, wc - t:wc, :] = kvn_ref[:, 0:KV_W]
    vo_ref[0, 0:wc - t, :] = cv_ref[0, t:wc, :]
    vo_ref[0, wc - t:wc, :] = kvn_ref[:, KV_W:2 * KV_W]


def _attn_sample(sinks, q, kvn, ck, cv, t):
    db = ck.shape[0]
    wc = ck.shape[1]
    return pl.pallas_call(
        _attn_sample_kernel,
        grid=(db,),
        in_specs=[
            pl.BlockSpec(memory_space=pltpu.SMEM),
            pl.BlockSpec((t, Q_W), lambda b: (b, 0)),
            pl.BlockSpec((t, 2 * KV_W), lambda b: (b, 0)),
            pl.BlockSpec((1, wc, KV_W), lambda b: (b, 0, 0)),
            pl.BlockSpec((1, wc, KV_W), lambda b: (b, 0, 0)),
        ],
        out_specs=[
            pl.BlockSpec((t, Q_W), lambda b: (b, 0)),
            pl.BlockSpec((1, wc, KV_W), lambda b: (b, 0, 0)),
            pl.BlockSpec((1, wc, KV_W), lambda b: (b, 0, 0)),
        ],
        out_shape=[
            jax.ShapeDtypeStruct((db * t, Q_W), F32),
            jax.ShapeDtypeStruct((db, wc, KV_W), F32),
            jax.ShapeDtypeStruct((db, wc, KV_W), F32),
        ],
        compiler_params=_params("arbitrary"),
        name="attn_sample",
    )(sinks, q, kvn, ck, cv)


def _proj_post_kernel(has_bias, a_ref, w_ref, b_ref, x_ref, g_ref, gn_ref, o_ref, xn_ref):
    y = jnp.dot(a_ref[...].astype(BF16), w_ref[...], preferred_element_type=F32)
    if has_bias:
        y = y + b_ref[...]
    x_new = x_ref[...] + _rmsnorm(y, g_ref[...])
    o_ref[...] = x_new
    xn_ref[...] = _rmsnorm(x_new, gn_ref[...]).astype(xn_ref.dtype)


def _proj_post(a, w, b, x, g, g_next, tm):
    m, k = a.shape
    has_bias = b is not None
    if b is None:
        b = jnp.zeros((1, D_MODEL), F32)
    return pl.pallas_call(
        functools.partial(_proj_post_kernel, has_bias),
        grid=(m // tm,),
        in_specs=[
            pl.BlockSpec((tm, k), lambda i: (i, 0)),
            pl.BlockSpec((k, D_MODEL), lambda i: (0, 0)),
            pl.BlockSpec((1, D_MODEL), lambda i: (0, 0)),
            pl.BlockSpec((tm, D_MODEL), lambda i: (i, 0)),
            pl.BlockSpec((1, D_MODEL), lambda i: (0, 0)),
            pl.BlockSpec((1, D_MODEL), lambda i: (0, 0)),
        ],
        out_specs=[pl.BlockSpec((tm, D_MODEL), lambda i: (i, 0)), pl.BlockSpec((tm, D_MODEL), lambda i: (i, 0))],
        out_shape=[jax.ShapeDtypeStruct((m, D_MODEL), F32), jax.ShapeDtypeStruct((m, D_MODEL), BF16)],
        compiler_params=_params("arbitrary"),
        name="proj_post",
    )(a, w, b, x, g, g_next)


_NT = (((1,), (1,)), ((), ()))


def _mlstm_proj_kernel(keys_transposed, n_qv, x_ref, g_ref, wt_ref, wk_ref, wg_ref, bg_ref, qv_ref, o_ref, k_ref,
                       gates_ref, xn_ref):
    j = pl.program_id(1)

    @pl.when(j == 0)
    def _():
        xn = _rmsnorm(x_ref[...], g_ref[...]).astype(BF16)
        xn_ref[...] = xn
        if keys_transposed:
            k_ref[...] = lax.dot_general(wk_ref[...], xn, _NT, preferred_element_type=F32)
        else:
            k_ref[...] = lax.dot_general(xn, wk_ref[...], _NT, preferred_element_type=F32)
        gates_ref[...] = lax.dot_general(xn, wg_ref[...], _NT, preferred_element_type=F32) + bg_ref[...]

    def tile():
        return lax.dot_general(xn_ref[...], wt_ref[...], _NT, preferred_element_type=F32)

    @pl.when(j < n_qv)
    def _():
        qv_ref[...] = tile().astype(qv_ref.dtype)

    @pl.when(j >= n_qv)
    def _():
        o_ref[...] = tile()


def _mlstm_proj(x, g, w_t, wg_t, bg, tm, keys_transposed, qv_dtype):
    m = x.shape[0]
    tn = PROJ_N_TILE
    assert QK_W == tn
    n_qv = (QK_W + V_W) // tn
    k_spec = pl.BlockSpec((QK_W, tm), lambda i, j: (0, i), pipeline_mode=pl.Buffered(1)) if keys_transposed else \
        pl.BlockSpec((tm, QK_W), lambda i, j: (i, 0), pipeline_mode=pl.Buffered(1))
    return pl.pallas_call(
        functools.partial(_mlstm_proj_kernel, keys_transposed, n_qv),
        grid=(m // tm, PROJ_W // tn),
        in_specs=[
            pl.BlockSpec((tm, D_MODEL), lambda i, j: (i, 0)),
            pl.BlockSpec((1, D_MODEL), lambda i, j: (0, 0)),
            pl.BlockSpec((tn, D_MODEL), lambda i, j: (j + jnp.minimum(j, 1), 0)),
            pl.BlockSpec((QK_W, D_MODEL), lambda i, j: (1, 0), pipeline_mode=pl.Buffered(1)),
            pl.BlockSpec((GATE_PAD, D_MODEL), lambda i, j: (0, 0)),
            pl.BlockSpec((1, GATE_PAD), lambda i, j: (0, 0)),
        ],
        out_specs=[
            pl.BlockSpec((tm, tn), lambda i, j: (i, jnp.minimum(j, n_qv - 1))),
            pl.BlockSpec((tm, tn), lambda i, j: (i, jnp.maximum(j - n_qv, 0))),
            k_spec,
            pl.BlockSpec((tm, GATE_PAD), lambda i, j: (i, 0)),
        ],
        out_shape=[
            jax.ShapeDtypeStruct((m, QK_W + V_W), qv_dtype),
            jax.ShapeDtypeStruct((m, V_W), F32),
            jax.ShapeDtypeStruct((QK_W, m) if keys_transposed else (m, QK_W), F32),
            jax.ShapeDtypeStruct((m, GATE_PAD), F32),
        ],
        scratch_shapes=[pltpu.VMEM((tm, D_MODEL), BF16)],
        compiler_params=_params("arbitrary", "arbitrary"),
        name="mlstm_proj",
    )(x, g, w_t, w_t, wg_t, bg)


def _log_sigmoid(x):
    return jnp.minimum(x, 0.0) - jnp.log1p(jnp.exp(-jnp.abs(x)))


def _chunk_gates(li, lf, m_state):
    L = li.shape[0]
    ri = lax.broadcasted_iota(jnp.int32, (L, L), 0)
    ci = lax.broadcasted_iota(jnp.int32, (L, L), 1)
    eye = ri == ci
    tril = ci <= ri
    lf_row = jnp.sum(jnp.where(eye, lf, 0.0), axis=0, keepdims=True)
    li_row = jnp.sum(jnp.where(eye, li, 0.0), axis=0, keepdims=True)
    b_col = jnp.sum(jnp.where(tril, lf_row, 0.0), axis=1, keepdims=True)
    b_row = jnp.sum(jnp.where(ri <= ci, lf, 0.0), axis=0, keepdims=True)
    dmat = jnp.where(tril, b_col - b_row + li_row, -jnp.inf)
    inter = b_col + m_state
    m_row = jnp.maximum(inter, jnp.max(dmat, axis=1, keepdims=True))
    b_last = b_col[L - 1:L, :]
    gk_col = b_last - b_col + li
    gk_row = b_last - b_row + li_row
    m_new = jnp.maximum(b_last + m_state, jnp.max(gk_col, axis=0, keepdims=True))
    decay = jnp.exp(b_last + m_state - m_new)
    return jnp.exp(dmat - m_row), jnp.exp(inter - m_row), m_row, gk_col, gk_row, m_new, decay


def _mlstm_chunk(q, k, v, li, lf, c_state, n_state, m_state):
    dexp, w_inter, m_row, gk_col, _, m_new, decay = _chunk_gates(li, lf, m_state)
    s = lax.dot_general(q, k, (((1,), (1,)), ((), ())), preferred_element_type=F32)
    sm = s * dexp
    num = (jnp.dot(sm, v, preferred_element_type=F32)
           + w_inter * jnp.dot(q, c_state, preferred_element_type=F32))
    den = jnp.sum(sm, axis=1, keepdims=True) + w_inter * jnp.sum(q * n_state, axis=1, keepdims=True)
    h = num * (1.0 / jnp.maximum(jnp.abs(den), jnp.exp(-m_row)))
    kw = k * jnp.exp(gk_col - m_new)
    c_new = decay * c_state + lax.dot_general(kw, v, (((0,), (0,)), ((), ())), preferred_element_type=F32)
    n_new = decay * n_state + jnp.sum(kw, axis=0, keepdims=True)
    return h, c_new, n_new, m_new


def _mlstm_chunk_t(q, kt, v1, li, lf, c_aug, m_state):
    dk = q.shape[1]
    dv = v1.shape[1] - LANES
    scale = dk ** -0.5
    dexp, w_inter, m_row, _, gk_row, m_new, decay = _chunk_gates(li, lf, m_state)
    qm = q.astype(BF16)
    s = jnp.dot(qm, kt.astype(BF16), preferred_element_type=F32)
    sm = (s * scale) * dexp
    num = (jnp.dot(sm.astype(BF16), v1, preferred_element_type=F32)
           + (w_inter * scale) * jnp.dot(qm, c_aug.astype(BF16), preferred_element_type=F32))
    den = num[:, dv:dv + 1]
    h = num[:, :dv] * (1.0 / jnp.maximum(jnp.abs(den), jnp.exp(-m_row)))
    kwt = (kt * jnp.exp(gk_row - m_new)).astype(BF16)
    c_new = decay * c_aug + jnp.dot(kwt, v1, preferred_element_type=F32)
    return h, c_new, m_new


def _head_cols(hd):
    q0 = hd * M_QK_DIM
    v0 = QK_W + hd * M_V_DIM
    o0 = hd * M_V_DIM
    return (q0, q0 + M_QK_DIM), (v0, v0 + M_V_DIM), (o0, o0 + M_V_DIM), hd, M_HEADS + hd


def _head_out(h, o, ng):
    return _rmsnorm(h, ng) * (0.5 * (jnp.tanh(0.5 * o) + 1.0))


def _mlstm_prompt_kernel(batch, qv_ref, og_ref, gates_ref, *rest):
    kt_refs = rest[:batch]
    ng_ref, h_ref, c_ref, m_ref = rest[batch:]

    @pl.when(pl.program_id(0) == 0)
    def _():
        c_ref[...] = jnp.zeros(c_ref.shape, F32)
        m_ref[...] = jnp.zeros(m_ref.shape, F32)

    L = qv_ref.shape[1]
    one_col = (lax.broadcasted_iota(jnp.int32, (L, LANES), 1) == 0).astype(BF16)
    for b in range(batch):
        for hd in range(M_HEADS):
            r = b * M_HEADS + hd
            (q0, q1), (v0, v1), (o0, o1), gi, gf = _head_cols(hd)
            va = jnp.concatenate([qv_ref[b, :, v0:v1], one_col], axis=1)
            h, c_new, m_new = _mlstm_chunk_t(
                qv_ref[b, :, q0:q1], kt_refs[b][hd * M_QK_DIM:(hd + 1) * M_QK_DIM, :], va,
                gates_ref[b, :, gi:gi + 1], _log_sigmoid(gates_ref[b, :, gf:gf + 1]),
                c_ref[r], m_ref[r:r + 1, 0:1])
            hn = _head_out(h, og_ref[b, :, o0:o1], ng_ref[:, hd * M_V_DIM:(hd + 1) * M_V_DIM])
            h_ref[b, :, hd * M_V_DIM:(hd + 1) * M_V_DIM] = hn.astype(h_ref.dtype)
            c_ref[r] = c_new
            m_ref[r:r + 1, :] = jnp.broadcast_to(m_new, (1, LANES))


def _mlstm_prompt(qv, og, kt, gates, ng, batch, seq):
    chunk = MLSTM_CHUNK
    nc = seq // chunk
    rows = batch * M_HEADS
    return pl.pallas_call(
        functools.partial(_mlstm_prompt_kernel, batch),
        grid=(nc,),
        in_specs=[pl.BlockSpec((batch, chunk, QK_W + V_W), lambda c: (0, c, 0)),
                  pl.BlockSpec((batch, chunk, V_W), lambda c: (0, c, 0)),
                  pl.BlockSpec((batch, chunk, GATE_PAD), lambda c: (0, c, 0))]
        + [pl.BlockSpec((QK_W, chunk), lambda c, b=b: (0, b * nc + c)) for b in range(batch)]
        + [pl.BlockSpec((1, V_W), lambda c: (0, 0))],
        out_specs=[
            pl.BlockSpec((batch, chunk, V_W), lambda c: (0, c, 0)),
            pl.BlockSpec((rows, M_QK_DIM, M_V_DIM + LANES), lambda c: (0, 0, 0)),
            pl.BlockSpec((rows, LANES), lambda c: (0, 0)),
        ],
        out_shape=[
            jax.ShapeDtypeStruct((batch, seq, V_W), BF16),
            jax.ShapeDtypeStruct((rows, M_QK_DIM, M_V_DIM + LANES), F32),
            jax.ShapeDtypeStruct((rows, LANES), F32),
        ],
        compiler_params=_params("arbitrary"),
        name="mlstm_prompt",
    )(qv.reshape(batch, seq, QK_W + V_W), og.reshape(batch, seq, V_W), gates.reshape(batch, seq, GATE_PAD),
      *([kt] * batch), ng)


def _mlstm_sample_kernel(t, qv_ref, og_ref, k_ref, gates_ref, ng_ref, c0_ref, n0_ref, m0_ref, h_ref, c_ref, n_ref,
                         m_ref):
    for s in range(c0_ref.shape[0]):
        rows = slice(s * t, (s + 1) * t)
        for hd in range(M_HEADS):
            (q0, q1), (v0, v1), (o0, o1), gi, gf = _head_cols(hd)
            h, c_new, n_new, m_new = _mlstm_chunk(
                qv_ref[rows, q0:q1] * (M_QK_DIM ** -0.5), k_ref[rows, hd * M_QK_DIM:(hd + 1) * M_QK_DIM],
                qv_ref[rows, v0:v1], gates_ref[rows, gi:gi + 1], _log_sigmoid(gates_ref[rows, gf:gf + 1]),
                c0_ref[s, hd], n0_ref[s, hd:hd + 1, :], m0_ref[s, :, hd:hd + 1])
            h_ref[rows, hd * M_V_DIM:(hd + 1) * M_V_DIM] = _head_out(
                h, og_ref[rows, o0:o1], ng_ref[:, hd * M_V_DIM:(hd + 1) * M_V_DIM])
            c_ref[s, hd] = c_new
            n_ref[s, hd:hd + 1, :] = n_new
            m_ref[s, :, hd:hd + 1] = m_new


def _mlstm_sample(qv, og, k, gates, ng, c0, n0, m0, t):
    db = c0.shape[0]
    sb = MLSTM_SAMPLE_SEQS
    m0 = m0.reshape(db, 1, M_HEADS)
    state_specs = [
        pl.BlockSpec((sb, M_HEADS, M_QK_DIM, M_V_DIM), lambda b: (b, 0, 0, 0)),
        pl.BlockSpec((sb, M_HEADS, M_QK_DIM), lambda b: (b, 0, 0)),
        pl.BlockSpec((sb, 1, M_HEADS), lambda b: (b, 0, 0)),
    ]
    return pl.pallas_call(
        functools.partial(_mlstm_sample_kernel, t),
        grid=(db // sb,),
        in_specs=[
            pl.BlockSpec((sb * t, QK_W + V_W), lambda b: (b, 0)),
            pl.BlockSpec((sb * t, V_W), lambda b: (b, 0)),
            pl.BlockSpec((sb * t, QK_W), lambda b: (b, 0)),
            pl.BlockSpec((sb * t, GATE_PAD), lambda b: (b, 0)),
            pl.BlockSpec((1, V_W), lambda b: (0, 0)),
        ] + state_specs,
        out_specs=[pl.BlockSpec((sb * t, V_W), lambda b: (b, 0))] + state_specs,
        out_shape=[
            jax.ShapeDtypeStruct((db * t, V_W), F32),
            jax.ShapeDtypeStruct(c0.shape, F32),
            jax.ShapeDtypeStruct(n0.shape, F32),
            jax.ShapeDtypeStruct(m0.shape, F32),
        ],
        compiler_params=_params("arbitrary"),
        name="mlstm_sample",
    )(qv, og, k, gates, ng, c0, n0, m0)


def _conv3(u, u1, u2, w, b):
    return ((b + w[0:1] * u2) + w[1:2] * u1) + w[2:3] * u


def _conv_stream(u, tail, w, b):
    u1 = pltpu.roll(u, 1, 0)
    u2 = pltpu.roll(u, 2, 0)
    c = _conv3(u, u1, u2, w, b)
    uf = u[0:SUBLANES]
    row = lax.broadcasted_iota(jnp.int32, uf.shape, 0)
    uf1 = jnp.where(row < 1, pltpu.roll(tail, 1, 0), pltpu.roll(uf, 1, 0))
    uf2 = jnp.where(row < 2, pltpu.roll(tail, 2, 0), pltpu.roll(uf, 2, 0))
    cf = _conv3(uf, uf1, uf2, w, b)
    return jnp.concatenate([cf, c[SUBLANES:]], axis=0)


def _conv_seq8(u, prev, w, b):
    nseq = u.shape[0] // SUBLANES
    p0 = jnp.broadcast_to(prev[:, 0:1, :], (nseq, SUBLANES, u.shape[1])).reshape(u.shape)
    p1 = jnp.broadcast_to(prev[:, 1:2, :], (nseq, SUBLANES, u.shape[1])).reshape(u.shape)
    row = lax.broadcasted_iota(jnp.int32, u.shape, 0) & (SUBLANES - 1)
    u1 = jnp.where(row == 0, p1, pltpu.roll(u, 1, 0))
    u2 = jnp.where(row == 0, p0, jnp.where(row == 1, p1, pltpu.roll(u, 2, 0)))
    return _conv3(u, u1, u2, w, b)


def _ffn_kernel(stream, hosts_cast, tiles_per_seq, x_ref, xn_ref, wg_ref, wv_ref, cw_ref, cb_ref, wd_ref, gpost_ref,
                *rest):
    if hosts_cast:
        nu_ref, nd_ref, o_ref, sg_ref, sv_ref, nu_bf_ref, nd_bf_ref, tail_g, tail_v = rest
        nu_bf_ref[...] = nu_ref[...].astype(BF16)
        nd_bf_ref[...] = nd_ref[...].astype(BF16)
    elif stream:
        o_ref, sg_ref, sv_ref, tail_g, tail_v = rest
    else:
        pg_ref, pv_ref, o_ref, sg_ref, sv_ref = rest
    i = pl.program_id(0)
    j = pl.program_id(1)
    nf = pl.num_programs(1)
    tm = x_ref.shape[0]

    if stream:
        @pl.when(i % tiles_per_seq == 0)
        def _():
            tail_g[j] = jnp.zeros(tail_g.shape[1:], F32)
            tail_v[j] = jnp.zeros(tail_v.shape[1:], F32)

    xn = xn_ref[...]
    ug = jnp.dot(xn, wg_ref[...], preferred_element_type=F32)
    uv = jnp.dot(xn, wv_ref[...], preferred_element_type=F32)
    cwg, cwv, cbg, cbv = cw_ref[j], cw_ref[j + nf], cb_ref[j], cb_ref[j + nf]
    if stream:
        cg = _conv_stream(ug, tail_g[j], cwg, cbg)
        cv = _conv_stream(uv, tail_v[j], cwv, cbv)
        tail_g[j] = ug[tm - SUBLANES:]
        tail_v[j] = uv[tm - SUBLANES:]
        sg_ref[0, j] = ug[tm - SUBLANES:]
        sv_ref[0, j] = uv[tm - SUBLANES:]
    else:
        cg = _conv_seq8(ug, pg_ref[...], cwg, cbg)
        cv = _conv_seq8(uv, pv_ref[...], cwv, cbv)
        sg_ref[...] = ug.reshape(sg_ref.shape)
        sv_ref[...] = uv.reshape(sv_ref.shape)
    h = (jax.nn.gelu(cg, approximate=True) * cv).astype(BF16)
    acc = jnp.where(j == 0, 0.0, o_ref[...])
    o_ref[...] = acc + jnp.dot(h, wd_ref[...], preferred_element_type=F32)

    @pl.when(j == nf - 1)
    def _():
        o_ref[...] = x_ref[...] + _rmsnorm(o_ref[...], gpost_ref[...])


def _ffn(x, xn, prev, w_up, conv_w, conv_b, w_down, gpost, tm, tf, seq, cast_next=None):
    m = x.shape[0]
    nf = D_FF // tf
    stream = prev is None
    conv_w = conv_w.reshape(CONV_W, 2 * nf, tf).transpose(1, 0, 2)
    conv_b = conv_b.reshape(2 * nf, 1, tf)
    in_specs = [
        pl.BlockSpec((tm, D_MODEL), lambda i, j: (i, 0)),
        pl.BlockSpec((tm, D_MODEL), lambda i, j: (i, 0)),
        pl.BlockSpec((None, D_MODEL, tf), lambda i, j: (0, 0, j)),
        pl.BlockSpec((None, D_MODEL, tf), lambda i, j: (0, 0, j + nf)),
        pl.BlockSpec((2 * nf, CONV_W, tf), lambda i, j: (0, 0, 0)),
        pl.BlockSpec((2 * nf, 1, tf), lambda i, j: (0, 0, 0)),
        pl.BlockSpec((None, tf, D_MODEL), lambda i, j: (0, j, 0)),
        pl.BlockSpec((1, D_MODEL), lambda i, j: (0, 0)),
    ]
    args = [x, xn, w_up, w_up, conv_w, conv_b, w_down, gpost]
    scratch = []
    if stream:
        tiles_per_seq = seq // tm
        state_shape = (m // tm, nf, SUBLANES, tf)
        state_spec = pl.BlockSpec((1, nf, SUBLANES, tf), lambda i, j: (i, 0, 0, 0))
        scratch += [pltpu.VMEM((nf, SUBLANES, tf), F32), pltpu.VMEM((nf, SUBLANES, tf), F32)]
    else:
        assert seq == SUBLANES and m == tm
        tiles_per_seq = 1
        nseq = m // seq
        state_shape = (nseq, SUBLANES, D_FF)
        state_spec = pl.BlockSpec((nseq, SUBLANES, tf), lambda i, j: (0, 0, j))
        in_specs += [
            pl.BlockSpec((nseq, CONV_W - 1, tf), lambda i, j: (0, 0, j)),
            pl.BlockSpec((nseq, CONV_W - 1, tf), lambda i, j: (0, 0, j + nf)),
        ]
        args += [prev, prev]
    out_specs = [pl.BlockSpec((tm, D_MODEL), lambda i, j: (i, 0)), state_spec, state_spec]
    out_shape = [
        jax.ShapeDtypeStruct((m, D_MODEL), F32),
        jax.ShapeDtypeStruct(state_shape, F32),
        jax.ShapeDtypeStruct(state_shape, F32),
    ]
    if cast_next is not None:
        assert stream
        nu, nd, layer_next = cast_next
        steps = (m // tm) * nf
        for w in (nu, nd):
            rows = w.shape[1] // steps
            in_specs.append(pl.BlockSpec((1, rows, w.shape[2]), lambda i, j: (layer_next, i * nf + j, 0)))
            out_specs.append(pl.BlockSpec((1, rows, w.shape[2]), lambda i, j: (0, i * nf + j, 0)))
            out_shape.append(jax.ShapeDtypeStruct((1,) + w.shape[1:], BF16))
            args.append(w)
    return pl.pallas_call(
        functools.partial(_ffn_kernel, stream, cast_next is not None, tiles_per_seq),
        grid=(m // tm, nf),
        in_specs=in_specs,
        out_specs=out_specs,
        out_shape=out_shape,
        scratch_shapes=scratch,
        compiler_params=_params("arbitrary", "arbitrary"),
        name="conv_ffn",
    )(*args)


def _rope_tables(pos):
    half = HEAD_DIM // 2
    inv = ROPE_THETA ** (-jnp.arange(half, dtype=F32) / half)
    ang = pos.astype(F32)[:, None] * inv[None, :]
    cos = jnp.cos(ang)
    sin = jnp.sin(ang)
    reps = LANES // HEAD_DIM
    return (jnp.tile(cos, (1, 2 * reps)), jnp.tile(jnp.concatenate([-sin, sin], axis=1), (1, reps)))


def _conv_state(sg, sv, tiles_per_seq):
    keep = slice(SUBLANES - (CONV_W - 1), SUBLANES)
    if tiles_per_seq is None:
        return jnp.concatenate([sg[:, keep], sv[:, keep]], axis=-1)
    last = slice(tiles_per_seq - 1, None, tiles_per_seq)

    def rows(s):
        s = s[last, :, keep, :]
        return jnp.swapaxes(s, 1, 2).reshape(s.shape[0], CONV_W - 1, -1)

    return jnp.concatenate([rows(sg), rows(sv)], axis=-1)


def _trunk(x, pos, seq, tm, wide_tm, cache, state, state_conv, p):
    m = x.shape[0]
    nseq = m // seq
    sample = cache is not None
    row = lambda v: v.reshape(1, -1)

    cos, sin = _rope_tables(pos)
    q, kv = _qkv_rope(x, row(p["norm_mix_pre"][0]), p["w_qkv"], row(p["attn_b_qkv"][0]), cos, sin, tm,
                      F32 if sample else BF16)
    if sample:
        ck, cv = cache
        o, k_new, v_new = _attn_sample(p["attn_sinks"][0], q, kv, ck.reshape(nseq, WINDOW, KV_W),
                                       cv.reshape(nseq, WINDOW, KV_W), seq)
    else:
        o, p["w_up0"], p["w_down0"], w_in_t, w_out = _attn_prompt(
            p["attn_sinks"][0], q, kv, nseq, seq,
            [(p["ffn_w_up"], 0), (p["ffn_w_down"], 0), (p["mlstm_w_in_t"], 0), (p["mlstm_w_out"], 0)])
        p["w_in_t"], p["w_out"] = w_in_t[0], w_out[0]
        kv3 = kv.reshape(nseq, seq, 2 * KV_W)
        k_new = kv3[:, seq - WINDOW:, :KV_W]
        v_new = kv3[:, seq - WINDOW:, KV_W:]
    k_new = k_new.reshape(1, nseq, WINDOW, N_KV_HEADS, HEAD_DIM)
    v_new = v_new.reshape(1, nseq, WINDOW, N_KV_HEADS, HEAD_DIM)
    x, xn = _proj_post(o, p["w_o"], row(p["attn_b_o"][0]), x, row(p["norm_mix_post"][0]),
                       row(p["norm_ffn_pre"][0]), tm)
    if sample:
        x, sg0, sv0 = _ffn(x, xn, state_conv[0], p["w_up0"], p["ffn_conv_w"][0], p["ffn_conv_b"][0],
                           p["w_down0"], row(p["norm_ffn_post"][0]), tm, FF_TILE, seq)
    else:
        x, sg0, sv0, p["w_up1"], p["w_down1"] = _ffn(
            x, xn, None, p["w_up0"], p["ffn_conv_w"][0], p["ffn_conv_b"][0], p["w_down0"],
            row(p["norm_ffn_post"][0]), tm, FF_TILE, seq, cast_next=(p["ffn_w_up"], p["ffn_w_down"], 1))

    g1 = row(p["norm_mix_pre"][1])
    ng = row(p["mlstm_norm"][0])
    qv, og, keys, gates = _mlstm_proj(x, g1, p["w_in_t"], p["w_gates_t"], p["b_gates"], wide_tm, not sample,
                                      F32 if sample else BF16)
    if sample:
        c0, n0, m0 = state
        h, c_new, n_new, m_new = _mlstm_sample(qv, og, keys, gates, ng, c0, n0, m0, seq)
    else:
        h, c_aug, m_new = _mlstm_prompt(qv, og, keys, gates, ng, nseq, seq)
        h = h.reshape(m, V_W)
        c_new = c_aug[:, :, :M_V_DIM]
        n_new = c_aug[:, :, M_V_DIM]
        m_new = m_new[:, 0]
    c_new = c_new.reshape(1, nseq, M_HEADS, M_QK_DIM, M_V_DIM)
    n_new = n_new.reshape(1, nseq, M_HEADS, M_QK_DIM)
    m_new = m_new.reshape(1, nseq, M_HEADS)
    x, xn = _proj_post(h, p["w_out"], None, x, row(p["norm_mix_post"][1]), row(p["norm_ffn_pre"][1]), tm)
    x, sg1, sv1 = _ffn(x, xn, state_conv[1] if sample else None, p["w_up1"], p["ffn_conv_w"][1],
                       p["ffn_conv_b"][1], p["w_down1"], row(p["norm_ffn_post"][1]), tm, FF_TILE, seq)
    tps = None if sample else seq // tm
    conv = jnp.stack([_conv_state(sg0, sv0, tps), _conv_state(sg1, sv1, tps)])
    return x, k_new, v_new, c_new, n_new, m_new, conv


def kernel(x_prompt, x_sample, cache_k, cache_v, state_C, state_n, state_m, state_conv, norm_mix_pre,
           norm_mix_post, norm_ffn_pre, norm_ffn_post, attn_w_qkv, attn_b_qkv, attn_w_o, attn_b_o, attn_sinks,
           mlstm_w_in, mlstm_b_gates, mlstm_norm, mlstm_w_out, ffn_w_up, ffn_conv_w, ffn_conv_b, ffn_w_down):
    batch, seq, _ = x_prompt.shape
    dec_batch, dec_seq, _ = x_sample.shape
    n_gates = 2 * M_HEADS
    w_in_t = jnp.swapaxes(mlstm_w_in, 1, 2)
    g0 = 2 * QK_W + 2 * V_W
    p = dict(
        norm_mix_pre=norm_mix_pre, norm_mix_post=norm_mix_post, norm_ffn_pre=norm_ffn_pre,
        norm_ffn_post=norm_ffn_post, attn_b_qkv=attn_b_qkv, attn_b_o=attn_b_o, attn_sinks=attn_sinks,
        mlstm_norm=mlstm_norm, ffn_conv_w=ffn_conv_w, ffn_conv_b=ffn_conv_b,
        w_qkv=attn_w_qkv[0].astype(BF16),
        w_o=attn_w_o[0].astype(BF16),
        mlstm_w_in_t=w_in_t, mlstm_w_out=mlstm_w_out,
        w_gates_t=jnp.pad(w_in_t[0, g0:].astype(BF16), ((0, GATE_PAD - n_gates), (0, 0))),
        b_gates=jnp.pad(mlstm_b_gates[0], (0, GATE_PAD - n_gates)).reshape(1, GATE_PAD),
        ffn_w_up=ffn_w_up, ffn_w_down=ffn_w_down,
    )
    yp, k_p, v_p, c_p, n_p, m_p, conv_p = _trunk(
        x_prompt.reshape(batch * seq, D_MODEL), jnp.arange(seq), seq, ROW_TILE, WIDE_ROW_TILE, None, None, None, p)
    ys, k_s, v_s, c_s, n_s, m_s, conv_s = _trunk(
        x_sample.reshape(dec_batch * dec_seq, D_MODEL),
        jnp.tile(PAST_LEN + jnp.arange(dec_seq), dec_batch), dec_seq, dec_batch * dec_seq, dec_batch * dec_seq,
        (cache_k[0], cache_v[0]), (state_C[0], state_n[0], state_m[0]), state_conv, p)
    return (yp.reshape(batch, seq, D_MODEL), ys.reshape(dec_batch, dec_seq, D_MODEL),
            k_p, v_p, k_s, v_s, c_p, n_p, m_p, c_s, n_s, m_s, conv_p, conv_s)
```

```python
import functools

import jax
import jax.numpy as jnp
from jax import lax
from jax.experimental import pallas as pl
from jax.experimental.pallas import tpu as pltpu

F32 = jnp.float32
BF16 = jnp.bfloat16

D_MODEL = 2048
WINDOW = 128
HEAD_DIM = 64
N_HEADS = 32
N_KV_HEADS = 4
GROUP = N_HEADS // N_KV_HEADS
Q_W = N_HEADS * HEAD_DIM
KV_W = N_KV_HEADS * HEAD_DIM
ROPE_THETA = 10000.0
M_HEADS = 4
M_QK_DIM = 256
M_V_DIM = 512
QK_W = M_HEADS * M_QK_DIM
V_W = M_HEADS * M_V_DIM
GATE_PAD = 128
PROJ_W = QK_W + 2 * V_W
D_FF = 4 * D_MODEL
CONV_W = 3
EPS = 1e-6
NEG_INF = -1e30
PAST_LEN = 16384

SUBLANES = 8
LANES = 128
VMEM_LIMIT_BYTES = 60 * 1024 * 1024

ROW_TILE = 512
WIDE_ROW_TILE = 1024
FF_TILE = 1024
PROJ_N_TILE = 1024
MLSTM_CHUNK = 256
MLSTM_SAMPLE_SEQS = 4


def _params(*sem):
    return pltpu.CompilerParams(dimension_semantics=sem, vmem_limit_bytes=VMEM_LIMIT_BYTES)


def _rmsnorm(xf, g):
    r = xf * lax.rsqrt(jnp.mean(xf * xf, axis=-1, keepdims=True) + EPS)
    return r * g


def _qkv_rope_kernel(x_ref, g_ref, w_ref, b_ref, cos_ref, sin_ref, q_ref, kv_ref):
    xn = _rmsnorm(x_ref[...], g_ref[...]).astype(BF16)
    y = jnp.dot(xn, w_ref[...], preferred_element_type=F32) + b_ref[...]
    cos = cos_ref[...]
    sin = sin_ref[...]
    lane = lax.broadcasted_iota(jnp.int32, cos.shape, 1)
    first_half = (lane & (HEAD_DIM - 1)) < (HEAD_DIM // 2)
    n_rot = (Q_W + KV_W) // LANES
    for c in range(n_rot):
        blk = y[:, c * LANES:(c + 1) * LANES]
        sw = jnp.where(first_half, pltpu.roll(blk, LANES - HEAD_DIM // 2, 1),
                       pltpu.roll(blk, HEAD_DIM // 2, 1))
        r = blk * cos + sw * sin
        if c < Q_W // LANES:
            q_ref[:, c * LANES:(c + 1) * LANES] = (r * (HEAD_DIM ** -0.5)).astype(q_ref.dtype)
        else:
            o = c * LANES - Q_W
            kv_ref[:, o:o + LANES] = r
    kv_ref[:, KV_W:2 * KV_W] = y[:, Q_W + KV_W:Q_W + 2 * KV_W]


def _qkv_rope(x, g, w, b, cos, sin, tm, q_dtype):
    m = x.shape[0]
    n = w.shape[1]
    n_pos_tiles = cos.shape[0] // tm
    return pl.pallas_call(
        _qkv_rope_kernel,
        grid=(m // tm,),
        in_specs=[
            pl.BlockSpec((tm, D_MODEL), lambda i: (i, 0)),
            pl.BlockSpec((1, D_MODEL), lambda i: (0, 0)),
            pl.BlockSpec((D_MODEL, n), lambda i: (0, 0)),
            pl.BlockSpec((1, n), lambda i: (0, 0)),
            pl.BlockSpec((tm, LANES), lambda i: (i % n_pos_tiles, 0)),
            pl.BlockSpec((tm, LANES), lambda i: (i % n_pos_tiles, 0)),
        ],
        out_specs=[
            pl.BlockSpec((tm, Q_W), lambda i: (i, 0)),
            pl.BlockSpec((tm, 2 * KV_W), lambda i: (i, 0)),
        ],
        out_shape=[
            jax.ShapeDtypeStruct((m, Q_W), q_dtype),
            jax.ShapeDtypeStruct((m, 2 * KV_W), F32),
        ],
        compiler_params=_params("arbitrary"),
        name="qkv_rope",
    )(x, g, w, b, cos, sin)


def _attn_prompt_kernel(n_cast, sinks_ref, q_ref, kvp_ref, kvc_ref, *rest):
    o_ref = rest[n_cast]
    for src_ref, dst_ref in zip(rest[:n_cast], rest[n_cast + 1:]):
        dst_ref[...] = src_ref[...].astype(BF16)
    n = pl.program_id(1)
    w = WINDOW
    ri = lax.broadcasted_iota(jnp.int32, (w, 2 * w), 0)
    ci = lax.broadcasted_iota(jnp.int32, (w, 2 * w), 1)
    has_prev = jnp.full((w, 2 * w), n, jnp.int32) > 0
    allowed = ((ci < w) & (ci > ri) & has_prev) | ((ci >= w) & ((ci - w) <= ri))
    low = lax.broadcasted_iota(jnp.int32, (w, LANES), 1) < HEAD_DIM
    ones = jnp.ones((2 * w, HEAD_DIM), BF16)
    for g in range(N_KV_HEADS):
        ks = slice(g * HEAD_DIM, (g + 1) * HEAD_DIM)
        vs = slice(KV_W + g * HEAD_DIM, KV_W + (g + 1) * HEAD_DIM)
        k = jnp.concatenate([kvp_ref[:, ks], kvc_ref[:, ks]], axis=0).astype(BF16)
        v = jnp.concatenate([kvp_ref[:, vs], kvc_ref[:, vs]], axis=0).astype(BF16)
        v1 = jnp.concatenate([v, ones], axis=1)
        q = jnp.concatenate(
            [q_ref[:, (g * GROUP + h) * HEAD_DIM:(g * GROUP + h + 1) * HEAD_DIM] for h in range(GROUP)],
            axis=0)
        s = lax.dot_general(q, k, (((1,), (1,)), ((), ())), preferred_element_type=F32)
        ps, es = [], []
        for h in range(GROUP):
            sh = jnp.where(allowed, s[h * w:(h + 1) * w], NEG_INF)
            sink = sinks_ref[g * GROUP + h]
            m = jnp.maximum(jnp.max(sh, axis=-1, keepdims=True), sink)
            ps.append(jnp.exp(sh - m).astype(BF16))
            es.append(jnp.exp(sink - m))
        oa = jnp.dot(jnp.concatenate(ps, axis=0), v1, preferred_element_type=F32)
        for h in range(0, GROUP, 2):
            a = oa[h * w:(h + 1) * w]
            b = oa[(h + 1) * w:(h + 2) * w]
            num = jnp.where(low, a, pltpu.roll(b, HEAD_DIM, 1))
            den = jnp.where(low, pltpu.roll(a, HEAD_DIM, 1) + es[h], b + es[h + 1])
            c0 = (g * GROUP + h) * HEAD_DIM
            o_ref[:, c0:c0 + LANES] = (num / den).astype(o_ref.dtype)


def _attn_prompt(sinks, q, kv, batch, seq, casts):
    nb = seq // WINDOW
    steps = batch * nb
    cast_in, cast_out, cast_shape = [], [], []
    for w, layer in casts:
        rows = w.shape[1] // steps
        cast_in.append(pl.BlockSpec((1, rows, w.shape[2]), lambda b, n, layer=layer: (layer, b * nb + n, 0)))
        cast_out.append(pl.BlockSpec((1, rows, w.shape[2]), lambda b, n: (0, b * nb + n, 0)))
        cast_shape.append(jax.ShapeDtypeStruct((1, rows * steps, w.shape[2]), BF16))
    return pl.pallas_call(
        functools.partial(_attn_prompt_kernel, len(casts)),
        grid=(batch, nb),
        in_specs=[
            pl.BlockSpec(memory_space=pltpu.SMEM),
            pl.BlockSpec((WINDOW, Q_W), lambda b, n: (b * nb + n, 0)),
            pl.BlockSpec((WINDOW, 2 * KV_W), lambda b, n: (b * nb + jnp.maximum(n - 1, 0), 0)),
            pl.BlockSpec((WINDOW, 2 * KV_W), lambda b, n: (b * nb + n, 0)),
        ] + cast_in,
        out_specs=[pl.BlockSpec((WINDOW, Q_W), lambda b, n: (b * nb + n, 0))] + cast_out,
        out_shape=[jax.ShapeDtypeStruct((batch * seq, Q_W), BF16)] + cast_shape,
        compiler_params=_params("arbitrary", "arbitrary"),
        name="attn_prompt",
    )(sinks, q, kv, kv, *[w for w, _ in casts])


def _attn_sample_kernel(sinks_ref, q_ref, kvn_ref, ck_ref, cv_ref, o_ref, ko_ref, vo_ref):
    t = q_ref.shape[0]
    wc = ck_ref.shape[1]
    rows = N_HEADS * t
    ri = lax.broadcasted_iota(jnp.int32, (rows, wc), 0) & (t - 1)
    ci = lax.broadcasted_iota(jnp.int32, (rows, wc), 1)
    allowed_c = ci > ri
    ri_n = lax.broadcasted_iota(jnp.int32, (rows, t), 0) & (t - 1)
    ci_n = lax.broadcasted_iota(jnp.int32, (rows, t), 1)
    allowed_n = ci_n <= ri_n
    nt = (((1,), (1,)), ((), ()))
    s_c, s_n = [], []
    for g in range(N_KV_HEADS):
        ks = slice(g * HEAD_DIM, (g + 1) * HEAD_DIM)
        q = jnp.concatenate(
            [q_ref[:, (g * GROUP + h) * HEAD_DIM:(g * GROUP + h + 1) * HEAD_DIM] for h in range(GROUP)],
            axis=0)
        s_c.append(lax.dot_general(q.astype(BF16), ck_ref[0, :, ks].astype(BF16), nt, preferred_element_type=F32))
        s_n.append(lax.dot_general(q, kvn_ref[:, ks], nt, preferred_element_type=F32))
    s_c = jnp.where(allowed_c, jnp.concatenate(s_c, axis=0), NEG_INF)
    s_n = jnp.where(allowed_n, jnp.concatenate(s_n, axis=0), NEG_INF)
    sink = jnp.concatenate([jnp.full((t, 1), sinks_ref[h], F32) for h in range(N_HEADS)], axis=0)
    m = jnp.maximum(jnp.maximum(jnp.max(s_c, axis=-1, keepdims=True),
                                jnp.max(s_n, axis=-1, keepdims=True)), sink)
    p_c = jnp.exp(s_c - m)
    p_n = jnp.exp(s_n - m)
    inv_l = 1.0 / (jnp.sum(p_c, axis=-1, keepdims=True) + jnp.sum(p_n, axis=-1, keepdims=True)
                   + jnp.exp(sink - m))
    p_c = (p_c * inv_l).astype(BF16)
    p_n = p_n * inv_l
    gr = GROUP * t
    for g in range(N_KV_HEADS):
        ks = slice(g * HEAD_DIM, (g + 1) * HEAD_DIM)
        vs = slice(KV_W + g * HEAD_DIM, KV_W + (g + 1) * HEAD_DIM)
        o = (jnp.dot(p_c[g * gr:(g + 1) * gr], cv_ref[0, :, ks].astype(BF16), preferred_element_type=F32)
             + jnp.dot(p_n[g * gr:(g + 1) * gr], kvn_ref[:, vs], preferred_element_type=F32))
        for h in range(GROUP):
            c0 = (g * GROUP + h) * HEAD_DIM
            o_ref[:, c0:c0 + HEAD_DIM] = o[h * t:(h + 1) * t]
    ko_ref[0, 0:wc - t, :] = ck_ref[0, t:wc, :]
    ko_ref[0, wc - t:wc, :] = kvn_ref[:, 0:KV_W]
    vo_ref[0, 0:wc - t, :] = cv_ref[0, t:wc, :]
    vo_ref[0, wc - t:wc, :] = kvn_ref[:, KV_W:2 * KV_W]


def _attn_sample(sinks, q, kvn, ck, cv, t):
    db = ck.shape[0]
    wc = ck.shape[1]
    return pl.pallas_call(
        _attn_sample_kernel,
        grid=(db,),
        in_specs=[
            pl.BlockSpec(memory_space=pltpu.SMEM),
            pl.BlockSpec((t, Q_W), lambda b: (b, 0)),
            pl.BlockSpec((t, 2 * KV_W), lambda b: (b, 0)),
            pl.BlockSpec((1, wc, KV_W), lambda b: (b, 0, 0)),
            pl.BlockSpec((1, wc, KV_W), lambda b: (b, 0, 0)),
        ],
        out_specs=[
            pl.BlockSpec((t, Q_W), lambda b: (b, 0)),
            pl.BlockSpec((1, wc, KV_W), lambda b: (b, 0, 0)),
            pl.BlockSpec((1, wc, KV_W), lambda b: (b, 0, 0)),
        ],
        out_shape=[
            jax.ShapeDtypeStruct((db * t, Q_W), F32),
            jax.ShapeDtypeStruct((db, wc, KV_W), F32),
            jax.ShapeDtypeStruct((db, wc, KV_W), F32),
        ],
        compiler_params=_params("arbitrary"),
        name="attn_sample",
    )(sinks, q, kvn, ck, cv)


def _proj_post_kernel(has_bias, a_ref, w_ref, b_ref, x_ref, g_ref, gn_ref, o_ref, xn_ref):
    y = jnp.dot(a_ref[...].astype(BF16), w_ref[...], preferred_element_type=F32)
    if has_bias:
        y = y + b_ref[...]
    x_new = x_ref[...] + _rmsnorm(y, g_ref[...])
    o_ref[...] = x_new
    xn_ref[...] = _rmsnorm(x_new, gn_ref[...]).astype(xn_ref.dtype)


def _proj_post(a, w, b, x, g, g_next, tm):
    m, k = a.shape
    has_bias = b is not None
    if b is None:
        b = jnp.zeros((1, D_MODEL), F32)
    return pl.pallas_call(
        functools.partial(_proj_post_kernel, has_bias),
        grid=(m // tm,),
        in_specs=[
            pl.BlockSpec((tm, k), lambda i: (i, 0)),
            pl.BlockSpec((k, D_MODEL), lambda i: (0, 0)),
            pl.BlockSpec((1, D_MODEL), lambda i: (0, 0)),
            pl.BlockSpec((tm, D_MODEL), lambda i: (i, 0)),
            pl.BlockSpec((1, D_MODEL), lambda i: (0, 0)),
            pl.BlockSpec((1, D_MODEL), lambda i: (0, 0)),
        ],
        out_specs=[pl.BlockSpec((tm, D_MODEL), lambda i: (i, 0)), pl.BlockSpec((tm, D_MODEL), lambda i: (i, 0))],
        out_shape=[jax.ShapeDtypeStruct((m, D_MODEL), F32), jax.ShapeDtypeStruct((m, D_MODEL), BF16)],
        compiler_params=_params("arbitrary"),
        name="proj_post",
    )(a, w, b, x, g, g_next)


_NT = (((1,), (1,)), ((), ()))


def _mlstm_proj_kernel(keys_transposed, x_ref, g_ref, wt_ref, wk_ref, wg_ref, bg_ref, o_ref, k_ref, gates_ref, xn_ref):
    @pl.when(pl.program_id(1) == 0)
    def _():
        xn = _rmsnorm(x_ref[...], g_ref[...]).astype(BF16)
        xn_ref[...] = xn
        if keys_transposed:
            k_ref[...] = lax.dot_general(wk_ref[...], xn, _NT, preferred_element_type=F32)
        else:
            k_ref[...] = lax.dot_general(xn, wk_ref[...], _NT, preferred_element_type=F32)
        gates_ref[...] = lax.dot_general(xn, wg_ref[...], _NT, preferred_element_type=F32) + bg_ref[...]

    o_ref[...] = lax.dot_general(xn_ref[...], wt_ref[...], _NT, preferred_element_type=F32)


def _mlstm_proj(x, g, w_t, wg_t, bg, tm, keys_transposed):
    m = x.shape[0]
    tn = PROJ_N_TILE
    assert QK_W == tn
    k_spec = pl.BlockSpec((QK_W, tm), lambda i, j: (0, i), pipeline_mode=pl.Buffered(1)) if keys_transposed else \
        pl.BlockSpec((tm, QK_W), lambda i, j: (i, 0), pipeline_mode=pl.Buffered(1))
    return pl.pallas_call(
        functools.partial(_mlstm_proj_kernel, keys_transposed),
        grid=(m // tm, PROJ_W // tn),
        in_specs=[
            pl.BlockSpec((tm, D_MODEL), lambda i, j: (i, 0)),
            pl.BlockSpec((1, D_MODEL), lambda i, j: (0, 0)),
            pl.BlockSpec((tn, D_MODEL), lambda i, j: (j + jnp.minimum(j, 1), 0)),
            pl.BlockSpec((QK_W, D_MODEL), lambda i, j: (1, 0), pipeline_mode=pl.Buffered(1)),
            pl.BlockSpec((GATE_PAD, D_MODEL), lambda i, j: (0, 0)),
            pl.BlockSpec((1, GATE_PAD), lambda i, j: (0, 0)),
        ],
        out_specs=[
            pl.BlockSpec((tm, tn), lambda i, j: (i, j)),
            k_spec,
            pl.BlockSpec((tm, GATE_PAD), lambda i, j: (i, 0)),
        ],
        out_shape=[
            jax.ShapeDtypeStruct((m, PROJ_W), F32),
            jax.ShapeDtypeStruct((QK_W, m) if keys_transposed else (m, QK_W), F32),
            jax.ShapeDtypeStruct((m, GATE_PAD), F32),
        ],
        scratch_shapes=[pltpu.VMEM((tm, D_MODEL), BF16)],
        compiler_params=_params("arbitrary", "arbitrary"),
        name="mlstm_proj",
    )(x, g, w_t, w_t, wg_t, bg)


def _log_sigmoid(x):
    return jnp.minimum(x, 0.0) - jnp.log1p(jnp.exp(-jnp.abs(x)))


def _chunk_gates(li, lf, m_state):
    L = li.shape[0]
    ri = lax.broadcasted_iota(jnp.int32, (L, L), 0)
    ci = lax.broadcasted_iota(jnp.int32, (L, L), 1)
    eye = ri == ci
    tril = ci <= ri
    lf_row = jnp.sum(jnp.where(eye, lf, 0.0), axis=0, keepdims=True)
    li_row = jnp.sum(jnp.where(eye, li, 0.0), axis=0, keepdims=True)
    b_col = jnp.sum(jnp.where(tril, lf_row, 0.0), axis=1, keepdims=True)
    b_row = jnp.sum(jnp.where(ri <= ci, lf, 0.0), axis=0, keepdims=True)
    dmat = jnp.where(tril, b_col - b_row + li_row, -jnp.inf)
    inter = b_col + m_state
    m_row = jnp.maximum(inter, jnp.max(dmat, axis=1, keepdims=True))
    b_last = b_col[L - 1:L, :]
    gk_col = b_last - b_col + li
    gk_row = b_last - b_row + li_row
    m_new = jnp.maximum(b_last + m_state, jnp.max(gk_col, axis=0, keepdims=True))
    decay = jnp.exp(b_last + m_state - m_new)
    return jnp.exp(dmat - m_row), jnp.exp(inter - m_row), m_row, gk_col, gk_row, m_new, decay


def _mlstm_chunk(q, k, v, li, lf, c_state, n_state, m_state):
    dexp, w_inter, m_row, gk_col, _, m_new, decay = _chunk_gates(li, lf, m_state)
    s = lax.dot_general(q, k, (((1,), (1,)), ((), ())), preferred_element_type=F32)
    sm = s * dexp
    num = (jnp.dot(sm, v, preferred_element_type=F32)
           + w_inter * jnp.dot(q, c_state, preferred_element_type=F32))
    den = jnp.sum(sm, axis=1, keepdims=True) + w_inter * jnp.sum(q * n_state, axis=1, keepdims=True)
    h = num * (1.0 / jnp.maximum(jnp.abs(den), jnp.exp(-m_row)))
    kw = k * jnp.exp(gk_col - m_new)
    c_new = decay * c_state + lax.dot_general(kw, v, (((0,), (0,)), ((), ())), preferred_element_type=F32)
    n_new = decay * n_state + jnp.sum(kw, axis=0, keepdims=True)
    return h, c_new, n_new, m_new


def _mlstm_chunk_t(q, kt, v1, li, lf, c_aug, m_state):
    dk = q.shape[1]
    dv = v1.shape[1] - LANES
    scale = dk ** -0.5
    dexp, w_inter, m_row, _, gk_row, m_new, decay = _chunk_gates(li, lf, m_state)
    qm = q.astype(BF16)
    s = jnp.dot(qm, kt.astype(BF16), preferred_element_type=F32)
    sm = (s * scale) * dexp
    num = (jnp.dot(sm.astype(BF16), v1, preferred_element_type=F32)
           + (w_inter * scale) * jnp.dot(qm, c_aug.astype(BF16), preferred_element_type=F32))
    den = num[:, dv:dv + 1]
    h = num[:, :dv] * (1.0 / jnp.maximum(jnp.abs(den), jnp.exp(-m_row)))
    kwt = (kt * jnp.exp(gk_row - m_new)).astype(BF16)
    c_new = decay * c_aug + jnp.dot(kwt, v1, preferred_element_type=F32)
    return h, c_new, m_new


def _head_cols(hd):
    q0 = hd * M_QK_DIM
    v0 = QK_W + hd * M_V_DIM
    o0 = QK_W + V_W + hd * M_V_DIM
    return (q0, q0 + M_QK_DIM), (v0, v0 + M_V_DIM), (o0, o0 + M_V_DIM), hd, M_HEADS + hd


def _head_out(h, o, ng):
    return _rmsnorm(h, ng) * (0.5 * (jnp.tanh(0.5 * o) + 1.0))


def _mlstm_prompt_kernel(batch, proj_ref, gates_ref, *rest):
    kt_refs = rest[:batch]
    ng_ref, h_ref, c_ref, m_ref = rest[batch:]

    @pl.when(pl.program_id(0) == 0)
    def _():
        c_ref[...] = jnp.zeros(c_ref.shape, F32)
        m_ref[...] = jnp.zeros(m_ref.shape, F32)

    L = proj_ref.shape[1]
    one_col = (lax.broadcasted_iota(jnp.int32, (L, LANES), 1) == 0).astype(BF16)
    for b in range(batch):
        for hd in range(M_HEADS):
            r = b * M_HEADS + hd
            (q0, q1), (v0, v1), (o0, o1), gi, gf = _head_cols(hd)
            va = jnp.concatenate([proj_ref[b, :, v0:v1].astype(BF16), one_col], axis=1)
            h, c_new, m_new = _mlstm_chunk_t(
                proj_ref[b, :, q0:q1], kt_refs[b][hd * M_QK_DIM:(hd + 1) * M_QK_DIM, :], va,
                gates_ref[b, :, gi:gi + 1], _log_sigmoid(gates_ref[b, :, gf:gf + 1]),
                c_ref[r], m_ref[r:r + 1, 0:1])
            hn = _head_out(h, proj_ref[b, :, o0:o1], ng_ref[:, hd * M_V_DIM:(hd + 1) * M_V_DIM])
            h_ref[b, :, hd * M_V_DIM:(hd + 1) * M_V_DIM] = hn.astype(h_ref.dtype)
            c_ref[r] = c_new
            m_ref[r:r + 1, :] = jnp.broadcast_to(m_new, (1, LANES))


def _mlstm_prompt(proj, kt, gates, ng, batch, seq):
    chunk = MLSTM_CHUNK
    nc = seq // chunk
    rows = batch * M_HEADS
    return pl.pallas_call(
        functools.partial(_mlstm_prompt_kernel, batch),
        grid=(nc,),
        in_specs=[pl.BlockSpec((batch, chunk, PROJ_W), lambda c: (0, c, 0)),
                  pl.BlockSpec((batch, chunk, GATE_PAD), lambda c: (0, c, 0))]
        + [pl.BlockSpec((QK_W, chunk), lambda c, b=b: (0, b * nc + c)) for b in range(batch)]
        + [pl.BlockSpec((1, V_W), lambda c: (0, 0))],
        out_specs=[
            pl.BlockSpec((batch, chunk, V_W), lambda c: (0, c, 0)),
            pl.BlockSpec((rows, M_QK_DIM, M_V_DIM + LANES), lambda c: (0, 0, 0)),
            pl.BlockSpec((rows, LANES), lambda c: (0, 0)),
        ],
        out_shape=[
            jax.ShapeDtypeStruct((batch, seq, V_W), BF16),
            jax.ShapeDtypeStruct((rows, M_QK_DIM, M_V_DIM + LANES), F32),
            jax.ShapeDtypeStruct((rows, LANES), F32),
        ],
        compiler_params=_params("arbitrary"),
        name="mlstm_prompt",
    )(proj.reshape(batch, seq, PROJ_W), gates.reshape(batch, seq, GATE_PAD), *([kt] * batch), ng)


def _mlstm_sample_kernel(t, proj_ref, k_ref, gates_ref, ng_ref, c0_ref, n0_ref, m0_ref, h_ref, c_ref, n_ref, m_ref):
    for s in range(c0_ref.shape[0]):
        rows = slice(s * t, (s + 1) * t)
        for hd in range(M_HEADS):
            (q0, q1), (v0, v1), (o0, o1), gi, gf = _head_cols(hd)
            h, c_new, n_new, m_new = _mlstm_chunk(
                proj_ref[rows, q0:q1] * (M_QK_DIM ** -0.5), k_ref[rows, hd * M_QK_DIM:(hd + 1) * M_QK_DIM],
                proj_ref[rows, v0:v1], gates_ref[rows, gi:gi + 1], _log_sigmoid(gates_ref[rows, gf:gf + 1]),
                c0_ref[s, hd], n0_ref[s, hd:hd + 1, :], m0_ref[s, :, hd:hd + 1])
            h_ref[rows, hd * M_V_DIM:(hd + 1) * M_V_DIM] = _head_out(
                h, proj_ref[rows, o0:o1], ng_ref[:, hd * M_V_DIM:(hd + 1) * M_V_DIM])
            c_ref[s, hd] = c_new
            n_ref[s, hd:hd + 1, :] = n_new
            m_ref[s, :, hd:hd + 1] = m_new


def _mlstm_sample(proj, k, gates, ng, c0, n0, m0, t):
    db = c0.shape[0]
    sb = MLSTM_SAMPLE_SEQS
    m0 = m0.reshape(db, 1, M_HEADS)
    state_specs = [
        pl.BlockSpec((sb, M_HEADS, M_QK_DIM, M_V_DIM), lambda b: (b, 0, 0, 0)),
        pl.BlockSpec((sb, M_HEADS, M_QK_DIM), lambda b: (b, 0, 0)),
        pl.BlockSpec((sb, 1, M_HEADS), lambda b: (b, 0, 0)),
    ]
    return pl.pallas_call(
        functools.partial(_mlstm_sample_kernel, t),
        grid=(db // sb,),
        in_specs=[
            pl.BlockSpec((sb * t, PROJ_W), lambda b: (b, 0)),
            pl.BlockSpec((sb * t, QK_W), lambda b: (b, 0)),
            pl.BlockSpec((sb * t, GATE_PAD), lambda b: (b, 0)),
            pl.BlockSpec((1, V_W), lambda b: (0, 0)),
        ] + state_specs,
        out_specs=[pl.BlockSpec((sb * t, V_W), lambda b: (b, 0))] + state_specs,
        out_shape=[
            jax.ShapeDtypeStruct((db * t, V_W), F32),
            jax.ShapeDtypeStruct(c0.shape, F32),
            jax.ShapeDtypeStruct(n0.shape, F32),
            jax.ShapeDtypeStruct(m0.shape, F32),
        ],
        compiler_params=_params("arbitrary"),
        name="mlstm_sample",
    )(proj, k, gates, ng, c0, n0, m0)


def _conv3(u, u1, u2, w, b):
    return ((b + w[0:1] * u2) + w[1:2] * u1) + w[2:3] * u


def _conv_stream(u, tail, w, b):
    u1 = pltpu.roll(u, 1, 0)
    u2 = pltpu.roll(u, 2, 0)
    c = _conv3(u, u1, u2, w, b)
    uf = u[0:SUBLANES]
    row = lax.broadcasted_iota(jnp.int32, uf.shape, 0)
    uf1 = jnp.where(row < 1, pltpu.roll(tail, 1, 0), pltpu.roll(uf, 1, 0))
    uf2 = jnp.where(row < 2, pltpu.roll(tail, 2, 0), pltpu.roll(uf, 2, 0))
    cf = _conv3(uf, uf1, uf2, w, b)
    return jnp.concatenate([cf, c[SUBLANES:]], axis=0)


def _conv_seq8(u, prev, w, b):
    nseq = u.shape[0] // SUBLANES
    p0 = jnp.broadcast_to(prev[:, 0:1, :], (nseq, SUBLANES, u.shape[1])).reshape(u.shape)
    p1 = jnp.broadcast_to(prev[:, 1:2, :], (nseq, SUBLANES, u.shape[1])).reshape(u.shape)
    row = lax.broadcasted_iota(jnp.int32, u.shape, 0) & (SUBLANES - 1)
    u1 = jnp.where(row == 0, p1, pltpu.roll(u, 1, 0))
    u2 = jnp.where(row == 0, p0, jnp.where(row == 1, p1, pltpu.roll(u, 2, 0)))
    return _conv3(u, u1, u2, w, b)


def _ffn_kernel(stream, hosts_cast, tiles_per_seq, x_ref, xn_ref, wg_ref, wv_ref, cw_ref, cb_ref, wd_ref, gpost_ref,
                *rest):
    if hosts_cast:
        nu_ref, nd_ref, o_ref, sg_ref, sv_ref, nu_bf_ref, nd_bf_ref, tail_g, tail_v = rest
        nu_bf_ref[...] = nu_ref[...].astype(BF16)
        nd_bf_ref[...] = nd_ref[...].astype(BF16)
    elif stream:
        o_ref, sg_ref, sv_ref, tail_g, tail_v = rest
    else:
        pg_ref, pv_ref, o_ref, sg_ref, sv_ref = rest
    i = pl.program_id(0)
    j = pl.program_id(1)
    nf = pl.num_programs(1)
    tm = x_ref.shape[0]

    if stream:
        @pl.when(i % tiles_per_seq == 0)
        def _():
            tail_g[j] = jnp.zeros(tail_g.shape[1:], F32)
            tail_v[j] = jnp.zeros(tail_v.shape[1:], F32)

    xn = xn_ref[...]
    ug = jnp.dot(xn, wg_ref[...], preferred_element_type=F32)
    uv = jnp.dot(xn, wv_ref[...], preferred_element_type=F32)
    cwg, cwv, cbg, cbv = cw_ref[j], cw_ref[j + nf], cb_ref[j], cb_ref[j + nf]
    if stream:
        cg = _conv_stream(ug, tail_g[j], cwg, cbg)
        cv = _conv_stream(uv, tail_v[j], cwv, cbv)
        tail_g[j] = ug[tm - SUBLANES:]
        tail_v[j] = uv[tm - SUBLANES:]
        sg_ref[0, j] = ug[tm - SUBLANES:]
        sv_ref[0, j] = uv[tm - SUBLANES:]
    else:
        cg = _conv_seq8(ug, pg_ref[...], cwg, cbg)
        cv = _conv_seq8(uv, pv_ref[...], cwv, cbv)
        sg_ref[...] = ug.reshape(sg_ref.shape)
        sv_ref[...] = uv.reshape(sv_ref.shape)
    h = (jax.nn.gelu(cg, approximate=True) * cv).astype(BF16)
    acc = jnp.where(j == 0, 0.0, o_ref[...])
    o_ref[...] = acc + jnp.dot(h, wd_ref[...], preferred_element_type=F32)

    @pl.when(j == nf - 1)
    def _():
        o_ref[...] = x_ref[...] + _rmsnorm(o_ref[...], gpost_ref[...])


def _ffn(x, xn, prev, w_up, conv_w, conv_b, w_down, gpost, tm, tf, seq, cast_next=None):
    m = x.shape[0]
    nf = D_FF // tf
    stream = prev is None
    conv_w = conv_w.reshape(CONV_W, 2 * nf, tf).transpose(1, 0, 2)
    conv_b = conv_b.reshape(2 * nf, 1, tf)
    in_specs = [
        pl.BlockSpec((tm, D_MODEL), lambda i, j: (i, 0)),
        pl.BlockSpec((tm, D_MODEL), lambda i, j: (i, 0)),
        pl.BlockSpec((None, D_MODEL, tf), lambda i, j: (0, 0, j)),
        pl.BlockSpec((None, D_MODEL, tf), lambda i, j: (0, 0, j + nf)),
        pl.BlockSpec((2 * nf, CONV_W, tf), lambda i, j: (0, 0, 0)),
        pl.BlockSpec((2 * nf, 1, tf), lambda i, j: (0, 0, 0)),
        pl.BlockSpec((None, tf, D_MODEL), lambda i, j: (0, j, 0)),
        pl.BlockSpec((1, D_MODEL), lambda i, j: (0, 0)),
    ]
    args = [x, xn, w_up, w_up, conv_w, conv_b, w_down, gpost]
    scratch = []
    if stream:
        tiles_per_seq = seq // tm
        state_shape = (m // tm, nf, SUBLANES, tf)
        state_spec = pl.BlockSpec((1, nf, SUBLANES, tf), lambda i, j: (i, 0, 0, 0))
        scratch += [pltpu.VMEM((nf, SUBLANES, tf), F32), pltpu.VMEM((nf, SUBLANES, tf), F32)]
    else:
        assert seq == SUBLANES and m == tm
        tiles_per_seq = 1
        nseq = m // seq
        state_shape = (nseq, SUBLANES, D_FF)
        state_spec = pl.BlockSpec((nseq, SUBLANES, tf), lambda i, j: (0, 0, j))
        in_specs += [
            pl.BlockSpec((nseq, CONV_W - 1, tf), lambda i, j: (0, 0, j)),
            pl.BlockSpec((nseq, CONV_W - 1, tf), lambda i, j: (0, 0, j + nf)),
        ]
        args += [prev, prev]
    out_specs = [pl.BlockSpec((tm, D_MODEL), lambda i, j: (i, 0)), state_spec, state_spec]
    out_shape = [
        jax.ShapeDtypeStruct((m, D_MODEL), F32),
        jax.ShapeDtypeStruct(state_shape, F32),
        jax.ShapeDtypeStruct(state_shape, F32),
    ]
    if cast_next is not None:
        assert stream
        nu, nd, layer_next = cast_next
        steps = (m // tm) * nf
        for w in (nu, nd):
            rows = w.shape[1] // steps
            in_specs.append(pl.BlockSpec((1, rows, w.shape[2]), lambda i, j: (layer_next, i * nf + j, 0)))
            out_specs.append(pl.BlockSpec((1, rows, w.shape[2]), lambda i, j: (0, i * nf + j, 0)))
            out_shape.append(jax.ShapeDtypeStruct((1,) + w.shape[1:], BF16))
            args.append(w)
    return pl.pallas_call(
        functools.partial(_ffn_kernel, stream, cast_next is not None, tiles_per_seq),
        grid=(m // tm, nf),
        in_specs=in_specs,
        out_specs=out_specs,
        out_shape=out_shape,
        scratch_shapes=scratch,
        compiler_params=_params("arbitrary", "arbitrary"),
        name="conv_ffn",
    )(*args)


def _rope_tables(pos):
    half = HEAD_DIM // 2
    inv = ROPE_THETA ** (-jnp.arange(half, dtype=F32) / half)
    ang = pos.astype(F32)[:, None] * inv[None, :]
    cos = jnp.cos(ang)
    sin = jnp.sin(ang)
    reps = LANES // HEAD_DIM
    return (jnp.tile(cos, (1, 2 * reps)), jnp.tile(jnp.concatenate([-sin, sin], axis=1), (1, reps)))


def _conv_state(sg, sv, tiles_per_seq):
    keep = slice(SUBLANES - (CONV_W - 1), SUBLANES)
    if tiles_per_seq is None:
        return jnp.concatenate([sg[:, keep], sv[:, keep]], axis=-1)
    last = slice(tiles_per_seq - 1, None, tiles_per_seq)

    def rows(s):
        s = s[last, :, keep, :]
        return jnp.swapaxes(s, 1, 2).reshape(s.shape[0], CONV_W - 1, -1)

    return jnp.concatenate([rows(sg), rows(sv)], axis=-1)


def _trunk(x, pos, seq, tm, wide_tm, cache, state, state_conv, p):
    m = x.shape[0]
    nseq = m // seq
    sample = cache is not None
    row = lambda v: v.reshape(1, -1)

    cos, sin = _rope_tables(pos)
    q, kv = _qkv_rope(x, row(p["norm_mix_pre"][0]), p["w_qkv"], row(p["attn_b_qkv"][0]), cos, sin, tm,
                      F32 if sample else BF16)
    if sample:
        ck, cv = cache
        o, k_new, v_new = _attn_sample(p["attn_sinks"][0], q, kv, ck.reshape(nseq, WINDOW, KV_W),
                                       cv.reshape(nseq, WINDOW, KV_W), seq)
    else:
        o, p["w_up0"], p["w_down0"], w_in_t, w_out = _attn_prompt(
            p["attn_sinks"][0], q, kv, nseq, seq,
            [(p["ffn_w_up"], 0), (p["ffn_w_down"], 0), (p["mlstm_w_in_t"], 0), (p["mlstm_w_out"], 0)])
        p["w_in_t"], p["w_out"] = w_in_t[0], w_out[0]
        kv3 = kv.reshape(nseq, seq, 2 * KV_W)
        k_new = kv3[:, seq - WINDOW:, :KV_W]
        v_new = kv3[:, seq - WINDOW:, KV_W:]
    k_new = k_new.reshape(1, nseq, WINDOW, N_KV_HEADS, HEAD_DIM)
    v_new = v_new.reshape(1, nseq, WINDOW, N_KV_HEADS, HEAD_DIM)
    x, xn = _proj_post(o, p["w_o"], row(p["attn_b_o"][0]), x, row(p["norm_mix_post"][0]),
                       row(p["norm_ffn_pre"][0]), tm)
    if sample:
        x, sg0, sv0 = _ffn(x, xn, state_conv[0], p["w_up0"], p["ffn_conv_w"][0], p["ffn_conv_b"][0],
                           p["w_down0"], row(p["norm_ffn_post"][0]), tm, FF_TILE, seq)
    else:
        x, sg0, sv0, p["w_up1"], p["w_down1"] = _ffn(
            x, xn, None, p["w_up0"], p["ffn_conv_w"][0], p["ffn_conv_b"][0], p["w_down0"],
            row(p["norm_ffn_post"][0]), tm, FF_TILE, seq, cast_next=(p["ffn_w_up"], p["ffn_w_down"], 1))

    g1 = row(p["norm_mix_pre"][1])
    ng = row(p["mlstm_norm"][0])
    proj, keys, gates = _mlstm_proj(x, g1, p["w_in_t"], p["w_gates_t"], p["b_gates"], wide_tm, not sample)
    if sample:
        c0, n0, m0 = state
        h, c_new, n_new, m_new = _mlstm_sample(proj, keys, gates, ng, c0, n0, m0, seq)
    else:
        h, c_aug, m_new = _mlstm_prompt(proj, keys, gates, ng, nseq, seq)
        h = h.reshape(m, V_W)
        c_new = c_aug[:, :, :M_V_DIM]
        n_new = c_aug[:, :, M_V_DIM]
        m_new = m_new[:, 0]
    c_new = c_new.reshape(1, nseq, M_HEADS, M_QK_DIM, M_V_DIM)
    n_new = n_new.reshape(1, nseq, M_HEADS, M_QK_DIM)
    m_new = m_new.reshape(1, nseq, M_HEADS)
    x, xn = _proj_post(h, p["w_out"], None, x, row(p["norm_mix_post"][1]), row(p["norm_ffn_pre"][1]), tm)
    x, sg1, sv1 = _ffn(x, xn, state_conv[1] if sample else None, p["w_up1"], p["ffn_conv_w"][1],
                       p["ffn_conv_b"][1], p["w_down1"], row(p["norm_ffn_post"][1]), tm, FF_TILE, seq)
    tps = None if sample else seq // tm
    conv = jnp.stack([_conv_state(sg0, sv0, tps), _conv_state(sg1, sv1, tps)])
    return x, k_new, v_new, c_new, n_new, m_new, conv


def kernel(x_prompt, x_sample, cache_k, cache_v, state_C, state_n, state_m, state_conv, norm_mix_pre,
           norm_mix_post, norm_ffn_pre, norm_ffn_post, attn_w_qkv, attn_b_qkv, attn_w_o, attn_b_o, attn_sinks,
           mlstm_w_in, mlstm_b_gates, mlstm_norm, mlstm_w_out, ffn_w_up, ffn_conv_w, ffn_conv_b, ffn_w_down):
    batch, seq, _ = x_prompt.shape
    dec_batch, dec_seq, _ = x_sample.shape
    n_gates = 2 * M_HEADS
    w_in_t = jnp.swapaxes(mlstm_w_in, 1, 2)
    g0 = 2 * QK_W + 2 * V_W
    p = dict(
        norm_mix_pre=norm_mix_pre, norm_mix_post=norm_mix_post, norm_ffn_pre=norm_ffn_pre,
        norm_ffn_post=norm_ffn_post, attn_b_qkv=attn_b_qkv, attn_b_o=attn_b_o, attn_sinks=attn_sinks,
        mlstm_norm=mlstm_norm, ffn_conv_w=ffn_conv_w, ffn_conv_b=ffn_conv_b,
        w_qkv=attn_w_qkv[0].astype(BF16),
        w_o=attn_w_o[0].astype(BF16),
        mlstm_w_in_t=w_in_t, mlstm_w_out=mlstm_w_out,
        w_gates_t=jnp.pad(w_in_t[0, g0:].astype(BF16), ((0, GATE_PAD - n_gates), (0, 0))),
        b_gates=jnp.pad(mlstm_b_gates[0], (0, GATE_PAD - n_gates)).reshape(1, GATE_PAD),
        ffn_w_up=ffn_w_up, ffn_w_down=ffn_w_down,
    )
    yp, k_p, v_p, c_p, n_p, m_p, conv_p = _trunk(
        x_prompt.reshape(batch * seq, D_MODEL), jnp.arange(seq), seq, ROW_TILE, WIDE_ROW_TILE, None, None, None, p)
    ys, k_s, v_s, c_s, n_s, m_s, conv_s = _trunk(
        x_sample.reshape(dec_batch * dec_seq, D_MODEL),
        jnp.tile(PAST_LEN + jnp.arange(dec_seq), dec_batch), dec_seq, dec_batch * dec_seq, dec_batch * dec_seq,
        (cache_k[0], cache_v[0]), (state_C[0], state_n[0], state_m[0]), state_conv, p)
    return (yp.reshape(batch, seq, D_MODEL), ys.reshape(dec_batch, dec_seq, D_MODEL),
            k_p, v_p, k_s, v_s, c_p, n_p, m_p, c_s, n_s, m_s, conv_p, conv_s)
```
